```python
import math
import jax, jax.numpy as jnp
from jax import lax
import numpy as np

D_MODEL = 2048
BATCH = 1
SEQ = 8192
DEPTH = 1

RWKV_HEADS = 16
RWKV_HEAD_DIM = 64
RWKV_WIDTH = RWKV_HEADS * RWKV_HEAD_DIM
DECAY_LORA = 64
AAA_LORA = 64
GATE_LORA = 128
DIFF_HEADS = 8
DIFF_QK_DIM = 64
DIFF_V_DIM = 2 * DIFF_QK_DIM
DIFF_WIDTH = DIFF_HEADS * DIFF_V_DIM
MIX_WIDTH = RWKV_WIDTH + DIFF_WIDTH
ROPE_THETA = 10000.0
Q_BLOCK = 128
RWKV_COLS = 3 * RWKV_WIDTH + DECAY_LORA + AAA_LORA + GATE_LORA
DIFF_QK_COLS = DIFF_HEADS * 2 * DIFF_QK_DIM
DIFF_COLS = 2 * DIFF_QK_COLS + DIFF_WIDTH
IN_COLS = RWKV_COLS + DIFF_COLS
MEM_LEN = 256
MEM_HEADS = 4
MEM_HEAD_DIM = D_MODEL // MEM_HEADS
N_GROUPS = 8
EXPERTS_PER_GROUP = 8
N_EXPERTS = N_GROUPS * EXPERTS_PER_GROUP
TOP_K = 2
D_EXPERT = D_MODEL // 2
EXPERT_BLOCK = 128
NORM_EPS = 1e-6
LNX_EPS = 64e-5
SUBLN_EPS = 1e-5

kernel_name = "hymba_rwkv7_diffattn_hiermoe_layer"


def rms_norm(x, g, eps=NORM_EPS):
    xf = x.astype(jnp.float32)
    y = xf * lax.rsqrt(jnp.mean(xf * xf, axis=-1, keepdims=True) + eps)
    return (y * g.astype(jnp.float32)).astype(x.dtype)


def wkv7_scan(r, w, k, v, a, b):
    B, S, H, N = r.shape

    def step(state, inp):
        r_t, w_t, k_t, v_t, a_t, b_t = inp
        sa = jnp.einsum('bhvk,bhk->bhv', state, a_t)
        state = (state * w_t[:, :, None, :] + sa[..., None] * b_t[:, :, None, :]
                 + v_t[..., None] * k_t[:, :, None, :])
        y_t = jnp.einsum('bhvk,bhk->bhv', state, r_t)
        return state, y_t

    s0 = jnp.zeros((B, H, N, N), jnp.float32)
    xs = tuple(jnp.moveaxis(t, 1, 0) for t in (r, w, k, v, a, b))
    _, ys = lax.scan(step, s0, xs)
    return jnp.moveaxis(ys, 0, 1)


def rwkv7_group(p, mu, w0, w2, a0, a2, g2, k_k, k_a, r_k, lnx_w, lnx_b):
    f32 = jnp.float32
    B, S, _ = p.shape
    H, N = RWKV_HEADS, RWKV_HEAD_DIM
    prev = jnp.pad(p, ((0, 0), (1, 0), (0, 0)))[:, :-1]
    p = (p + (prev - p) * mu).astype(f32)
    split_at = [RWKV_WIDTH, 2 * RWKV_WIDTH, 3 * RWKV_WIDTH,
                3 * RWKV_WIDTH + DECAY_LORA, 3 * RWKV_WIDTH + DECAY_LORA + AAA_LORA]
    r, k, v, wd, ad, gd = jnp.split(p, split_at, axis=-1)
    w_raw = -jax.nn.softplus(-(w0.astype(f32) + jnp.tanh(wd) @ w2.astype(f32))) - 0.5
    decay = jnp.exp(-jnp.exp(w_raw))
    a = jax.nn.sigmoid(a0.astype(f32) + ad @ a2.astype(f32))
    g = jax.nn.sigmoid(gd) @ g2.astype(f32)
    heads = lambda t: t.reshape(B, S, H, N)
    kk = heads(k * k_k.astype(f32))
    kk = kk / jnp.maximum(jnp.sqrt(jnp.sum(kk * kk, axis=-1, keepdims=True)), 1e-12)
    k = k * (1.0 + (a - 1.0) * k_a.astype(f32))
    r_h, k_h, v_h, a_h = heads(r), heads(k), heads(v), heads(a)
    y = wkv7_scan(r_h, heads(decay), k_h, v_h, -kk, kk * a_h)
    mean = jnp.mean(y, axis=-1, keepdims=True)
    var = jnp.mean((y - mean) ** 2, axis=-1, keepdims=True)
    y = ((y - mean) * lax.rsqrt(var + LNX_EPS)).reshape(B, S, RWKV_WIDTH)
    y = y * lnx_w.astype(f32) + lnx_b.astype(f32)
    bonus = jnp.sum(r_h * k_h * r_k.astype(f32), axis=-1, keepdims=True) * v_h
    y = y + bonus.reshape(B, S, RWKV_WIDTH)
    return y * g


def rope_tables(positions):
    inv_freq = ROPE_THETA ** (-(jnp.arange(0, DIFF_QK_DIM, 2, dtype=jnp.float32) / DIFF_QK_DIM))
    ang = positions.astype(jnp.float32)[..., None] * inv_freq
    return jnp.cos(ang), jnp.sin(ang)


def apply_rope(t, cos, sin):
    t1, t2 = jnp.split(t, 2, axis=-1)
    return jnp.concatenate([t1 * cos - t2 * sin, t1 * sin + t2 * cos], axis=-1)


def diff_attention_group(p, positions, lambdas, subln_w, lambda_init):
    f32 = jnp.float32
    B, S, _ = p.shape
    H, DQ, DV = DIFF_HEADS, DIFF_QK_DIM, DIFF_V_DIM
    q, k, v = jnp.split(p.astype(f32), [DIFF_QK_COLS, 2 * DIFF_QK_COLS], axis=-1)
    q = q.reshape(B, S, H, 2, DQ)
    k = k.reshape(B, S, H, 2, DQ)
    v = v.reshape(B, S, H, DV)
    cos, sin = rope_tables(positions)
    cos, sin = cos[:, :, None, None, :], sin[:, :, None, None, :]
    q = apply_rope(q, cos, sin) * (DQ ** -0.5)
    k = apply_rope(k, cos, sin)
    lam_p = lambdas.astype(f32)
    lam = (jnp.exp(jnp.sum(lam_p[0] * lam_p[1])) - jnp.exp(jnp.sum(lam_p[2] * lam_p[3]))
           + lambda_init)
    nb = S // Q_BLOCK
    qb = jnp.moveaxis(q.reshape(B, nb, Q_BLOCK, H, 2, DQ), 1, 0)
    kpos = jnp.arange(S)

    def block(args):
        q_blk, start = args
        s = jnp.einsum('bqhcd,bkhcd->bhcqk', q_blk, k)
        qpos = start + jnp.arange(Q_BLOCK)
        s = jnp.where(kpos[None, :] <= qpos[:, None], s, -jnp.inf)
        pr = jax.nn.softmax(s, axis=-1)
        attn = pr[:, :, 0] - lam * pr[:, :, 1]
        return jnp.einsum('bhqk,bkhe->bqhe', attn, v)

    o = lax.map(block, (qb, jnp.arange(nb) * Q_BLOCK))
    o = jnp.moveaxis(o, 0, 1).reshape(B, S, H, DV)
    o = o * lax.rsqrt(jnp.mean(o * o, axis=-1, keepdims=True) + SUBLN_EPS)
    o = o * subln_w.astype(f32) * (1.0 - lambda_init)
    return o.reshape(B, S, DIFF_WIDTH)


def memory_cross_attention(h, memn, wq, wk, wv, wo):
    f32 = jnp.float32
    B, S, D = h.shape
    M = memn.shape[1]
    q = (h @ wq).reshape(B, S, MEM_HEADS, MEM_HEAD_DIM).astype(f32)
    k = (memn @ wk).reshape(B, M, MEM_HEADS, MEM_HEAD_DIM).astype(f32)
    v = (memn @ wv).reshape(B, M, MEM_HEADS, MEM_HEAD_DIM).astype(f32)
    s = jnp.einsum('bqhd,bkhd->bhqk', q, k) * (MEM_HEAD_DIM ** -0.5)
    pr = jax.nn.softmax(s, axis=-1)
    o = jnp.einsum('bhqk,bkhd->bqhd', pr, v).reshape(B, S, D).astype(h.dtype)
    return o @ wo


def hierarchical_moe(h, wg_r, bg_r, we_r, be_r, w_gate, w_up, w_down):
    f32 = jnp.float32
    B, S, D = h.shape
    Ntok = B * S
    hf = h.reshape(Ntok, D)
    hf32 = hf.astype(f32)
    g_logits = hf32 @ wg_r.astype(f32) + bg_r.astype(f32)
    g_prob = jax.nn.softmax(g_logits, axis=-1)
    g_idx = jnp.argmax(g_logits, axis=-1)
    g_w = jnp.take_along_axis(g_prob, g_idx[:, None], axis=-1)[:, 0]
    e_logits = (hf32 @ we_r.astype(f32) + be_r.astype(f32)).reshape(Ntok, N_GROUPS, EXPERTS_PER_GROUP)
    e_logits = jnp.take_along_axis(e_logits, g_idx[:, None, None], axis=1)[:, 0]
    e_prob = jax.nn.softmax(e_logits, axis=-1)
    top_p, top_i = lax.top_k(e_prob, TOP_K)
    top_p = top_p / jnp.sum(top_p, axis=-1, keepdims=True)
    weights = g_w[:, None] * top_p
    expert_id = g_idx[:, None] * EXPERTS_PER_GROUP + top_i

    A = Ntok * TOP_K
    flat_e = expert_id.reshape(A).astype(jnp.int32)
    flat_tok = jnp.repeat(jnp.arange(Ntok, dtype=jnp.int32), TOP_K)
    flat_w = weights.reshape(A)
    order = jnp.argsort(flat_e, stable=True)
    sorted_e = flat_e[order]
    counts = jnp.bincount(flat_e, length=N_EXPERTS)
    padded_counts = ((counts + EXPERT_BLOCK - 1) // EXPERT_BLOCK) * EXPERT_BLOCK
    padded_end = jnp.cumsum(padded_counts)
    padded_start = padded_end - padded_counts
    start = jnp.cumsum(counts) - counts
    rank = jnp.arange(A) - start[sorted_e]
    dest = padded_start[sorted_e] + rank
    n_blocks = -(-(A + N_EXPERTS * (EXPERT_BLOCK - 1)) // EXPERT_BLOCK)
    P = n_blocks * EXPERT_BLOCK
    buf_tok = jnp.zeros((P,), jnp.int32).at[dest].set(flat_tok[order])
    buf_w = jnp.zeros((P,), f32).at[dest].set(flat_w[order])
    block_start = jnp.arange(n_blocks) * EXPERT_BLOCK
    block_e = jnp.minimum(jnp.searchsorted(padded_end, block_start, side='right'), N_EXPERTS - 1)

    def run_block(args):
        tok, w, e = args
        xb = hf[tok]
        hid = jax.nn.silu(xb @ w_gate[e]) * (xb @ w_up[e])
        return (hid @ w_down[e]).astype(f32) * w[:, None]

    ys = lax.map(run_block, (buf_tok.reshape(n_blocks, EXPERT_BLOCK),
                             buf_w.reshape(n_blocks, EXPERT_BLOCK), block_e))
    out = jnp.zeros((Ntok, D), f32).at[buf_tok].add(ys.reshape(P, D))
    return out.reshape(B, S, D).astype(h.dtype)


def setup_inputs(seed: int = 0) -> dict:
    key = jax.random.key(seed)
    ks = iter(jax.random.split(key, 48))
    f32 = jnp.float32
    L, D = DEPTH, D_MODEL

    def nrm(shape, scale):
        return jax.random.normal(next(ks), shape, f32) * scale

    def unif(shape, lo, hi):
        return jax.random.uniform(next(ks), shape, f32, lo, hi)

    return {
        "x": nrm((BATCH, SEQ, D), 1.0),
        "mem": nrm((BATCH, MEM_LEN, D), 1.0),
        "positions": jnp.broadcast_to(jnp.arange(SEQ, dtype=jnp.int32), (BATCH, SEQ)),
        "mix_norm": 1.0 + nrm((L, D), 0.02),
        "w_in": nrm((L, D, IN_COLS), D ** -0.5),
        "shift_mu": unif((L, RWKV_COLS), 0.0, 1.0),
        "decay_w0": unif((L, RWKV_WIDTH), -5.0, 1.0),
        "decay_w2": nrm((L, DECAY_LORA, RWKV_WIDTH), 0.1),
        "aaa_a0": nrm((L, RWKV_WIDTH), 0.1),
        "aaa_a2": nrm((L, AAA_LORA, RWKV_WIDTH), AAA_LORA ** -0.5),
        "gate_g2": nrm((L, GATE_LORA, RWKV_WIDTH), GATE_LORA ** -0.5),
        "k_k": 0.85 + nrm((L, RWKV_WIDTH), 0.02),
        "k_a": 1.0 + nrm((L, RWKV_WIDTH), 0.02),
        "r_k": nrm((L, RWKV_HEADS, RWKV_HEAD_DIM), 0.1),
        "lnx_w": 1.0 + nrm((L, RWKV_WIDTH), 0.02),
        "lnx_b": nrm((L, RWKV_WIDTH), 0.02),
        "diff_lambda": nrm((L, 4, DIFF_QK_DIM), 0.1),
        "subln_w": 1.0 + nrm((L, DIFF_V_DIM), 0.02),
        "w_out": nrm((L, MIX_WIDTH, D), MIX_WIDTH ** -0.5),
        "mem_q_norm": 1.0 + nrm((L, D), 0.02),
        "mem_kv_norm": 1.0 + nrm((L, D), 0.02),
        "wq_mem": nrm((L, D, D), D ** -0.5),
        "wk_mem": nrm((L, D, D), D ** -0.5),
        "wv_mem": nrm((L, D, D), D ** -0.5),
        "wo_mem": nrm((L, D, D), D ** -0.5),
        "moe_norm": 1.0 + nrm((L, D), 0.02),
        "router_group_w": nrm((L, D, N_GROUPS), D ** -0.5),
        "router_group_b": nrm((L, N_GROUPS), 0.01),
        "router_expert_w": nrm((L, D, N_EXPERTS), D ** -0.5),
        "router_expert_b": nrm((L, N_EXPERTS), 0.01),
        "expert_gate": nrm((L, N_EXPERTS, D, D_EXPERT), D ** -0.5),
        "expert_up": nrm((L, N_EXPERTS, D, D_EXPERT), D ** -0.5),
        "expert_down": nrm((L, N_EXPERTS, D_EXPERT, D), D_EXPERT ** -0.5),
        "final_norm": 1.0 + nrm((D,), 0.02),
    }


def reference(x, mem, positions, mix_norm, w_in, shift_mu, decay_w0, decay_w2, aaa_a0, aaa_a2,
              gate_g2, k_k, k_a, r_k, lnx_w, lnx_b, diff_lambda, subln_w, w_out,
              mem_q_norm, mem_kv_norm, wq_mem, wk_mem, wv_mem, wo_mem,
              moe_norm, router_group_w, router_group_b, router_expert_w, router_expert_b,
              expert_gate, expert_up, expert_down, final_norm):
    h = x
    for l in range(DEPTH):
        lambda_init = 0.8 - 0.6 * math.exp(-0.3 * l)
        xn = rms_norm(h, mix_norm[l])
        proj = xn @ w_in[l]
        y_rwkv = rwkv7_group(proj[..., :RWKV_COLS], shift_mu[l], decay_w0[l], decay_w2[l],
                             aaa_a0[l], aaa_a2[l], gate_g2[l], k_k[l], k_a[l], r_k[l],
                             lnx_w[l], lnx_b[l])
        y_diff = diff_attention_group(proj[..., RWKV_COLS:], positions, diff_lambda[l],
                                      subln_w[l], lambda_init)
        mixed = jnp.concatenate([y_rwkv, y_diff], axis=-1).astype(h.dtype)
        h = h + mixed @ w_out[l]
        h = h + memory_cross_attention(rms_norm(h, mem_q_norm[l]), rms_norm(mem, mem_kv_norm[l]),
                                       wq_mem[l], wk_mem[l], wv_mem[l], wo_mem[l])
        h = h + hierarchical_moe(rms_norm(h, moe_norm[l]), router_group_w[l], router_group_b[l],
                                 router_expert_w[l], router_expert_b[l],
                                 expert_gate[l], expert_up[l], expert_down[l])
    return rms_norm(h, final_norm)
```

```python
import functools
import math

import numpy as np
import jax
import jax.numpy as jnp
from jax import lax
from jax.experimental import pallas as pl
from jax.experimental.pallas import tpu as pltpu

F32 = jnp.float32
BF16 = jnp.bfloat16
HIGHEST = lax.Precision.HIGHEST

RWKV_HEADS = 16
HEAD_N = 64
RWKV_WIDTH = RWKV_HEADS * HEAD_N
DECAY_LORA = 64
AAA_LORA = 64
GATE_LORA = 128
RWKV_COLS = 3 * RWKV_WIDTH + DECAY_LORA + AAA_LORA + GATE_LORA
DIFF_HEADS = 8
DIFF_QK = 64
DIFF_V = 128
DIFF_WIDTH = DIFF_HEADS * DIFF_V
ROPE_THETA = 10000.0
MEM_HEADS = 4
N_GROUPS = 8
EXPERTS_PER_GROUP = 8
N_EXPERTS = N_GROUPS * EXPERTS_PER_GROUP
TOP_K = 2
NORM_EPS = 1e-6
LNX_EPS = 64e-5
SUBLN_EPS = 1e-5

LANES = 128
VMEM_LIMIT = 48 * 1024 * 1024

CHUNK = 64
SCAN_ROWS = 256
MOE_ROWS = 512
MOE_SUB = 128
MOE_SPLIT = 4


def _params(*sem):
    return pltpu.CompilerParams(dimension_semantics=sem, vmem_limit_bytes=VMEM_LIMIT)


def _mm_kernel(*refs, has_norm, has_res):
    it = iter(refs)
    x_ref = next(it)
    g_ref = next(it) if has_norm else None
    w_ref = next(it)
    r_ref = next(it) if has_res else None
    o_ref = next(it)
    xs_ref = next(it)

    @pl.when(pl.program_id(1) == 0)
    def _():
        x = x_ref[...].astype(F32)
        if has_norm:
            x = x * lax.rsqrt(jnp.mean(x * x, axis=-1, keepdims=True) + NORM_EPS) * g_ref[...]
        xs_ref[...] = x.astype(BF16)

    acc = jnp.dot(xs_ref[...], w_ref[...], preferred_element_type=F32)
    if has_res:
        acc = acc + r_ref[...]
    o_ref[...] = acc.astype(o_ref.dtype)


def matmul(x, w, *, gain=None, residual=None, out_dtype=F32, tm=512, tn=512):
    M, K = x.shape
    N = w.shape[1]
    tm = min(tm, M)
    assert M % tm == 0 and N % tn == 0, (M, N, tm, tn)
    in_specs = [pl.BlockSpec((tm, K), lambda i, j: (i, 0))]
    args = [x]
    if gain is not None:
        in_specs.append(pl.BlockSpec((1, K), lambda i, j: (0, 0)))
        args.append(gain.reshape(1, K).astype(F32))
    in_specs.append(pl.BlockSpec((K, tn), lambda i, j: (0, j)))
    args.append(w)
    if residual is not None:
        in_specs.append(pl.BlockSpec((tm, tn), lambda i, j: (i, j)))
        args.append(residual)
    return pl.pallas_call(
        functools.partial(_mm_kernel, has_norm=gain is not None, has_res=residual is not None),
        out_shape=jax.ShapeDtypeStruct((M, N), out_dtype),
        grid=(M // tm, N // tn),
        in_specs=in_specs,
        out_specs=pl.BlockSpec((tm, tn), lambda i, j: (i, j)),
        scratch_shapes=[pltpu.VMEM((tm, K), BF16)],
        compiler_params=_params("parallel", "arbitrary"),
        name="matmul",
    )(*args)


def _sigmoid(x):
    return 1.0 / (1.0 + jnp.exp(-x))


def _rwkv_prep_kernel(p_ref, pp_ref, mu_ref, w0_ref, w2_ref, a0_ref, a2_ref, g2_ref,
                      r_o, k_o, v_o, lw_o, a_o, g_o):
    W = RWKV_WIDTH
    p = p_ref[...]
    last = jnp.where(pl.program_id(0) == 0, 0.0, pp_ref[7:8, :])
    prev = pltpu.roll(p, 1, axis=0)
    row = lax.broadcasted_iota(jnp.int32, p.shape, 0)
    prev = jnp.where(row == 0, last, prev)
    ps = p + (prev - p) * mu_ref[...]
    r_o[...] = ps[:, 0:W]
    k_o[...] = ps[:, W:2 * W]
    v_o[...] = ps[:, 2 * W:3 * W]
    o = 3 * W
    wd = ps[:, o:o + DECAY_LORA]
    ad = ps[:, o + DECAY_LORA:o + DECAY_LORA + AAA_LORA]
    gd = ps[:, o + DECAY_LORA + AAA_LORA:o + DECAY_LORA + AAA_LORA + GATE_LORA]
    z = w0_ref[...] + jnp.dot(jnp.tanh(wd).astype(BF16), w2_ref[...], preferred_element_type=F32)
    nz = -z
    softplus = jnp.maximum(nz, 0.0) + jnp.log(1.0 + jnp.exp(-jnp.abs(nz)))
    w_raw = -softplus - 0.5
    lw_o[...] = -jnp.exp(w_raw)
    a_o[...] = _sigmoid(a0_ref[...] + jnp.dot(ad.astype(BF16), a2_ref[...], preferred_element_type=F32))
    g_o[...] = jnp.dot(_sigmoid(gd).astype(BF16), g2_ref[...], preferred_element_type=F32)


def rwkv_prep(p, mu, w0, w2, a0, a2, g2, *, tm=256):
    S = p.shape[0]
    tm = min(tm, S)
    W = RWKV_WIDTH
    row = lambda x: x.reshape(1, -1).astype(F32)
    full = lambda a: pl.BlockSpec(a.shape, lambda i: (0, 0))
    args = [p, p, row(mu), row(w0), w2.astype(BF16), row(a0), a2.astype(BF16), g2.astype(BF16)]
    in_specs = [pl.BlockSpec((tm, RWKV_COLS), lambda i: (i, 0)),
                pl.BlockSpec((8, RWKV_COLS), lambda i: (jnp.maximum(i * (tm // 8) - 1, 0), 0))]
    in_specs += [full(a) for a in args[2:]]
    out = jax.ShapeDtypeStruct((S, W), F32)
    return pl.pallas_call(
        _rwkv_prep_kernel,
        out_shape=[out] * 6,
        grid=(S // tm,),
        in_specs=in_specs,
        out_specs=[pl.BlockSpec((tm, W), lambda i: (i, 0))] * 6,
        compiler_params=_params("parallel"),
        name="rwkv_prep",
    )(*args)


def _dot_nt(a, b, precision=None):
    return lax.dot_general(a, b, (((1,), (1,)), ((), ())), preferred_element_type=F32, precision=precision)


def _dot_tn(a, b, precision=None):
    return lax.dot_general(a, b, (((0,), (0,)), ((), ())), preferred_element_type=F32, precision=precision)


def _dot(a, b, precision=None):
    return jnp.dot(a, b, preferred_element_type=F32, precision=precision)


def _unit_lower_inverse(n_strict, row, col):
    C = n_strict.shape[0]
    eye = (row == col).astype(F32)
    t = eye
    b = 1
    while b < C:
        sh = (2 * b).bit_length() - 1
        same_pair = (row >> sh) == (col >> sh)
        low_left = same_pair & ((row & b) != 0) & ((col & b) == 0)
        nb = jnp.where(low_left, n_strict, 0.0)
        if b == 1:
            t = eye + nb
        else:
            t = t + _dot(t, _dot(nb, t, HIGHEST), HIGHEST)
        b *= 2
    return t


def _rwkv_scan_kernel(r_ref, k_ref, v_ref, lw_ref, a_ref, g_ref, kk_ref, ka_ref, rk_ref, lnw_ref, lnb_ref,
                      y_ref, state_ref):
    C = CHUNK
    N = HEAD_N

    @pl.when(pl.program_id(1) == 0)
    def _():
        state_ref[...] = jnp.zeros_like(state_ref)

    row = lax.broadcasted_iota(jnp.int32, (C, C), 0)
    col = lax.broadcasted_iota(jnp.int32, (C, C), 1)
    tri_incl = (row >= col).astype(F32)
    strict = row > col
    incl = row >= col

    n_chunks = r_ref.shape[0] // C
    heads_per_step = r_ref.shape[1] // N
    for j in range(heads_per_step):
        ls = slice(j * N, (j + 1) * N)
        k_k, k_a, r_k = kk_ref[:, ls], ka_ref[:, ls], rk_ref[:, ls]
        ln_w, ln_b = lnw_ref[:, ls], lnb_ref[:, ls]
        s = state_ref[j]
        for q in range(n_chunks):
            rs = slice(q * C, (q + 1) * C)
            r = r_ref[rs, ls]
            k = k_ref[rs, ls]
            v = v_ref[rs, ls]
            lw = lw_ref[rs, ls]
            a = a_ref[rs, ls]
            g = g_ref[rs, ls]
            kk = k * k_k
            kk = kk / jnp.maximum(jnp.sqrt(jnp.sum(kk * kk, axis=-1, keepdims=True)), 1e-12)
            k2 = k * (1.0 + (a - 1.0) * k_a)
            cs = _dot(tri_incl, lw, HIGHEST)
            e_neg = jnp.exp(-cs)
            at = -kk * jnp.exp(cs - lw)
            bt = (kk * a) * e_neg
            kt = k2 * e_neg
            rt = r * jnp.exp(cs)
            pc = jnp.exp(cs[C - 1:C, :])
            a_ab = jnp.where(strict, _dot_nt(at, bt), 0.0)
            a_ak = jnp.where(strict, _dot_nt(at, kt), 0.0)
            a_rb = jnp.where(incl, _dot_nt(rt, bt), 0.0)
            a_rk = jnp.where(incl, _dot_nt(rt, kt), 0.0)
            t = _unit_lower_inverse(a_ab, row, col)
            u = _dot(t, _dot_nt(at, s) + _dot(a_ak, v))
            y = _dot_nt(rt, s) + _dot(a_rb, u) + _dot(a_rk, v)
            s = s * pc + _dot_tn(u, bt * pc) + _dot_tn(v, kt * pc)
            mean = jnp.mean(y, axis=-1, keepdims=True)
            yc = y - mean
            var = jnp.mean(yc * yc, axis=-1, keepdims=True)
            yn = yc * lax.rsqrt(var + LNX_EPS) * ln_w + ln_b
            bonus = jnp.sum(r * k2 * r_k, axis=-1, keepdims=True) * v
            y_ref[rs, ls] = (yn + bonus) * g
        state_ref[j] = s


def rwkv_scan(r, k, v, lw, a, g, k_k, k_a, r_k, lnx_w, lnx_b, *, rows=SCAN_ROWS):
    S, W = r.shape
    rows = min(rows, S)
    hp = LANES // HEAD_N
    row = lambda x: x.reshape(1, W).astype(F32)
    seq = pl.BlockSpec((rows, LANES), lambda h, c: (c, h))
    par = pl.BlockSpec((1, LANES), lambda h, c: (0, h))
    return pl.pallas_call(
        _rwkv_scan_kernel,
        out_shape=jax.ShapeDtypeStruct((S, W), F32),
        grid=(W // LANES, S // rows),
        in_specs=[seq] * 6 + [par] * 5,
        out_specs=seq,
        scratch_shapes=[pltpu.VMEM((hp, HEAD_N, HEAD_N), F32)],
        compiler_params=_params("parallel", "arbitrary"),
        name="rwkv_scan",
    )(r, k, v, lw, a, g, row(k_k), row(k_a), row(r_k), row(lnx_w), row(lnx_b))


def _rope_kernel(q_ref, k_ref, v_ref, pos_ref, freq_ref, qo_ref, ko_ref, vo_ref):
    ang = pos_ref[...] * freq_ref[...]
    lane = lax.broadcasted_iota(jnp.int32, ang.shape, 1)
    first_half = (lane % DIFF_QK) < (DIFF_QK // 2)
    cos = jnp.cos(ang)
    sin = jnp.where(first_half, -1.0, 1.0) * jnp.sin(ang)
    half = DIFF_QK // 2
    scale = DIFF_QK ** -0.5
    for b in range(q_ref.shape[1] // LANES):
        ls = slice(b * LANES, (b + 1) * LANES)
        for src, dst, sc in ((q_ref, qo_ref, scale), (k_ref, ko_ref, 1.0)):
            x = src[:, ls]
            partner = jnp.where(first_half, pltpu.roll(x, LANES - half, axis=1), pltpu.roll(x, half, axis=1))
            dst[:, ls] = ((x * cos + partner * sin) * sc).astype(dst.dtype)
    vo_ref[...] = v_ref[...].astype(vo_ref.dtype)


def rope(proj_d, positions, *, tm=256):
    S = proj_d.shape[0]
    tm = min(tm, S)
    W = DIFF_WIDTH
    inv_freq = ROPE_THETA ** (-(jnp.arange(0, DIFF_QK, 2, dtype=F32) / DIFF_QK))
    freq = jnp.tile(inv_freq, LANES // (DIFF_QK // 2)).reshape(1, LANES)
    pos = positions.reshape(S, 1).astype(F32)
    out = jax.ShapeDtypeStruct((S, W), BF16)
    return pl.pallas_call(
        _rope_kernel,
        out_shape=[out] * 3,
        grid=(S // tm,),
        in_specs=[pl.BlockSpec((tm, W), lambda i: (i, 0)),
                  pl.BlockSpec((tm, W), lambda i: (i, 1)),
                  pl.BlockSpec((tm, W), lambda i: (i, 2)),
                  pl.BlockSpec((tm, 1), lambda i: (i, 0)),
                  pl.BlockSpec((1, LANES), lambda i: (0, 0))],
        out_specs=[pl.BlockSpec((tm, W), lambda i: (i, 0))] * 3,
        compiler_params=_params("parallel"),
        name="rope",
    )(proj_d, proj_d, proj_d, pos, freq)


def _diff_flash_kernel(qi_tab, ki_tab, q_ref, k_ref, v_ref, lam_ref, sw_ref, o_ref, m_ref, l_ref, acc_ref,
                       *, tq, tk, lambda_init):
    p = pl.program_id(1)
    qi = qi_tab[p]
    ki = ki_tab[p]
    last_ki = ((qi + 1) * tq - 1) // tk

    @pl.when(ki == 0)
    def _():
        m_ref[...] = jnp.full_like(m_ref, -jnp.inf)
        l_ref[...] = jnp.zeros_like(l_ref)
        acc_ref[...] = jnp.zeros_like(acc_ref)

    qpos = qi * tq + lax.broadcasted_iota(jnp.int32, (tq, tk), 0)
    kpos = ki * tk + lax.broadcasted_iota(jnp.int32, (tq, tk), 1)
    causal = kpos <= qpos
    v = v_ref[...]
    for c in range(2):
        ls = slice(c * DIFF_QK, (c + 1) * DIFF_QK)
        s = _dot_nt(q_ref[:, ls], k_ref[:, ls])
        s = jnp.where(causal, s, -jnp.inf)
        m_old = m_ref[c]
        m_new = jnp.maximum(m_old, jnp.max(s, axis=-1, keepdims=True))
        alpha = jnp.exp(m_old - m_new)
        pr = jnp.exp(s - m_new[:, 0:1])
        l_ref[c] = alpha * l_ref[c] + jnp.sum(pr, axis=-1, keepdims=True)
        acc_ref[c] = alpha * acc_ref[c] + _dot(pr.astype(BF16), v)
        m_ref[c] = m_new

    @pl.when(ki == last_ki)
    def _():
        lp = lam_ref[...]
        lam = (jnp.exp(jnp.sum(lp[0:1] * lp[1:2], axis=-1, keepdims=True))
               - jnp.exp(jnp.sum(lp[2:3] * lp[3:4], axis=-1, keepdims=True)) + lambda_init)
        o = acc_ref[0] / l_ref[0] - lam * (acc_ref[1] / l_ref[1])
        o = o * lax.rsqrt(jnp.mean(o * o, axis=-1, keepdims=True) + SUBLN_EPS)
        o_ref[...] = o * sw_ref[...] * (1.0 - lambda_init)


def diff_flash(q, k, v, lambdas, subln_w, lambda_init, *, tq=1024, tk=512):
    S = q.shape[0]
    tq = min(tq, S)
    tk = min(tk, S)
    pairs = [(qi, ki) for qi in range(S // tq) for ki in range(((qi + 1) * tq - 1) // tk + 1)]
    qi_tab = jnp.asarray([p[0] for p in pairs], jnp.int32)
    ki_tab = jnp.asarray([p[1] for p in pairs], jnp.int32)
    grid_spec = pltpu.PrefetchScalarGridSpec(
        num_scalar_prefetch=2,
        grid=(DIFF_HEADS, len(pairs)),
        in_specs=[pl.BlockSpec((tq, LANES), lambda h, p, qt, kt: (qt[p], h)),
                  pl.BlockSpec((tk, LANES), lambda h, p, qt, kt: (kt[p], h)),
                  pl.BlockSpec((tk, LANES), lambda h, p, qt, kt: (kt[p], h)),
                  pl.BlockSpec((4, DIFF_QK), lambda h, p, qt, kt: (0, 0)),
                  pl.BlockSpec((1, DIFF_V), lambda h, p, qt, kt: (0, 0))],
        out_specs=pl.BlockSpec((tq, LANES), lambda h, p, qt, kt: (qt[p], h)),
        scratch_shapes=[pltpu.VMEM((2, tq, LANES), F32),
                        pltpu.VMEM((2, tq, LANES), F32),
                        pltpu.VMEM((2, tq, DIFF_V), F32)],
    )
    return pl.pallas_call(
        functools.partial(_diff_flash_kernel, tq=tq, tk=tk, lambda_init=lambda_init),
        out_shape=jax.ShapeDtypeStruct((S, DIFF_WIDTH), F32),
        grid_spec=grid_spec,
        compiler_params=_params("parallel", "arbitrary"),
        name="diff_flash",
    )(qi_tab, ki_tab, q, k, v, lambdas.astype(F32), subln_w.reshape(1, DIFF_V).astype(F32))


def _mem_attn_kernel(q_ref, k_ref, v_ref, o_ref):
    D = q_ref.shape[1]
    hd = D // MEM_HEADS
    scale = hd ** -0.5
    for h in range(MEM_HEADS):
        ls = slice(h * hd, (h + 1) * hd)
        s = _dot_nt(q_ref[:, ls], k_ref[:, ls]) * scale
        s = s - jnp.max(s, axis=-1, keepdims=True)
        e = jnp.exp(s)
        pr = e / jnp.sum(e, axis=-1, keepdims=True)
        o_ref[:, ls] = _dot(pr.astype(BF16), v_ref[:, ls]).astype(o_ref.dtype)


def mem_attn(q, k, v, *, tq=512):
    S, D = q.shape
    M = k.shape[0]
    tq = min(tq, S)
    return pl.pallas_call(
        _mem_attn_kernel,
        out_shape=jax.ShapeDtypeStruct((S, D), BF16),
        grid=(S // tq,),
        in_specs=[pl.BlockSpec((tq, D), lambda i: (i, 0)),
                  pl.BlockSpec((M, D), lambda i: (0, 0)),
                  pl.BlockSpec((M, D), lambda i: (0, 0))],
        out_specs=pl.BlockSpec((tq, D), lambda i: (i, 0)),
        compiler_params=_params("parallel"),
        name="mem_attn",
    )(q, k, v)


def _first_argmax(x, lane, big):
    m = jnp.max(x, axis=-1, keepdims=True)
    idx = jnp.min(jnp.where(x == m, lane, big), axis=-1, keepdims=True)
    return m, idx


def _router_kernel(h_ref, gain_ref, wg_ref, bg_ref, we_ref, be_ref, hn_ref, eid_ref, ew_ref):
    x = h_ref[...]
    hn = x * lax.rsqrt(jnp.mean(x * x, axis=-1, keepdims=True) + NORM_EPS) * gain_ref[...]
    hn_ref[...] = hn
    g_logits = _dot(hn, wg_ref[...], HIGHEST) + bg_ref[...]
    e_logits = _dot(hn, we_ref[...], HIGHEST) + be_ref[...]
    tm = x.shape[0]
    lane_g = lax.broadcasted_iota(jnp.int32, (tm, N_GROUPS), 1)
    g_max, g_idx = _first_argmax(g_logits, lane_g, N_GROUPS)
    g_w = 1.0 / jnp.sum(jnp.exp(g_logits - g_max), axis=-1, keepdims=True)
    lane_e = lax.broadcasted_iota(jnp.int32, (tm, N_EXPERTS), 1)
    in_group = (lane_e // EXPERTS_PER_GROUP) == g_idx
    el = jnp.where(in_group, e_logits, -jnp.inf)
    e_max = jnp.max(el, axis=-1, keepdims=True)
    ex = jnp.exp(el - e_max)
    prob = ex / jnp.sum(ex, axis=-1, keepdims=True)
    prob = jnp.where(in_group, prob, -1.0)
    p1, i1 = _first_argmax(prob, lane_e, N_EXPERTS)
    p2, i2 = _first_argmax(jnp.where(lane_e == i1, -1.0, prob), lane_e, N_EXPERTS)
    tot = p1 + p2
    lane_o = lax.broadcasted_iota(jnp.int32, (tm, LANES), 1)
    eid_ref[...] = jnp.where(lane_o == 0, i1, jnp.where(lane_o == 1, i2, 0))
    ew_ref[...] = jnp.where(lane_o == 0, g_w * (p1 / tot), jnp.where(lane_o == 1, g_w * (p2 / tot), 0.0))


def router(h, gain, wg, bg, we, be, *, tm=512):
    S, D = h.shape
    tm = min(tm, S)
    full = lambda a: pl.BlockSpec(a.shape, lambda i: (0, 0))
    args = [h, gain.reshape(1, D).astype(F32), wg.astype(F32), bg.reshape(1, -1).astype(F32),
            we.astype(F32), be.reshape(1, -1).astype(F32)]
    hn, eid, ew = pl.pallas_call(
        _router_kernel,
        out_shape=[jax.ShapeDtypeStruct((S, D), F32),
                   jax.ShapeDtypeStruct((S, LANES), jnp.int32),
                   jax.ShapeDtypeStruct((S, LANES), F32)],
        grid=(S // tm,),
        in_specs=[pl.BlockSpec((tm, D), lambda i: (i, 0))] + [full(a) for a in args[1:]],
        out_specs=[pl.BlockSpec((tm, D), lambda i: (i, 0)),
                   pl.BlockSpec((tm, LANES), lambda i: (i, 0)),
                   pl.BlockSpec((tm, LANES), lambda i: (i, 0))],
        compiler_params=_params("parallel"),
        name="router",
    )(*args)
    return hn, eid[:, :TOP_K], ew[:, :TOP_K]


def _rank_kernel(e_ref, rank_ref, cnt_ref, carry_ref):
    @pl.when(pl.program_id(0) == 0)
    def _():
        carry_ref[...] = jnp.zeros_like(carry_ref)

    e = e_ref[...]
    tm = e.shape[0]
    onehot = (lax.broadcasted_iota(jnp.int32, (tm, LANES), 1) == e).astype(BF16)
    r = lax.broadcasted_iota(jnp.int32, (tm, tm), 0)
    c = lax.broadcasted_iota(jnp.int32, (tm, tm), 1)
    before = _dot((r > c).astype(BF16), onehot) + carry_ref[...]
    rank_ref[...] = jnp.sum(jnp.where(onehot > 0, before, 0.0), axis=-1, keepdims=True).astype(jnp.int32)
    carry_ref[...] = carry_ref[...] + jnp.sum(onehot.astype(F32), axis=0, keepdims=True)
    cnt_ref[...] = carry_ref[...].astype(jnp.int32)


def expert_rank(flat_e, *, tm=512):
    A = flat_e.shape[0]
    tm = min(tm, A)
    rank, cnt = pl.pallas_call(
        _rank_kernel,
        out_shape=[jax.ShapeDtypeStruct((A, 1), jnp.int32), jax.ShapeDtypeStruct((1, LANES), jnp.int32)],
        grid=(A // tm,),
        in_specs=[pl.BlockSpec((tm, 1), lambda i: (i, 0))],
        out_specs=[pl.BlockSpec((tm, 1), lambda i: (i, 0)), pl.BlockSpec((1, LANES), lambda i: (0, 0))],
        scratch_shapes=[pltpu.VMEM((1, LANES), F32)],
        compiler_params=_params("arbitrary"),
        name="expert_rank",
    )(flat_e.reshape(A, 1))
    return rank[:, 0], cnt[0, :N_EXPERTS]


def _experts_kernel(blk_e, blk_n, src_row, dst_row, hn_hbm, w_ref, wg_ref, wu_ref, wd_ref, out_hbm,
                    xg_ref, xb_ref, acc_ref, gsem, ssem):
    i = pl.program_id(0)
    j = pl.program_id(1)
    nsplit = pl.num_programs(1)
    n_valid = blk_n[i]
    n_sub = (n_valid + (MOE_SUB - 1)) // MOE_SUB
    n_rows = n_sub * MOE_SUB
    base = i * MOE_ROWS

    def gather(r):
        return pltpu.make_async_copy(hn_hbm.at[pl.ds(src_row[base + r], 1)], xg_ref.at[pl.ds(r, 1)], gsem)

    def scatter(r):
        return pltpu.make_async_copy(acc_ref.at[pl.ds(r, 1)], out_hbm.at[pl.ds(dst_row[base + r], 1)], ssem)

    @pl.when(j == 0)
    def _():
        def start(r, _):
            gather(r).start()
            return 0
        lax.fori_loop(0, n_rows, start, 0)

        def wait(r, _):
            gather(r).wait()
            return 0
        lax.fori_loop(0, n_rows, wait, 0)

        def cast(sb, _):
            rs = pl.ds(pl.multiple_of(sb * MOE_SUB, MOE_SUB), MOE_SUB)
            xb_ref[rs, :] = xg_ref[rs, :].astype(BF16)
            return 0
        lax.fori_loop(0, n_sub, cast, 0)

    wg = wg_ref[0].astype(BF16)
    wu = wu_ref[0].astype(BF16)
    wd = wd_ref[0].astype(BF16)

    def sub_block(sb, _):
        rs = pl.ds(pl.multiple_of(sb * MOE_SUB, MOE_SUB), MOE_SUB)
        xb = xb_ref[rs, :]
        gate = _dot(xb, wg)
        hid = (gate * _sigmoid(gate)) * _dot(xb, wu)
        part = _dot(hid.astype(BF16), wd)

        @pl.when(j == 0)
        def _():
            acc_ref[rs, :] = part

        @pl.when(j > 0)
        def _():
            acc_ref[rs, :] = acc_ref[rs, :] + part
        return 0
    lax.fori_loop(0, n_sub, sub_block, 0)

    @pl.when(j == nsplit - 1)
    def _():
        def scale(sb, _):
            rs = pl.ds(pl.multiple_of(sb * MOE_SUB, MOE_SUB), MOE_SUB)
            acc_ref[rs, :] = acc_ref[rs, :] * w_ref[rs, :]
            return 0
        lax.fori_loop(0, n_sub, scale, 0)

        def start(r, _):
            scatter(r).start()
            return 0
        lax.fori_loop(0, n_valid, start, 0)

        def wait(r, _):
            scatter(r).wait()
            return 0
        lax.fori_loop(0, n_valid, wait, 0)


def experts(hn, blk_e, blk_n, src_row, dst_row, row_w, w_gate, w_up, w_down, n_out_rows):
    S, D = hn.shape
    E, _, DE = w_gate.shape
    nb = blk_e.shape[0]
    de = DE // MOE_SPLIT
    split = lambda i, j, bn: jnp.where(bn[i] > 0, j, MOE_SPLIT - 1)
    grid_spec = pltpu.PrefetchScalarGridSpec(
        num_scalar_prefetch=4,
        grid=(nb, MOE_SPLIT),
        in_specs=[pl.BlockSpec(memory_space=pl.ANY),
                  pl.BlockSpec((MOE_ROWS, 1), lambda i, j, be, bn, sr, dr: (i, 0)),
                  pl.BlockSpec((1, D, de), lambda i, j, be, bn, sr, dr: (be[i], 0, split(i, j, bn))),
                  pl.BlockSpec((1, D, de), lambda i, j, be, bn, sr, dr: (be[i], 0, split(i, j, bn))),
                  pl.BlockSpec((1, de, D), lambda i, j, be, bn, sr, dr: (be[i], split(i, j, bn), 0))],
        out_specs=pl.BlockSpec(memory_space=pl.ANY),
        scratch_shapes=[pltpu.VMEM((MOE_ROWS, D), F32),
                        pltpu.VMEM((MOE_ROWS, D), BF16),
                        pltpu.VMEM((MOE_ROWS, D), F32),
                        pltpu.SemaphoreType.DMA(()),
                        pltpu.SemaphoreType.DMA(())],
    )
    return pl.pallas_call(
        _experts_kernel,
        out_shape=jax.ShapeDtypeStruct((n_out_rows, D), F32),
        grid_spec=grid_spec,
        compiler_params=_params("arbitrary", "arbitrary"),
        name="experts",
    )(blk_e, blk_n, src_row, dst_row, hn, row_w, w_gate, w_up, w_down)


def _combine_kernel(*refs, has_norm):
    h_ref, y0_ref, y1_ref = refs[:3]
    o_ref = refs[-1]
    x = h_ref[...] + (y0_ref[0] + y1_ref[0])
    if has_norm:
        x = x * lax.rsqrt(jnp.mean(x * x, axis=-1, keepdims=True) + NORM_EPS) * refs[3][...]
    o_ref[...] = x


def combine(h, y_slots, gain=None, *, tm=512):
    S, D = h.shape
    tm = min(tm, S)
    y3 = y_slots.reshape(TOP_K, S, D)
    in_specs = [pl.BlockSpec((tm, D), lambda i: (i, 0)),
                pl.BlockSpec((1, tm, D), lambda i: (0, i, 0)),
                pl.BlockSpec((1, tm, D), lambda i: (1, i, 0))]
    args = [h, y3, y3]
    if gain is not None:
        in_specs.append(pl.BlockSpec((1, D), lambda i: (0, 0)))
        args.append(gain.reshape(1, D).astype(F32))
    return pl.pallas_call(
        functools.partial(_combine_kernel, has_norm=gain is not None),
        out_shape=jax.ShapeDtypeStruct((S, D), F32),
        grid=(S // tm,),
        in_specs=in_specs,
        out_specs=pl.BlockSpec((tm, D), lambda i: (i, 0)),
        compiler_params=_params("parallel"),
        name="combine",
    )(*args)


def moe_layout(expert_id, weights):
    S = expert_id.shape[0]
    A = S * TOP_K
    flat_e = expert_id.reshape(A).astype(jnp.int32)
    flat_w = weights.reshape(A)
    rank, counts = expert_rank(flat_e)
    nblk = (counts + MOE_ROWS - 1) // MOE_ROWS
    blk_end = jnp.cumsum(nblk)
    blk_start = blk_end - nblk
    dest = blk_start[flat_e] * MOE_ROWS + rank
    nb = (A + N_EXPERTS * (MOE_ROWS - 1)) // MOE_ROWS
    total = blk_end[-1]
    blk = jnp.minimum(jnp.arange(nb, dtype=jnp.int32), total - 1)
    blk_e = jnp.minimum(jnp.searchsorted(blk_end, blk, side='right'), N_EXPERTS - 1).astype(jnp.int32)
    blk_n = jnp.clip(counts[blk_e] - (blk - blk_start[blk_e]) * MOE_ROWS, 0, MOE_ROWS)
    blk_n = jnp.where(jnp.arange(nb) < total, blk_n, 0).astype(jnp.int32)
    a = jnp.arange(A, dtype=jnp.int32)
    src_row = jnp.zeros((nb * MOE_ROWS,), jnp.int32).at[dest].set(a // TOP_K)
    dst_row = jnp.zeros((nb * MOE_ROWS,), jnp.int32).at[dest].set((a % TOP_K) * S + a // TOP_K)
    row_w = jnp.zeros((nb * MOE_ROWS,), F32).at[dest].set(flat_w).reshape(nb * MOE_ROWS, 1)
    return blk_e, blk_n, src_row, dst_row, row_w


def kernel(x, mem, positions, mix_norm, w_in, shift_mu, decay_w0, decay_w2, aaa_a0, aaa_a2, gate_g2, k_k, k_a,
           r_k, lnx_w, lnx_b, diff_lambda, subln_w, w_out, mem_q_norm, mem_kv_norm, wq_mem, wk_mem, wv_mem,
           wo_mem, moe_norm, router_group_w, router_group_b, router_expert_w, router_expert_b, expert_gate,
           expert_up, expert_down, final_norm):
    B, S, D = x.shape
    depth = w_in.shape[0]
    outs = []
    for b in range(B):
        h = x[b]
        memb = mem[b]
        for l in range(depth):
            lambda_init = 0.8 - 0.6 * math.exp(-0.3 * l)
            w_in_b = w_in[l].astype(BF16)
            proj_r = matmul(h, w_in_b[:, :RWKV_COLS], gain=mix_norm[l], tn=RWKV_COLS // 2)
            proj_d = matmul(h, w_in_b[:, RWKV_COLS:], gain=mix_norm[l], tn=1024)
            r, k, v, lw, a, g = rwkv_prep(proj_r, shift_mu[l], decay_w0[l], decay_w2[l], aaa_a0[l], aaa_a2[l],
                                          gate_g2[l])
            y_rwkv = rwkv_scan(r, k, v, lw, a, g, k_k[l], k_a[l], r_k[l], lnx_w[l], lnx_b[l])
            qr, kr, vb = rope(proj_d, positions[b])
            y_diff = diff_flash(qr, kr, vb, diff_lambda[l], subln_w[l], lambda_init)
            mixed = jnp.concatenate([y_rwkv, y_diff], axis=-1)
            h = matmul(mixed, w_out[l].astype(BF16), residual=h)
            q = matmul(h, wq_mem[l].astype(BF16), gain=mem_q_norm[l], out_dtype=BF16)
            km = matmul(memb, wk_mem[l].astype(BF16), gain=mem_kv_norm[l], out_dtype=BF16)
            vm = matmul(memb, wv_mem[l].astype(BF16), gain=mem_kv_norm[l], out_dtype=BF16)
            o = mem_attn(q, km, vm)
            h = matmul(o, wo_mem[l].astype(BF16), residual=h)
            hn, expert_id, weights = router(h, moe_norm[l], router_group_w[l], router_group_b[l],
                                            router_expert_w[l], router_expert_b[l])
            blk_e, blk_n, src_row, dst_row, row_w = moe_layout(expert_id, weights)
            y_slots = experts(hn, blk_e, blk_n, src_row, dst_row, row_w,
                              expert_gate[l], expert_up[l], expert_down[l], TOP_K * S)
            h = combine(h, y_slots, final_norm if l == depth - 1 else None)
        outs.append(h)
    return jnp.stack(outs, axis=0)
```

```python
import functools
import math

import numpy as np
import jax
import jax.numpy as jnp
from jax import lax
from jax.experimental import pallas as pl
from jax.experimental.pallas import tpu as pltpu

F32 = jnp.float32
BF16 = jnp.bfloat16
HIGHEST = lax.Precision.HIGHEST

RWKV_HEADS = 16
HEAD_N = 64
RWKV_WIDTH = RWKV_HEADS * HEAD_N
DECAY_LORA = 64
AAA_LORA = 64
GATE_LORA = 128
RWKV_COLS = 3 * RWKV_WIDTH + DECAY_LORA + AAA_LORA + GATE_LORA
DIFF_HEADS = 8
DIFF_QK = 64
DIFF_V = 128
DIFF_WIDTH = DIFF_HEADS * DIFF_V
ROPE_THETA = 10000.0
MEM_HEADS = 4
N_GROUPS = 8
EXPERTS_PER_GROUP = 8
N_EXPERTS = N_GROUPS * EXPERTS_PER_GROUP
TOP_K = 2
NORM_EPS = 1e-6
LNX_EPS = 64e-5
SUBLN_EPS = 1e-5

LANES = 128
VMEM_LIMIT = 48 * 1024 * 1024

CHUNK = 64
SCAN_ROWS = 256
MOE_ROWS = 512
MOE_SUB = 256
MOE_SPLIT = 4
MOE_DMA_UNROLL = 8


def _params(*sem):
    return pltpu.CompilerParams(dimension_semantics=sem, vmem_limit_bytes=VMEM_LIMIT)


def _mm_kernel(*refs, has_norm, has_res):
    it = iter(refs)
    x_ref = next(it)
    g_ref = next(it) if has_norm else None
    w_ref = next(it)
    r_ref = next(it) if has_res else None
    o_ref = next(it)
    xs_ref = next(it)

    @pl.when(pl.program_id(1) == 0)
    def _():
        x = x_ref[...].astype(F32)
        if has_norm:
            x = x * lax.rsqrt(jnp.mean(x * x, axis=-1, keepdims=True) + NORM_EPS) * g_ref[...]
        xs_ref[...] = x.astype(BF16)

    acc = jnp.dot(xs_ref[...], w_ref[...], preferred_element_type=F32)
    if has_res:
        acc = acc + r_ref[...]
    o_ref[...] = acc.astype(o_ref.dtype)


def matmul(x, w, *, gain=None, residual=None, out_dtype=F32, tm=512, tn=512):
    M, K = x.shape
    N = w.shape[1]
    tm = min(tm, M)
    assert M % tm == 0 and N % tn == 0, (M, N, tm, tn)
    in_specs = [pl.BlockSpec((tm, K), lambda i, j: (i, 0))]
    args = [x]
    if gain is not None:
        in_specs.append(pl.BlockSpec((1, K), lambda i, j: (0, 0)))
        args.append(gain.reshape(1, K).astype(F32))
    in_specs.append(pl.BlockSpec((K, tn), lambda i, j: (0, j)))
    args.append(w)
    if residual is not None:
        in_specs.append(pl.BlockSpec((tm, tn), lambda i, j: (i, j)))
        args.append(residual)
    return pl.pallas_call(
        functools.partial(_mm_kernel, has_norm=gain is not None, has_res=residual is not None),
        out_shape=jax.ShapeDtypeStruct((M, N), out_dtype),
        grid=(M // tm, N // tn),
        in_specs=in_specs,
        out_specs=pl.BlockSpec((tm, tn), lambda i, j: (i, j)),
        scratch_shapes=[pltpu.VMEM((tm, K), BF16)],
        compiler_params=_params("parallel", "arbitrary"),
        name="matmul",
    )(*args)


def _sigmoid(x):
    return 1.0 / (1.0 + jnp.exp(-x))


def _rwkv_prep_kernel(p_ref, pp_ref, mu_ref, w0_ref, w2_ref, a0_ref, a2_ref, g2_ref,
                      r_o, k_o, v_o, lw_o, cs_o, a_o, g_o):
    W = RWKV_WIDTH
    p = p_ref[...]
    last = jnp.where(pl.program_id(0) == 0, 0.0, pp_ref[7:8, :])
    prev = pltpu.roll(p, 1, axis=0)
    row = lax.broadcasted_iota(jnp.int32, p.shape, 0)
    prev = jnp.where(row == 0, last, prev)
    ps = p + (prev - p) * mu_ref[...]
    r_o[...] = ps[:, 0:W]
    k_o[...] = ps[:, W:2 * W]
    v_o[...] = ps[:, 2 * W:3 * W]
    o = 3 * W
    wd = ps[:, o:o + DECAY_LORA]
    ad = ps[:, o + DECAY_LORA:o + DECAY_LORA + AAA_LORA]
    gd = ps[:, o + DECAY_LORA + AAA_LORA:o + DECAY_LORA + AAA_LORA + GATE_LORA]
    z = w0_ref[...] + jnp.dot(jnp.tanh(wd).astype(BF16), w2_ref[...], preferred_element_type=F32)
    nz = -z
    softplus = jnp.maximum(nz, 0.0) + jnp.log(1.0 + jnp.exp(-jnp.abs(nz)))
    w_raw = -softplus - 0.5
    lw = -jnp.exp(w_raw)
    lw_o[...] = lw
    tm = p.shape[0]
    rr = lax.broadcasted_iota(jnp.int32, (tm, tm), 0)
    cc = lax.broadcasted_iota(jnp.int32, (tm, tm), 1)
    sh = CHUNK.bit_length() - 1
    tri = (((rr >> sh) == (cc >> sh)) & (rr >= cc)).astype(F32)
    cs_o[...] = _dot(tri, lw, HIGHEST)
    a_o[...] = _sigmoid(a0_ref[...] + jnp.dot(ad.astype(BF16), a2_ref[...], preferred_element_type=F32))
    g_o[...] = jnp.dot(_sigmoid(gd).astype(BF16), g2_ref[...], preferred_element_type=F32)


def rwkv_prep(p, mu, w0, w2, a0, a2, g2, *, tm=256):
    S = p.shape[0]
    tm = min(tm, S)
    W = RWKV_WIDTH
    row = lambda x: x.reshape(1, -1).astype(F32)
    full = lambda a: pl.BlockSpec(a.shape, lambda i: (0, 0))
    args = [p, p, row(mu), row(w0), w2.astype(BF16), row(a0), a2.astype(BF16), g2.astype(BF16)]
    in_specs = [pl.BlockSpec((tm, RWKV_COLS), lambda i: (i, 0)),
                pl.BlockSpec((8, RWKV_COLS), lambda i: (jnp.maximum(i * (tm // 8) - 1, 0), 0))]
    in_specs += [full(a) for a in args[2:]]
    out = jax.ShapeDtypeStruct((S, W), F32)
    return pl.pallas_call(
        _rwkv_prep_kernel,
        out_shape=[out] * 7,
        grid=(S // tm,),
        in_specs=in_specs,
        out_specs=[pl.BlockSpec((tm, W), lambda i: (i, 0))] * 7,
        compiler_params=_params("parallel"),
        name="rwkv_prep",
    )(*args)


def _dot_nt(a, b, precision=None):
    return lax.dot_general(a, b, (((1,), (1,)), ((), ())), preferred_element_type=F32, precision=precision)


def _dot_tn(a, b, precision=None):
    return lax.dot_general(a, b, (((0,), (0,)), ((), ())), preferred_element_type=F32, precision=precision)


def _dot(a, b, precision=None):
    return jnp.dot(a, b, preferred_element_type=F32, precision=precision)


def _rwkv_scan_kernel(r_ref, k_ref, v_ref, lw_ref, cs_ref, a_ref, g_ref, kk_ref, ka_ref, rk_ref, lnw_ref, lnb_ref,
                      y_ref, state_ref):
    C = CHUNK
    N = HEAD_N

    @pl.when(pl.program_id(1) == 0)
    def _():
        state_ref[...] = jnp.zeros_like(state_ref)

    row = lax.broadcasted_iota(jnp.int32, (C, C), 0)
    col = lax.broadcasted_iota(jnp.int32, (C, C), 1)
    eye = (row == col).astype(F32)
    strict = row > col
    incl = row >= col

    n_chunks = r_ref.shape[0] // C
    heads = r_ref.shape[1] // N
    items = [(j, q) for j in range(heads) for q in range(n_chunks)]
    G = range(len(items))

    def tile(ref, j, q):
        return ref[q * C:(q + 1) * C, j * N:(j + 1) * N]

    def par(ref, j):
        return ref[:, j * N:(j + 1) * N]

    r = [tile(r_ref, j, q) for j, q in items]
    v = [tile(v_ref, j, q) for j, q in items]
    k2, at, bt, kt, rt, pc = [], [], [], [], [], []
    for g, (j, q) in enumerate(items):
        k = tile(k_ref, j, q)
        a = tile(a_ref, j, q)
        cs = tile(cs_ref, j, q)
        kk = k * par(kk_ref, j)
        kk = kk / jnp.maximum(jnp.sqrt(jnp.sum(kk * kk, axis=-1, keepdims=True)), 1e-12)
        k2_ = k * (1.0 + (a - 1.0) * par(ka_ref, j))
        e_neg = jnp.exp(-cs)
        k2.append(k2_)
        at.append((-kk * jnp.exp(cs - tile(lw_ref, j, q))).astype(BF16))
        bt.append((kk * a) * e_neg)
        kt.append(k2_ * e_neg)
        rt.append(r[g] * jnp.exp(cs))
        pc.append(jnp.exp(cs[C - 1:C, :]))
    vb = [x.astype(BF16) for x in v]
    btb = [x.astype(BF16) for x in bt]
    ktb = [x.astype(BF16) for x in kt]
    rtb = [x.astype(BF16) for x in rt]
    n_ab = [jnp.where(strict, _dot_nt(at[g], btb[g]), 0.0) for g in G]
    a_ak = [jnp.where(strict, _dot_nt(at[g], ktb[g]), 0.0).astype(BF16) for g in G]
    a_rb = [jnp.where(incl, _dot_nt(rtb[g], btb[g]), 0.0).astype(BF16) for g in G]
    a_rk = [jnp.where(incl, _dot_nt(rtb[g], ktb[g]), 0.0).astype(BF16) for g in G]
    akv = [_dot(a_ak[g], vb[g]) for g in G]
    t = None
    b = 1
    while b < C:
        sh = (2 * b).bit_length() - 1
        low_left = ((row >> sh) == (col >> sh)) & ((row & b) != 0) & ((col & b) == 0)
        nb = [jnp.where(low_left, n_ab[g], 0.0) for g in G]
        if b == 1:
            t = [eye + nb[g] for g in G]
        else:
            tb = [t[g].astype(BF16) for g in G]
            z = [_dot(nb[g].astype(BF16), tb[g]).astype(BF16) for g in G]
            t = [t[g] + _dot(tb[g], z[g]) for g in G]
        b *= 2
    tb = [t[g].astype(BF16) for g in G]
    wm = [_dot(tb[g], at[g]).astype(BF16) for g in G]
    u0 = [_dot(tb[g], akv[g].astype(BF16)).astype(BF16) for g in G]
    rm = [(rt[g] + _dot(a_rb[g], wm[g])).astype(BF16) for g in G]
    y0 = [_dot(a_rb[g], u0[g]) + _dot(a_rk[g], vb[g]) for g in G]
    bp = [(bt[g] * pc[g]).astype(BF16) for g in G]
    kp = [(kt[g] * pc[g]).astype(BF16) for g in G]
    mp = [_dot_tn(wm[g], bp[g]).astype(BF16) for g in G]
    s_add = [_dot_tn(u0[g], bp[g]) + _dot_tn(vb[g], kp[g]) for g in G]

    for j in range(heads):
        s = state_ref[j]
        for q in range(n_chunks):
            g = j * n_chunks + q
            sb = s.astype(BF16)
            y = _dot_nt(rm[g], sb) + y0[g]
            s = s * pc[g] + _dot(sb, mp[g]) + s_add[g]
            mean = jnp.mean(y, axis=-1, keepdims=True)
            yc = y - mean
            var = jnp.mean(yc * yc, axis=-1, keepdims=True)
            yn = yc * lax.rsqrt(var + LNX_EPS) * par(lnw_ref, j) + par(lnb_ref, j)
            bonus = jnp.sum(r[g] * k2[g] * par(rk_ref, j), axis=-1, keepdims=True) * v[g]
            y_ref[q * C:(q + 1) * C, j * N:(j + 1) * N] = (yn + bonus) * tile(g_ref, j, q)
        state_ref[j] = s


def rwkv_scan(r, k, v, lw, cs, a, g, k_k, k_a, r_k, lnx_w, lnx_b, *, rows=SCAN_ROWS):
    S, W = r.shape
    rows = min(rows, S)
    hp = LANES // HEAD_N
    row = lambda x: x.reshape(1, W).astype(F32)
    seq = pl.BlockSpec((rows, LANES), lambda h, c: (c, h))
    par = pl.BlockSpec((1, LANES), lambda h, c: (0, h))
    return pl.pallas_call(
        _rwkv_scan_kernel,
        out_shape=jax.ShapeDtypeStruct((S, W), F32),
        grid=(W // LANES, S // rows),
        in_specs=[seq] * 7 + [par] * 5,
        out_specs=seq,
        scratch_shapes=[pltpu.VMEM((hp, HEAD_N, HEAD_N), F32)],
        compiler_params=_params("parallel", "arbitrary"),
        name="rwkv_scan",
    )(r, k, v, lw, cs, a, g, row(k_k), row(k_a), row(r_k), row(lnx_w), row(lnx_b))


def _rope_kernel(q_ref, k_ref, v_ref, pos_ref, freq_ref, qo_ref, ko_ref, vo_ref):
    ang = pos_ref[...] * freq_ref[...]
    lane = lax.broadcasted_iota(jnp.int32, ang.shape, 1)
    first_half = (lane % DIFF_QK) < (DIFF_QK // 2)
    cos = jnp.cos(ang)
    sin = jnp.where(first_half, -1.0, 1.0) * jnp.sin(ang)
    half = DIFF_QK // 2
    scale = DIFF_QK ** -0.5
    for b in range(q_ref.shape[1] // LANES):
        ls = slice(b * LANES, (b + 1) * LANES)
        for src, dst, sc in ((q_ref, qo_ref, scale), (k_ref, ko_ref, 1.0)):
            x = src[:, ls]
            partner = jnp.where(first_half, pltpu.roll(x, LANES - half, axis=1), pltpu.roll(x, half, axis=1))
            dst[:, ls] = ((x * cos + partner * sin) * sc).astype(dst.dtype)
    vo_ref[...] = v_ref[...].astype(vo_ref.dtype)


def rope(proj_d, positions, *, tm=256):
    S = proj_d.shape[0]
    tm = min(tm, S)
    W = DIFF_WIDTH
    inv_freq = ROPE_THETA ** (-(jnp.arange(0, DIFF_QK, 2, dtype=F32) / DIFF_QK))
    freq = jnp.tile(inv_freq, LANES // (DIFF_QK // 2)).reshape(1, LANES)
    pos = positions.reshape(S, 1).astype(F32)
    out = jax.ShapeDtypeStruct((S, W), BF16)
    return pl.pallas_call(
        _rope_kernel,
        out_shape=[out] * 3,
        grid=(S // tm,),
        in_specs=[pl.BlockSpec((tm, W), lambda i: (i, 0)),
                  pl.BlockSpec((tm, W), lambda i: (i, 1)),
                  pl.BlockSpec((tm, W), lambda i: (i, 2)),
                  pl.BlockSpec((tm, 1), lambda i: (i, 0)),
                  pl.BlockSpec((1, LANES), lambda i: (0, 0))],
        out_specs=[pl.BlockSpec((tm, W), lambda i: (i, 0))] * 3,
        compiler_params=_params("parallel"),
        name="rope",
    )(proj_d, proj_d, proj_d, pos, freq)


def _diff_flash_kernel(qi_tab, ki_tab, q_ref, k_ref, v_ref, lam_ref, sw_ref, o_ref,
                       m_ref, l_ref, acc_ref, s_ref, p_ref, alpha_ref, *, tile, row_block, diag_parts, lambda_init):
    p = pl.program_id(1)
    qi = qi_tab[p]
    ki = ki_tab[p]

    @pl.when(ki == 0)
    def _():
        m_ref[...] = jnp.full_like(m_ref, -jnp.inf)
        l_ref[...] = jnp.zeros_like(l_ref)
        acc_ref[...] = jnp.zeros_like(acc_ref)

    def process(row0, nrows, kmax, masked):
        rows = slice(row0, row0 + nrows)
        q = q_ref[rows, :]
        k = k_ref[0:kmax, :]
        v = v_ref[0:kmax, :]
        lane = lax.broadcasted_iota(jnp.int32, q.shape, 1)
        reps = kmax // LANES
        for c in range(2):
            qc = jnp.where((lane >> (DIFF_QK.bit_length() - 1)) == c, q, jnp.zeros_like(q))
            s_ref[0:nrows, 0:kmax] = _dot_nt(qc, k)
            for r0 in range(0, nrows, row_block):
                ar = slice(row0 + r0, row0 + r0 + row_block)
                lr = slice(r0, r0 + row_block)
                s = s_ref[lr, 0:kmax]
                if masked:
                    qpos = lax.broadcasted_iota(jnp.int32, s.shape, 0) + (row0 + r0)
                    kpos = lax.broadcasted_iota(jnp.int32, s.shape, 1)
                    s = jnp.where(kpos <= qpos, s, -jnp.inf)
                m_old = m_ref[c, ar, :]
                m_new = jnp.maximum(m_old, jnp.max(s, axis=-1, keepdims=True))
                alpha = jnp.exp(m_old - m_new)
                pr = jnp.exp(s - jnp.concatenate([m_new] * reps, axis=1))
                l_ref[c, ar, :] = alpha * l_ref[c, ar, :] + jnp.sum(pr, axis=-1, keepdims=True)
                m_ref[c, ar, :] = m_new
                alpha_ref[lr, :] = alpha
                p_ref[lr, 0:kmax] = pr.astype(BF16)
            acc_ref[c, rows, :] = (alpha_ref[0:nrows, :] * acc_ref[c, rows, :]
                                   + _dot(p_ref[0:nrows, 0:kmax], v))

    @pl.when(ki < qi)
    def _():
        process(0, tile, tile, False)

    @pl.when(ki == qi)
    def _():
        part = tile // diag_parts
        for a in range(diag_parts):
            process(a * part, part, (a + 1) * part, True)
        lp = lam_ref[...]
        lam = (jnp.exp(jnp.sum(lp[0:1] * lp[1:2], axis=-1, keepdims=True))
               - jnp.exp(jnp.sum(lp[2:3] * lp[3:4], axis=-1, keepdims=True)) + lambda_init)
        o = acc_ref[0] / l_ref[0] - lam * (acc_ref[1] / l_ref[1])
        o = o * lax.rsqrt(jnp.mean(o * o, axis=-1, keepdims=True) + SUBLN_EPS)
        o_ref[...] = o * sw_ref[...] * (1.0 - lambda_init)


def diff_flash(q, k, v, lambdas, subln_w, lambda_init, *, tile=1024, row_block=32, diag_parts=4):
    S = q.shape[0]
    tile = min(tile, S)
    assert S % tile == 0 and (tile // diag_parts) % LANES == 0 and (tile // diag_parts) % row_block == 0
    pairs = [(qi, ki) for qi in range(S // tile) for ki in range(qi + 1)]
    qi_tab = jnp.asarray([p[0] for p in pairs], jnp.int32)
    ki_tab = jnp.asarray([p[1] for p in pairs], jnp.int32)
    grid_spec = pltpu.PrefetchScalarGridSpec(
        num_scalar_prefetch=2,
        grid=(DIFF_HEADS, len(pairs)),
        in_specs=[pl.BlockSpec((tile, LANES), lambda h, p, qt, kt: (qt[p], h)),
                  pl.BlockSpec((tile, LANES), lambda h, p, qt, kt: (kt[p], h)),
                  pl.BlockSpec((tile, LANES), lambda h, p, qt, kt: (kt[p], h)),
                  pl.BlockSpec((4, DIFF_QK), lambda h, p, qt, kt: (0, 0)),
                  pl.BlockSpec((1, DIFF_V), lambda h, p, qt, kt: (0, 0))],
        out_specs=pl.BlockSpec((tile, LANES), lambda h, p, qt, kt: (qt[p], h)),
        scratch_shapes=[pltpu.VMEM((2, tile, LANES), F32),
                        pltpu.VMEM((2, tile, LANES), F32),
                        pltpu.VMEM((2, tile, DIFF_V), F32),
                        pltpu.VMEM((tile, tile), F32),
                        pltpu.VMEM((tile, tile), BF16),
                        pltpu.VMEM((tile, LANES), F32)],
    )
    return pl.pallas_call(
        functools.partial(_diff_flash_kernel, tile=tile, row_block=row_block, diag_parts=diag_parts,
                          lambda_init=lambda_init),
        out_shape=jax.ShapeDtypeStruct((S, DIFF_WIDTH), F32),
        grid_spec=grid_spec,
        compiler_params=_params("parallel", "arbitrary"),
        name="diff_flash",
    )(qi_tab, ki_tab, q, k, v, lambdas.astype(F32), subln_w.reshape(1, DIFF_V).astype(F32))


def _mem_attn_kernel(q_ref, k_ref, v_ref, o_ref):
    D = q_ref.shape[1]
    hd = D // MEM_HEADS
    scale = hd ** -0.5
    for h in range(MEM_HEADS):
        ls = slice(h * hd, (h + 1) * hd)
        s = _dot_nt(q_ref[:, ls], k_ref[:, ls]) * scale
        s = s - jnp.max(s, axis=-1, keepdims=True)
        e = jnp.exp(s)
        pr = e / jnp.sum(e, axis=-1, keepdims=True)
        o_ref[:, ls] = _dot(pr.astype(BF16), v_ref[:, ls]).astype(o_ref.dtype)


def mem_attn(q, k, v, *, tq=512):
    S, D = q.shape
    M = k.shape[0]
    tq = min(tq, S)
    return pl.pallas_call(
        _mem_attn_kernel,
        out_shape=jax.ShapeDtypeStruct((S, D), BF16),
        grid=(S // tq,),
        in_specs=[pl.BlockSpec((tq, D), lambda i: (i, 0)),
                  pl.BlockSpec((M, D), lambda i: (0, 0)),
                  pl.BlockSpec((M, D), lambda i: (0, 0))],
        out_specs=pl.BlockSpec((tq, D), lambda i: (i, 0)),
        compiler_params=_params("parallel"),
        name="mem_attn",
    )(q, k, v)


def _first_argmax(x, lane, big):
    m = jnp.max(x, axis=-1, keepdims=True)
    idx = jnp.min(jnp.where(x == m, lane, big), axis=-1, keepdims=True)
    return m, idx


def _router_kernel(h_ref, gain_ref, wg_ref, bg_ref, we_ref, be_ref, hn_ref, eid_ref, ew_ref):
    x = h_ref[...]
    hn = x * lax.rsqrt(jnp.mean(x * x, axis=-1, keepdims=True) + NORM_EPS) * gain_ref[...]
    hn_ref[...] = hn
    g_logits = _dot(hn, wg_ref[...], HIGHEST) + bg_ref[...]
    e_logits = _dot(hn, we_ref[...], HIGHEST) + be_ref[...]
    tm = x.shape[0]
    lane_g = lax.broadcasted_iota(jnp.int32, (tm, N_GROUPS), 1)
    g_max, g_idx = _first_argmax(g_logits, lane_g, N_GROUPS)
    g_w = 1.0 / jnp.sum(jnp.exp(g_logits - g_max), axis=-1, keepdims=True)
    lane_e = lax.broadcasted_iota(jnp.int32, (tm, N_EXPERTS), 1)
    in_group = (lane_e // EXPERTS_PER_GROUP) == g_idx
    el = jnp.where(in_group, e_logits, -jnp.inf)
    e_max = jnp.max(el, axis=-1, keepdims=True)
    ex = jnp.exp(el - e_max)
    prob = ex / jnp.sum(ex, axis=-1, keepdims=True)
    prob = jnp.where(in_group, prob, -1.0)
    p1, i1 = _first_argmax(prob, lane_e, N_EXPERTS)
    p2, i2 = _first_argmax(jnp.where(lane_e == i1, -1.0, prob), lane_e, N_EXPERTS)
    tot = p1 + p2
    lane_o = lax.broadcasted_iota(jnp.int32, (tm, LANES), 1)
    eid_ref[...] = jnp.where(lane_o == 0, i1, jnp.where(lane_o == 1, i2, 0))
    ew_ref[...] = jnp.where(lane_o == 0, g_w * (p1 / tot), jnp.where(lane_o == 1, g_w * (p2 / tot), 0.0))


def router(h, gain, wg, bg, we, be, *, tm=512):
    S, D = h.shape
    tm = min(tm, S)
    full = lambda a: pl.BlockSpec(a.shape, lambda i: (0, 0))
    args = [h, gain.reshape(1, D).astype(F32), wg.astype(F32), bg.reshape(1, -1).astype(F32),
            we.astype(F32), be.reshape(1, -1).astype(F32)]
    hn, eid, ew = pl.pallas_call(
        _router_kernel,
        out_shape=[jax.ShapeDtypeStruct((S, D), F32),
                   jax.ShapeDtypeStruct((S, LANES), jnp.int32),
                   jax.ShapeDtypeStruct((S, LANES), F32)],
        grid=(S // tm,),
        in_specs=[pl.BlockSpec((tm, D), lambda i: (i, 0))] + [full(a) for a in args[1:]],
        out_specs=[pl.BlockSpec((tm, D), lambda i: (i, 0)),
                   pl.BlockSpec((tm, LANES), lambda i: (i, 0)),
                   pl.BlockSpec((tm, LANES), lambda i: (i, 0))],
        compiler_params=_params("parallel"),
        name="router",
    )(*args)
    return hn, eid[:, :TOP_K], ew[:, :TOP_K]


def _rank_kernel(e_ref, rank_ref, cnt_ref, carry_ref):
    @pl.when(pl.program_id(0) == 0)
    def _():
        carry_ref[...] = jnp.zeros_like(carry_ref)

    e = e_ref[...]
    tm = e.shape[0]
    onehot = (lax.broadcasted_iota(jnp.int32, (tm, LANES), 1) == e).astype(BF16)
    r = lax.broadcasted_iota(jnp.int32, (tm, tm), 0)
    c = lax.broadcasted_iota(jnp.int32, (tm, tm), 1)
    before = _dot((r > c).astype(BF16), onehot) + carry_ref[...]
    rank_ref[...] = jnp.sum(jnp.where(onehot > 0, before, 0.0), axis=-1, keepdims=True).astype(jnp.int32)
    carry_ref[...] = carry_ref[...] + jnp.sum(onehot.astype(F32), axis=0, keepdims=True)
    cnt_ref[...] = carry_ref[...].astype(jnp.int32)


def expert_rank(flat_e, *, tm=512):
    A = flat_e.shape[0]
    tm = min(tm, A)
    rank, cnt = pl.pallas_call(
        _rank_kernel,
        out_shape=[jax.ShapeDtypeStruct((A, 1), jnp.int32), jax.ShapeDtypeStruct((1, LANES), jnp.int32)],
        grid=(A // tm,),
        in_specs=[pl.BlockSpec((tm, 1), lambda i: (i, 0))],
        out_specs=[pl.BlockSpec((tm, 1), lambda i: (i, 0)), pl.BlockSpec((1, LANES), lambda i: (0, 0))],
        scratch_shapes=[pltpu.VMEM((1, LANES), F32)],
        compiler_params=_params("arbitrary"),
        name="expert_rank",
    )(flat_e.reshape(A, 1))
    return rank[:, 0], cnt[0, :N_EXPERTS]


def _for_rows(n, fn):
    sh = MOE_DMA_UNROLL.bit_length() - 1

    def group(t, _):
        for u in range(MOE_DMA_UNROLL):
            fn(t * MOE_DMA_UNROLL + u)
        return 0
    lax.fori_loop(0, n >> sh, group, 0)

    def one(r, _):
        fn(r)
        return 0
    lax.fori_loop((n >> sh) << sh, n, one, 0)


def _experts_kernel(blk_e, blk_n, src_row, dst_row, hn_hbm, w_ref, wg_ref, wu_ref, wd_ref, out_hbm,
                    xg_ref, xb_ref, acc_ref, wgb_ref, wub_ref, wdb_ref, gsem, ssem):
    i = pl.program_id(0)
    j = pl.program_id(1)
    nb = pl.num_programs(0)
    nsplit = pl.num_programs(1)
    sub_shift = MOE_SUB.bit_length() - 1

    def n_sub_of(b):
        return (blk_n[b] + (MOE_SUB - 1)) >> sub_shift

    def gather(b, r):
        return pltpu.make_async_copy(hn_hbm.at[pl.ds(src_row[b * MOE_ROWS + r], 1)], xg_ref.at[pl.ds(r, 1)], gsem)

    def scatter(b, r):
        return pltpu.make_async_copy(acc_ref.at[b & 1, pl.ds(r, 1)],
                                     out_hbm.at[pl.ds(dst_row[b * MOE_ROWS + r], 1)], ssem)

    def start_gather(b):
        _for_rows(n_sub_of(b) << sub_shift, lambda r: gather(b, r).start())

    def wait_gather(b):
        _for_rows(n_sub_of(b) << sub_shift, lambda r: gather(b, r).wait())

    def start_scatter(b):
        _for_rows(blk_n[b], lambda r: scatter(b, r).start())

    def wait_scatter(b):
        _for_rows(blk_n[b], lambda r: scatter(b, r).wait())

    n_sub = n_sub_of(i)
    slot = i & 1

    @pl.when(j == 0)
    def _():
        @pl.when(i == 0)
        def _():
            start_gather(0)
        wait_gather(i)

        def cast(sb, _):
            rs = pl.ds(pl.multiple_of(sb * MOE_SUB, MOE_SUB), MOE_SUB)
            xb_ref[rs, :] = xg_ref[rs, :].astype(BF16)
            return 0
        lax.fori_loop(0, n_sub, cast, 0)

        @pl.when(i + 1 < nb)
        def _():
            start_gather(i + 1)

    @pl.when(n_sub > 0)
    def _():
        wgb_ref[...] = wg_ref[0].astype(BF16)
        wub_ref[...] = wu_ref[0].astype(BF16)
        wdb_ref[...] = wd_ref[0].astype(BF16)

    def sub_block(sb, _):
        rs = pl.ds(pl.multiple_of(sb * MOE_SUB, MOE_SUB), MOE_SUB)
        xb = xb_ref[rs, :]
        gate = _dot(xb, wgb_ref[...])
        hid = (gate * _sigmoid(gate)) * _dot(xb, wub_ref[...])
        part = _dot(hid.astype(BF16), wdb_ref[...])

        @pl.when(j == 0)
        def _():
            acc_ref[slot, rs, :] = part

        @pl.when((j > 0) & (j < nsplit - 1))
        def _():
            acc_ref[slot, rs, :] = acc_ref[slot, rs, :] + part

        @pl.when(j == nsplit - 1)
        def _():
            acc_ref[slot, rs, :] = (acc_ref[slot, rs, :] + part) * w_ref[rs, :]
        return 0
    lax.fori_loop(0, n_sub, sub_block, 0)

    @pl.when(j == nsplit - 1)
    def _():
        @pl.when(i > 0)
        def _():
            wait_scatter(i - 1)
        start_scatter(i)

        @pl.when(i == nb - 1)
        def _():
            wait_scatter(i)


def experts(hn, blk_e, blk_n, src_row, dst_row, row_w, w_gate, w_up, w_down, n_out_rows):
    S, D = hn.shape
    E, _, DE = w_gate.shape
    nb = blk_e.shape[0]
    de = DE // MOE_SPLIT
    split = lambda i, j, bn: jnp.where(bn[i] > 0, j, MOE_SPLIT - 1)
    grid_spec = pltpu.PrefetchScalarGridSpec(
        num_scalar_prefetch=4,
        grid=(nb, MOE_SPLIT),
        in_specs=[pl.BlockSpec(memory_space=pl.ANY),
                  pl.BlockSpec((MOE_ROWS, 1), lambda i, j, be, bn, sr, dr: (i, 0)),
                  pl.BlockSpec((1, D, de), lambda i, j, be, bn, sr, dr: (be[i], 0, split(i, j, bn))),
                  pl.BlockSpec((1, D, de), lambda i, j, be, bn, sr, dr: (be[i], 0, split(i, j, bn))),
                  pl.BlockSpec((1, de, D), lambda i, j, be, bn, sr, dr: (be[i], split(i, j, bn), 0))],
        out_specs=pl.BlockSpec(memory_space=pl.ANY),
        scratch_shapes=[pltpu.VMEM((MOE_ROWS, D), F32),
                        pltpu.VMEM((MOE_ROWS, D), BF16),
                        pltpu.VMEM((2, MOE_ROWS, D), F32),
                        pltpu.VMEM((D, de), BF16),
                        pltpu.VMEM((D, de), BF16),
                        pltpu.VMEM((de, D), BF16),
                        pltpu.SemaphoreType.DMA(()),
                        pltpu.SemaphoreType.DMA(())],
    )
    return pl.pallas_call(
        _experts_kernel,
        out_shape=jax.ShapeDtypeStruct((n_out_rows, D), F32),
        grid_spec=grid_spec,
        compiler_params=_params("arbitrary", "arbitrary"),
        name="experts",
    )(blk_e, blk_n, src_row, dst_row, hn, row_w, w_gate, w_up, w_down)


def _combine_kernel(*refs, has_norm):
    h_ref, y0_ref, y1_ref = refs[:3]
    o_ref = refs[-1]
    x = h_ref[...] + (y0_ref[0] + y1_ref[0])
    if has_norm:
        x = x * lax.rsqrt(jnp.mean(x * x, axis=-1, keepdims=True) + NORM_EPS) * refs[3][...]
    o_ref[...] = x


def combine(h, y_slots, gain=None, *, tm=512):
    S, D = h.shape
    tm = min(tm, S)
    y3 = y_slots.reshape(TOP_K, S, D)
    in_specs = [pl.BlockSpec((tm, D), lambda i: (i, 0)),
                pl.BlockSpec((1, tm, D), lambda i: (0, i, 0)),
                pl.BlockSpec((1, tm, D), lambda i: (1, i, 0))]
    args = [h, y3, y3]
    if gain is not None:
        in_specs.append(pl.BlockSpec((1, D), lambda i: (0, 0)))
        args.append(gain.reshape(1, D).astype(F32))
    return pl.pallas_call(
        functools.partial(_combine_kernel, has_norm=gain is not None),
        out_shape=jax.ShapeDtypeStruct((S, D), F32),
        grid=(S // tm,),
        in_specs=in_specs,
        out_specs=pl.BlockSpec((tm, D), lambda i: (i, 0)),
        compiler_params=_params("parallel"),
        name="combine",
    )(*args)


def moe_layout(expert_id, weights):
    S = expert_id.shape[0]
    A = S * TOP_K
    flat_e = expert_id.reshape(A).astype(jnp.int32)
    flat_w = weights.reshape(A)
    rank, counts = expert_rank(flat_e)
    nblk = (counts + MOE_ROWS - 1) // MOE_ROWS
    blk_end = jnp.cumsum(nblk)
    blk_start = blk_end - nblk
    dest = blk_start[flat_e] * MOE_ROWS + rank
    nb = (A + N_EXPERTS * (MOE_ROWS - 1)) // MOE_ROWS
    total = blk_end[-1]
    blk = jnp.minimum(jnp.arange(nb, dtype=jnp.int32), total - 1)
    blk_e = jnp.minimum(jnp.searchsorted(blk_end, blk, side='right'), N_EXPERTS - 1).astype(jnp.int32)
    blk_n = jnp.clip(counts[blk_e] - (blk - blk_start[blk_e]) * MOE_ROWS, 0, MOE_ROWS)
    blk_n = jnp.where(jnp.arange(nb) < total, blk_n, 0).astype(jnp.int32)
    a = jnp.arange(A, dtype=jnp.int32)
    src_row = jnp.zeros((nb * MOE_ROWS,), jnp.int32).at[dest].set(a // TOP_K)
    dst_row = jnp.zeros((nb * MOE_ROWS,), jnp.int32).at[dest].set((a % TOP_K) * S + a // TOP_K)
    row_w = jnp.zeros((nb * MOE_ROWS,), F32).at[dest].set(flat_w).reshape(nb * MOE_ROWS, 1)
    return blk_e, blk_n, src_row, dst_row, row_w


def kernel(x, mem, positions, mix_norm, w_in, shift_mu, decay_w0, decay_w2, aaa_a0, aaa_a2, gate_g2, k_k, k_a,
           r_k, lnx_w, lnx_b, diff_lambda, subln_w, w_out, mem_q_norm, mem_kv_norm, wq_mem, wk_mem, wv_mem,
           wo_mem, moe_norm, router_group_w, router_group_b, router_expert_w, router_expert_b, expert_gate,
           expert_up, expert_down, final_norm):
    B, S, D = x.shape
    depth = w_in.shape[0]
    outs = []
    for b in range(B):
        h = x[b]
        memb = mem[b]
        for l in range(depth):
            lambda_init = 0.8 - 0.6 * math.exp(-0.3 * l)
            w_in_b = w_in[l].astype(BF16)
            proj_r = matmul(h, w_in_b[:, :RWKV_COLS], gain=mix_norm[l], tn=RWKV_COLS // 2)
            proj_d = matmul(h, w_in_b[:, RWKV_COLS:], gain=mix_norm[l], tn=1024)
            r, k, v, lw, cs, a, g = rwkv_prep(proj_r, shift_mu[l], decay_w0[l], decay_w2[l], aaa_a0[l],
                                              aaa_a2[l], gate_g2[l])
            y_rwkv = rwkv_scan(r, k, v, lw, cs, a, g, k_k[l], k_a[l], r_k[l], lnx_w[l], lnx_b[l])
            qr, kr, vb = rope(proj_d, positions[b])
            y_diff = diff_flash(qr, kr, vb, diff_lambda[l], subln_w[l], lambda_init)
            mixed = jnp.concatenate([y_rwkv, y_diff], axis=-1)
            h = matmul(mixed, w_out[l].astype(BF16), residual=h)
            q = matmul(h, wq_mem[l].astype(BF16), gain=mem_q_norm[l], out_dtype=BF16)
            km = matmul(memb, wk_mem[l].astype(BF16), gain=mem_kv_norm[l], out_dtype=BF16)
            vm = matmul(memb, wv_mem[l].astype(BF16), gain=mem_kv_norm[l], out_dtype=BF16)
            o = mem_attn(q, km, vm)
            h = matmul(o, wo_mem[l].astype(BF16), residual=h)
            hn, expert_id, weights = router(h, moe_norm[l], router_group_w[l], router_group_b[l],
                                            router_expert_w[l], router_expert_b[l])
            blk_e, blk_n, src_row, dst_row, row_w = moe_layout(expert_id, weights)
            y_slots = experts(hn, blk_e, blk_n, src_row, dst_row, row_w,
                              expert_gate[l], expert_up[l], expert_down[l], TOP_K * S)
            h = combine(h, y_slots, final_norm if l == depth - 1 else None)
        outs.append(h)
    return jnp.stack(outs, axis=0)
```

```python
import functools
import math

import numpy as np
import jax
import jax.numpy as jnp
from jax import lax
from jax.experimental import pallas as pl
from jax.experimental.pallas import tpu as pltpu

F32 = jnp.float32
BF16 = jnp.bfloat16
HIGHEST = lax.Precision.HIGHEST

RWKV_HEADS = 16
HEAD_N = 64
RWKV_WIDTH = RWKV_HEADS * HEAD_N
DECAY_LORA = 64
AAA_LORA = 64
GATE_LORA = 128
RWKV_COLS = 3 * RWKV_WIDTH + DECAY_LORA + AAA_LORA + GATE_LORA
DIFF_HEADS = 8
DIFF_QK = 64
DIFF_V = 128
DIFF_WIDTH = DIFF_HEADS * DIFF_V
ROPE_THETA = 10000.0
MEM_HEADS = 4
N_GROUPS = 8
EXPERTS_PER_GROUP = 8
N_EXPERTS = N_GROUPS * EXPERTS_PER_GROUP
TOP_K = 2
NORM_EPS = 1e-6
LNX_EPS = 64e-5
SUBLN_EPS = 1e-5

LANES = 128
VMEM_LIMIT = 48 * 1024 * 1024

CHUNK = 64
SCAN_ROWS = 256
MOE_ROWS = 512
MOE_SUB = 256
MOE_SPLIT = 4
MOE_DMA_UNROLL = 8


def _params(*sem):
    return pltpu.CompilerParams(dimension_semantics=sem, vmem_limit_bytes=VMEM_LIMIT)


def _mm_kernel(*refs, has_norm, has_res):
    it = iter(refs)
    x_ref = next(it)
    g_ref = next(it) if has_norm else None
    w_ref = next(it)
    r_ref = next(it) if has_res else None
    o_ref = next(it)
    xs_ref = next(it)

    @pl.when(pl.program_id(1) == 0)
    def _():
        x = x_ref[...].astype(F32)
        if has_norm:
            x = x * lax.rsqrt(jnp.mean(x * x, axis=-1, keepdims=True) + NORM_EPS) * g_ref[...]
        xs_ref[...] = x.astype(BF16)

    acc = jnp.dot(xs_ref[...], w_ref[...], preferred_element_type=F32)
    if has_res:
        acc = acc + r_ref[...]
    o_ref[...] = acc.astype(o_ref.dtype)


def matmul(x, w, *, gain=None, residual=None, out_dtype=F32, tm=512, tn=512):
    M, K = x.shape
    N = w.shape[1]
    tm = min(tm, M)
    assert M % tm == 0 and N % tn == 0, (M, N, tm, tn)
    in_specs = [pl.BlockSpec((tm, K), lambda i, j: (i, 0))]
    args = [x]
    if gain is not None:
        in_specs.append(pl.BlockSpec((1, K), lambda i, j: (0, 0)))
        args.append(gain.reshape(1, K).astype(F32))
    in_specs.append(pl.BlockSpec((K, tn), lambda i, j: (0, j)))
    args.append(w)
    if residual is not None:
        in_specs.append(pl.BlockSpec((tm, tn), lambda i, j: (i, j)))
        args.append(residual)
    return pl.pallas_call(
        functools.partial(_mm_kernel, has_norm=gain is not None, has_res=residual is not None),
        out_shape=jax.ShapeDtypeStruct((M, N), out_dtype),
        grid=(M // tm, N // tn),
        in_specs=in_specs,
        out_specs=pl.BlockSpec((tm, tn), lambda i, j: (i, j)),
        scratch_shapes=[pltpu.VMEM((tm, K), BF16)],
        compiler_params=_params("parallel", "arbitrary"),
        name="matmul",
    )(*args)


def _sigmoid(x):
    return 1.0 / (1.0 + jnp.exp(-x))


def _rwkv_prep_kernel(p_ref, pp_ref, mu_ref, w0_ref, w2_ref, a0_ref, a2_ref, g2_ref,
                      r_o, k_o, v_o, lw_o, cs_o, a_o, g_o):
    W = RWKV_WIDTH
    p = p_ref[...]
    last = jnp.where(pl.program_id(0) == 0, 0.0, pp_ref[7:8, :])
    prev = pltpu.roll(p, 1, axis=0)
    row = lax.broadcasted_iota(jnp.int32, p.shape, 0)
    prev = jnp.where(row == 0, last, prev)
    ps = p + (prev - p) * mu_ref[...]
    r_o[...] = ps[:, 0:W]
    k_o[...] = ps[:, W:2 * W]
    v_o[...] = ps[:, 2 * W:3 * W]
    o = 3 * W
    wd = ps[:, o:o + DECAY_LORA]
    ad = ps[:, o + DECAY_LORA:o + DECAY_LORA + AAA_LORA]
    gd = ps[:, o + DECAY_LORA + AAA_LORA:o + DECAY_LORA + AAA_LORA + GATE_LORA]
    z = w0_ref[...] + jnp.dot(jnp.tanh(wd).astype(BF16), w2_ref[...], preferred_element_type=F32)
    nz = -z
    softplus = jnp.maximum(nz, 0.0) + jnp.log(1.0 + jnp.exp(-jnp.abs(nz)))
    w_raw = -softplus - 0.5
    lw = -jnp.exp(w_raw)
    lw_o[...] = lw
    tm = p.shape[0]
    rr = lax.broadcasted_iota(jnp.int32, (tm, tm), 0)
    cc = lax.broadcasted_iota(jnp.int32, (tm, tm), 1)
    sh = CHUNK.bit_length() - 1
    tri = (((rr >> sh) == (cc >> sh)) & (rr >= cc)).astype(F32)
    cs_o[...] = _dot(tri, lw, HIGHEST)
    a_o[...] = _sigmoid(a0_ref[...] + jnp.dot(ad.astype(BF16), a2_ref[...], preferred_element_type=F32))
    g_o[...] = jnp.dot(_sigmoid(gd).astype(BF16), g2_ref[...], preferred_element_type=F32)


def rwkv_prep(p, mu, w0, w2, a0, a2, g2, *, tm=256):
    S = p.shape[0]
    tm = min(tm, S)
    W = RWKV_WIDTH
    row = lambda x: x.reshape(1, -1).astype(F32)
    full = lambda a: pl.BlockSpec(a.shape, lambda i: (0, 0))
    args = [p, p, row(mu), row(w0), w2.astype(BF16), row(a0), a2.astype(BF16), g2.astype(BF16)]
    in_specs = [pl.BlockSpec((tm, RWKV_COLS), lambda i: (i, 0)),
                pl.BlockSpec((8, RWKV_COLS), lambda i: (jnp.maximum(i * (tm // 8) - 1, 0), 0))]
    in_specs += [full(a) for a in args[2:]]
    out = jax.ShapeDtypeStruct((S, W), F32)
    return pl.pallas_call(
        _rwkv_prep_kernel,
        out_shape=[out] * 7,
        grid=(S // tm,),
        in_specs=in_specs,
        out_specs=[pl.BlockSpec((tm, W), lambda i: (i, 0))] * 7,
        compiler_params=_params("parallel"),
        name="rwkv_prep",
    )(*args)


def _dot_nt(a, b, precision=None):
    return lax.dot_general(a, b, (((1,), (1,)), ((), ())), preferred_element_type=F32, precision=precision)


def _dot_tn(a, b, precision=None):
    return lax.dot_general(a, b, (((0,), (0,)), ((), ())), preferred_element_type=F32, precision=precision)


def _dot(a, b, precision=None):
    return jnp.dot(a, b, preferred_element_type=F32, precision=precision)


def _rwkv_scan_kernel(r_ref, k_ref, v_ref, lw_ref, cs_ref, a_ref, g_ref, kk_ref, ka_ref, rk_ref, lnw_ref, lnb_ref,
                      y_ref, state_ref):
    C = CHUNK
    N = HEAD_N

    @pl.when(pl.program_id(1) == 0)
    def _():
        state_ref[...] = jnp.zeros_like(state_ref)

    row = lax.broadcasted_iota(jnp.int32, (C, C), 0)
    col = lax.broadcasted_iota(jnp.int32, (C, C), 1)
    eye = (row == col).astype(F32)
    strict = row > col
    incl = row >= col

    n_chunks = r_ref.shape[0] // C
    heads = r_ref.shape[1] // N
    items = [(j, q) for j in range(heads) for q in range(n_chunks)]
    G = range(len(items))

    def tile(ref, j, q):
        return ref[q * C:(q + 1) * C, j * N:(j + 1) * N]

    def par(ref, j):
        return ref[:, j * N:(j + 1) * N]

    r = [tile(r_ref, j, q) for j, q in items]
    v = [tile(v_ref, j, q) for j, q in items]
    k2, at, bt, kt, rt, pc = [], [], [], [], [], []
    for g, (j, q) in enumerate(items):
        k = tile(k_ref, j, q)
        a = tile(a_ref, j, q)
        cs = tile(cs_ref, j, q)
        kk = k * par(kk_ref, j)
        kk = kk / jnp.maximum(jnp.sqrt(jnp.sum(kk * kk, axis=-1, keepdims=True)), 1e-12)
        k2_ = k * (1.0 + (a - 1.0) * par(ka_ref, j))
        e_neg = jnp.exp(-cs)
        k2.append(k2_)
        at.append((-kk * jnp.exp(cs - tile(lw_ref, j, q))).astype(BF16))
        bt.append((kk * a) * e_neg)
        kt.append(k2_ * e_neg)
        rt.append(r[g] * jnp.exp(cs))
        pc.append(jnp.exp(cs[C - 1:C, :]))
    vb = [x.astype(BF16) for x in v]
    btb = [x.astype(BF16) for x in bt]
    ktb = [x.astype(BF16) for x in kt]
    rtb = [x.astype(BF16) for x in rt]
    n_ab = [jnp.where(strict, _dot_nt(at[g], btb[g]), 0.0) for g in G]
    a_ak = [jnp.where(strict, _dot_nt(at[g], ktb[g]), 0.0).astype(BF16) for g in G]
    a_rb = [jnp.where(incl, _dot_nt(rtb[g], btb[g]), 0.0).astype(BF16) for g in G]
    a_rk = [jnp.where(incl, _dot_nt(rtb[g], ktb[g]), 0.0).astype(BF16) for g in G]
    akv = [_dot(a_ak[g], vb[g]) for g in G]
    t = None
    b = 1
    while b < C:
        sh = (2 * b).bit_length() - 1
        low_left = ((row >> sh) == (col >> sh)) & ((row & b) != 0) & ((col & b) == 0)
        nb = [jnp.where(low_left, n_ab[g], 0.0) for g in G]
        if b == 1:
            t = [eye + nb[g] for g in G]
        else:
            tb = [t[g].astype(BF16) for g in G]
            z = [_dot(nb[g].astype(BF16), tb[g]).astype(BF16) for g in G]
            t = [t[g] + _dot(tb[g], z[g]) for g in G]
        b *= 2
    tb = [t[g].astype(BF16) for g in G]
    wm = [_dot(tb[g], at[g]).astype(BF16) for g in G]
    u0 = [_dot(tb[g], akv[g].astype(BF16)).astype(BF16) for g in G]
    rm = [(rt[g] + _dot(a_rb[g], wm[g])).astype(BF16) for g in G]
    y0 = [_dot(a_rb[g], u0[g]) + _dot(a_rk[g], vb[g]) for g in G]
    bp = [(bt[g] * pc[g]).astype(BF16) for g in G]
    kp = [(kt[g] * pc[g]).astype(BF16) for g in G]
    mp = [_dot_tn(wm[g], bp[g]).astype(BF16) for g in G]
    s_add = [_dot_tn(u0[g], bp[g]) + _dot_tn(vb[g], kp[g]) for g in G]

    for j in range(heads):
        s = state_ref[j]
        for q in range(n_chunks):
            g = j * n_chunks + q
            sb = s.astype(BF16)
            y = _dot_nt(rm[g], sb) + y0[g]
            s = s * pc[g] + _dot(sb, mp[g]) + s_add[g]
            mean = jnp.mean(y, axis=-1, keepdims=True)
            yc = y - mean
            var = jnp.mean(yc * yc, axis=-1, keepdims=True)
            yn = yc * lax.rsqrt(var + LNX_EPS) * par(lnw_ref, j) + par(lnb_ref, j)
            bonus = jnp.sum(r[g] * k2[g] * par(rk_ref, j), axis=-1, keepdims=True) * v[g]
            y_ref[q * C:(q + 1) * C, j * N:(j + 1) * N] = (yn + bonus) * tile(g_ref, j, q)
        state_ref[j] = s


def rwkv_scan(r, k, v, lw, cs, a, g, k_k, k_a, r_k, lnx_w, lnx_b, *, rows=SCAN_ROWS):
    S, W = r.shape
    rows = min(rows, S)
    hp = LANES // HEAD_N
    row = lambda x: x.reshape(1, W).astype(F32)
    seq = pl.BlockSpec((rows, LANES), lambda h, c: (c, h))
    par = pl.BlockSpec((1, LANES), lambda h, c: (0, h))
    return pl.pallas_call(
        _rwkv_scan_kernel,
        out_shape=jax.ShapeDtypeStruct((S, W), F32),
        grid=(W // LANES, S // rows),
        in_specs=[seq] * 7 + [par] * 5,
        out_specs=seq,
        scratch_shapes=[pltpu.VMEM((hp, HEAD_N, HEAD_N), F32)],
        compiler_params=_params("parallel", "arbitrary"),
        name="rwkv_scan",
    )(r, k, v, lw, cs, a, g, row(k_k), row(k_a), row(r_k), row(lnx_w), row(lnx_b))


def _rope_kernel(q_ref, k_ref, v_ref, pos_ref, freq_ref, qo_ref, ko_ref, vo_ref):
    ang = pos_ref[...] * freq_ref[...]
    lane = lax.broadcasted_iota(jnp.int32, ang.shape, 1)
    first_half = (lane % DIFF_QK) < (DIFF_QK // 2)
    cos = jnp.cos(ang)
    sin = jnp.where(first_half, -1.0, 1.0) * jnp.sin(ang)
    half = DIFF_QK // 2
    scale = DIFF_QK ** -0.5
    for b in range(q_ref.shape[1] // LANES):
        ls = slice(b * LANES, (b + 1) * LANES)
        for src, dst, sc in ((q_ref, qo_ref, scale), (k_ref, ko_ref, 1.0)):
            x = src[:, ls]
            partner = jnp.where(first_half, pltpu.roll(x, LANES - half, axis=1), pltpu.roll(x, half, axis=1))
            dst[:, ls] = ((x * cos + partner * sin) * sc).astype(dst.dtype)
    vo_ref[...] = v_ref[...].astype(vo_ref.dtype)


def rope(proj_d, positions, *, tm=256):
    S = proj_d.shape[0]
    tm = min(tm, S)
    W = DIFF_WIDTH
    inv_freq = ROPE_THETA ** (-(jnp.arange(0, DIFF_QK, 2, dtype=F32) / DIFF_QK))
    freq = jnp.tile(inv_freq, LANES // (DIFF_QK // 2)).reshape(1, LANES)
    pos = positions.reshape(S, 1).astype(F32)
    out = jax.ShapeDtypeStruct((S, W), BF16)
    return pl.pallas_call(
        _rope_kernel,
        out_shape=[out] * 3,
        grid=(S // tm,),
        in_specs=[pl.BlockSpec((tm, W), lambda i: (i, 0)),
                  pl.BlockSpec((tm, W), lambda i: (i, 1)),
                  pl.BlockSpec((tm, W), lambda i: (i, 2)),
                  pl.BlockSpec((tm, 1), lambda i: (i, 0)),
                  pl.BlockSpec((1, LANES), lambda i: (0, 0))],
        out_specs=[pl.BlockSpec((tm, W), lambda i: (i, 0))] * 3,
        compiler_params=_params("parallel"),
        name="rope",
    )(proj_d, proj_d, proj_d, pos, freq)


def _diff_flash_kernel(qi_tab, ki_tab, q_ref, k_ref, v_ref, lam_ref, sw_ref, o_ref,
                       m_ref, l_ref, acc_ref, s_ref, p_ref, alpha_ref, *, tile, row_block, diag_parts, lambda_init):
    p = pl.program_id(1)
    qi = qi_tab[p]
    ki = ki_tab[p]

    @pl.when(ki == 0)
    def _():
        m_ref[...] = jnp.full_like(m_ref, -jnp.inf)
        l_ref[...] = jnp.zeros_like(l_ref)
        acc_ref[...] = jnp.zeros_like(acc_ref)

    def process(row0, nrows, kmax, masked):
        rows = slice(row0, row0 + nrows)
        q = q_ref[rows, :]
        k = k_ref[0:kmax, :]
        v = v_ref[0:kmax, :]
        lane = lax.broadcasted_iota(jnp.int32, q.shape, 1)
        reps = kmax // LANES
        for c in range(2):
            qc = jnp.where((lane >> (DIFF_QK.bit_length() - 1)) == c, q, jnp.zeros_like(q))
            s_ref[0:nrows, 0:kmax] = _dot_nt(qc, k)
            for r0 in range(0, nrows, row_block):
                ar = slice(row0 + r0, row0 + r0 + row_block)
                lr = slice(r0, r0 + row_block)
                s = s_ref[lr, 0:kmax]
                if masked:
                    qpos = lax.broadcasted_iota(jnp.int32, s.shape, 0) + (row0 + r0)
                    kpos = lax.broadcasted_iota(jnp.int32, s.shape, 1)
                    s = jnp.where(kpos <= qpos, s, -jnp.inf)
                m_old = m_ref[c, ar, :]
                m_new = jnp.maximum(m_old, jnp.max(s, axis=-1, keepdims=True))
                alpha = jnp.exp(m_old - m_new)
                pr = jnp.exp(s - jnp.concatenate([m_new] * reps, axis=1))
                l_ref[c, ar, :] = alpha * l_ref[c, ar, :] + jnp.sum(pr, axis=-1, keepdims=True)
                m_ref[c, ar, :] = m_new
                alpha_ref[lr, :] = alpha
                p_ref[lr, 0:kmax] = pr.astype(BF16)
            acc_ref[c, rows, :] = (alpha_ref[0:nrows, :] * acc_ref[c, rows, :]
                                   + _dot(p_ref[0:nrows, 0:kmax], v))

    @pl.when(ki < qi)
    def _():
        process(0, tile, tile, False)

    @pl.when(ki == qi)
    def _():
        part = tile // diag_parts
        for a in range(diag_parts):
            process(a * part, part, (a + 1) * part, True)
        lp = lam_ref[...]
        lam = (jnp.exp(jnp.sum(lp[0:1] * lp[1:2], axis=-1, keepdims=True))
               - jnp.exp(jnp.sum(lp[2:3] * lp[3:4], axis=-1, keepdims=True)) + lambda_init)
        o = acc_ref[0] / l_ref[0] - lam * (acc_ref[1] / l_ref[1])
        o = o * lax.rsqrt(jnp.mean(o * o, axis=-1, keepdims=True) + SUBLN_EPS)
        o_ref[...] = o * sw_ref[...] * (1.0 - lambda_init)


def diff_flash(q, k, v, lambdas, subln_w, lambda_init, *, tile=1024, row_block=32, diag_parts=4):
    S = q.shape[0]
    tile = min(tile, S)
    assert S % tile == 0 and (tile // diag_parts) % LANES == 0 and (tile // diag_parts) % row_block == 0
    pairs = [(qi, ki) for qi in range(S // tile) for ki in range(qi + 1)]
    qi_tab = jnp.asarray([p[0] for p in pairs], jnp.int32)
    ki_tab = jnp.asarray([p[1] for p in pairs], jnp.int32)
    grid_spec = pltpu.PrefetchScalarGridSpec(
        num_scalar_prefetch=2,
        grid=(DIFF_HEADS, len(pairs)),
        in_specs=[pl.BlockSpec((tile, LANES), lambda h, p, qt, kt: (qt[p], h)),
                  pl.BlockSpec((tile, LANES), lambda h, p, qt, kt: (kt[p], h)),
                  pl.BlockSpec((tile, LANES), lambda h, p, qt, kt: (kt[p], h)),
                  pl.BlockSpec((4, DIFF_QK), lambda h, p, qt, kt: (0, 0)),
                  pl.BlockSpec((1, DIFF_V), lambda h, p, qt, kt: (0, 0))],
        out_specs=pl.BlockSpec((tile, LANES), lambda h, p, qt, kt: (qt[p], h)),
        scratch_shapes=[pltpu.VMEM((2, tile, LANES), F32),
                        pltpu.VMEM((2, tile, LANES), F32),
                        pltpu.VMEM((2, tile, DIFF_V), F32),
                        pltpu.VMEM((tile, tile), F32),
                        pltpu.VMEM((tile, tile), BF16),
                        pltpu.VMEM((tile, LANES), F32)],
    )
    return pl.pallas_call(
        functools.partial(_diff_flash_kernel, tile=tile, row_block=row_block, diag_parts=diag_parts,
                          lambda_init=lambda_init),
        out_shape=jax.ShapeDtypeStruct((S, DIFF_WIDTH), F32),
        grid_spec=grid_spec,
        compiler_params=_params("parallel", "arbitrary"),
        name="diff_flash",
    )(qi_tab, ki_tab, q, k, v, lambdas.astype(F32), subln_w.reshape(1, DIFF_V).astype(F32))


def _mem_attn_kernel(q_ref, k_ref, v_ref, o_ref):
    D = q_ref.shape[1]
    hd = D // MEM_HEADS
    scale = hd ** -0.5
    for h in range(MEM_HEADS):
        ls = slice(h * hd, (h + 1) * hd)
        s = _dot_nt(q_ref[:, ls], k_ref[:, ls]) * scale
        s = s - jnp.max(s, axis=-1, keepdims=True)
        e = jnp.exp(s)
        pr = e / jnp.sum(e, axis=-1, keepdims=True)
        o_ref[:, ls] = _dot(pr.astype(BF16), v_ref[:, ls]).astype(o_ref.dtype)


def mem_attn(q, k, v, *, tq=512):
    S, D = q.shape
    M = k.shape[0]
    tq = min(tq, S)
    return pl.pallas_call(
        _mem_attn_kernel,
        out_shape=jax.ShapeDtypeStruct((S, D), BF16),
        grid=(S // tq,),
        in_specs=[pl.BlockSpec((tq, D), lambda i: (i, 0)),
                  pl.BlockSpec((M, D), lambda i: (0, 0)),
                  pl.BlockSpec((M, D), lambda i: (0, 0))],
        out_specs=pl.BlockSpec((tq, D), lambda i: (i, 0)),
        compiler_params=_params("parallel"),
        name="mem_attn",
    )(q, k, v)


def _first_argmax(x, lane, big):
    m = jnp.max(x, axis=-1, keepdims=True)
    idx = jnp.min(jnp.where(x == m, lane, big), axis=-1, keepdims=True)
    return m, idx


def _router_kernel(h_ref, gain_ref, wg_ref, bg_ref, we_ref, be_ref, hn_ref, eid_ref, ew_ref, cnt_ref):
    @pl.when(pl.program_id(0) == 0)
    def _():
        cnt_ref[...] = jnp.zeros_like(cnt_ref)

    x = h_ref[...]
    hn = x * lax.rsqrt(jnp.mean(x * x, axis=-1, keepdims=True) + NORM_EPS) * gain_ref[...]
    hn_ref[...] = hn
    g_logits = _dot(hn, wg_ref[...], HIGHEST) + bg_ref[...]
    e_logits = _dot(hn, we_ref[...], HIGHEST) + be_ref[...]
    tm = x.shape[0]
    lane_g = lax.broadcasted_iota(jnp.int32, (tm, N_GROUPS), 1)
    g_max, g_idx = _first_argmax(g_logits, lane_g, N_GROUPS)
    g_w = 1.0 / jnp.sum(jnp.exp(g_logits - g_max), axis=-1, keepdims=True)
    lane_e = lax.broadcasted_iota(jnp.int32, (tm, N_EXPERTS), 1)
    in_group = (lane_e // EXPERTS_PER_GROUP) == g_idx
    el = jnp.where(in_group, e_logits, -jnp.inf)
    e_max = jnp.max(el, axis=-1, keepdims=True)
    ex = jnp.exp(el - e_max)
    prob = ex / jnp.sum(ex, axis=-1, keepdims=True)
    prob = jnp.where(in_group, prob, -1.0)
    p1, i1 = _first_argmax(prob, lane_e, N_EXPERTS)
    p2, i2 = _first_argmax(jnp.where(lane_e == i1, -1.0, prob), lane_e, N_EXPERTS)
    tot = p1 + p2
    lane_o = lax.broadcasted_iota(jnp.int32, (tm, LANES), 1)
    eid_ref[...] = jnp.where(lane_o == 0, i1, jnp.where(lane_o == 1, i2, 0))
    ew_ref[...] = jnp.where(lane_o == 0, g_w * (p1 / tot), jnp.where(lane_o == 1, g_w * (p2 / tot), 0.0))
    chosen = ((lane_o == i1) | (lane_o == i2)).astype(F32)
    cnt_ref[...] = cnt_ref[...] + jnp.sum(chosen, axis=0, keepdims=True).astype(jnp.int32)


def router(h, gain, wg, bg, we, be, *, tm=512):
    S, D = h.shape
    tm = min(tm, S)
    full = lambda a: pl.BlockSpec(a.shape, lambda i: (0, 0))
    args = [h, gain.reshape(1, D).astype(F32), wg.astype(F32), bg.reshape(1, -1).astype(F32),
            we.astype(F32), be.reshape(1, -1).astype(F32)]
    return pl.pallas_call(
        _router_kernel,
        out_shape=[jax.ShapeDtypeStruct((S, D), F32),
                   jax.ShapeDtypeStruct((S, LANES), jnp.int32),
                   jax.ShapeDtypeStruct((S, LANES), F32),
                   jax.ShapeDtypeStruct((1, LANES), jnp.int32)],
        grid=(S // tm,),
        in_specs=[pl.BlockSpec((tm, D), lambda i: (i, 0))] + [full(a) for a in args[1:]],
        out_specs=[pl.BlockSpec((tm, D), lambda i: (i, 0)),
                   pl.BlockSpec((tm, LANES), lambda i: (i, 0)),
                   pl.BlockSpec((tm, LANES), lambda i: (i, 0)),
                   pl.BlockSpec((1, LANES), lambda i: (0, 0))],
        compiler_params=_params("arbitrary"),
        name="router",
    )(*args)


def _plan_kernel(cnt_ref, blk_e_ref, blk_n_ref, start_ref):
    nb = blk_e_ref.shape[0]
    sh = MOE_ROWS.bit_length() - 1

    def per_expert(e, b):
        n = cnt_ref[e]
        start_ref[e] = b

        def per_block(t, _):
            blk_e_ref[b + t] = e
            blk_n_ref[b + t] = jnp.minimum(n - t * MOE_ROWS, MOE_ROWS)
            return 0
        k = (n + (MOE_ROWS - 1)) >> sh
        lax.fori_loop(0, k, per_block, 0)
        return b + k
    total = lax.fori_loop(0, N_EXPERTS, per_expert, 0)

    def rest(e, _):
        start_ref[e] = total
        return 0
    lax.fori_loop(N_EXPERTS, start_ref.shape[0], rest, 0)
    last_e = blk_e_ref[jnp.maximum(total - 1, 0)]

    def tail(b, _):
        blk_e_ref[b] = last_e
        blk_n_ref[b] = 0
        return 0
    lax.fori_loop(total, nb, tail, 0)


def block_plan(counts, n_assign):
    nb = (n_assign + N_EXPERTS * (MOE_ROWS - 1)) // MOE_ROWS
    smem = pl.BlockSpec(memory_space=pltpu.SMEM)
    return pl.pallas_call(
        _plan_kernel,
        out_shape=[jax.ShapeDtypeStruct((nb,), jnp.int32), jax.ShapeDtypeStruct((nb,), jnp.int32),
                   jax.ShapeDtypeStruct((LANES,), jnp.int32)],
        in_specs=[smem],
        out_specs=[smem, smem, smem],
        name="block_plan",
    )(counts.reshape(LANES))


def _dest_kernel(eid_ref, start_ref, dest_ref, carry_ref):
    @pl.when(pl.program_id(0) == 0)
    def _():
        carry_ref[...] = jnp.zeros_like(carry_ref)

    eid = eid_ref[...]
    tm = eid.shape[0]
    lane = lax.broadcasted_iota(jnp.int32, (tm, LANES), 1)
    oh0 = lane == eid[:, 0:1]
    oh1 = lane == eid[:, 1:2]
    both = jnp.where(oh0 | oh1, 1.0, 0.0).astype(BF16)
    r = lax.broadcasted_iota(jnp.int32, (tm, tm), 0)
    c = lax.broadcasted_iota(jnp.int32, (tm, tm), 1)
    before = _dot(jnp.where(r > c, 1.0, 0.0).astype(BF16), both) + carry_ref[...]
    pos = before + (start_ref[...] * MOE_ROWS).astype(F32)
    d0 = jnp.sum(jnp.where(oh0, pos, 0.0), axis=-1, keepdims=True).astype(jnp.int32)
    d1 = jnp.sum(jnp.where(oh1, pos, 0.0), axis=-1, keepdims=True).astype(jnp.int32)
    dest_ref[...] = jnp.where(lane == 0, d0, jnp.where(lane == 1, d1, 0))
    carry_ref[...] = carry_ref[...] + jnp.sum(both.astype(F32), axis=0, keepdims=True)


def assignment_dest(eid, blk_start, *, tm=512):
    S = eid.shape[0]
    tm = min(tm, S)
    return pl.pallas_call(
        _dest_kernel,
        out_shape=jax.ShapeDtypeStruct((S, LANES), jnp.int32),
        grid=(S // tm,),
        in_specs=[pl.BlockSpec((tm, LANES), lambda i: (i, 0)), pl.BlockSpec((1, LANES), lambda i: (0, 0))],
        out_specs=pl.BlockSpec((tm, LANES), lambda i: (i, 0)),
        scratch_shapes=[pltpu.VMEM((1, LANES), F32)],
        compiler_params=_params("arbitrary"),
        name="assignment_dest",
    )(eid, blk_start.reshape(1, LANES))


def _for_rows(n, fn):
    sh = MOE_DMA_UNROLL.bit_length() - 1

    def group(t, _):
        for u in range(MOE_DMA_UNROLL):
            fn(t * MOE_DMA_UNROLL + u)
        return 0
    lax.fori_loop(0, n >> sh, group, 0)

    def one(r, _):
        fn(r)
        return 0
    lax.fori_loop((n >> sh) << sh, n, one, 0)


def _experts_kernel(blk_e, blk_n, dest, hn_hbm, wg_ref, wu_ref, wd_ref, out_hbm,
                    xg_ref, xb_ref, acc_ref, wgb_ref, wub_ref, wdb_ref, inv_ref, gsem, ssem):
    i = pl.program_id(0)
    j = pl.program_id(1)
    nb = pl.num_programs(0)
    nsplit = pl.num_programs(1)
    n_tok = hn_hbm.shape[0]
    sub_shift = MOE_SUB.bit_length() - 1
    k_shift = TOP_K.bit_length() - 1

    def n_sub_of(b):
        return (blk_n[b] + (MOE_SUB - 1)) >> sub_shift

    def gather(b, r):
        tok = inv_ref[b * MOE_ROWS + r] >> k_shift
        return pltpu.make_async_copy(hn_hbm.at[pl.ds(tok, 1)], xg_ref.at[pl.ds(r, 1)], gsem)

    def scatter(b, r):
        a = inv_ref[b * MOE_ROWS + r]
        row = (a & (TOP_K - 1)) * n_tok + (a >> k_shift)
        return pltpu.make_async_copy(acc_ref.at[b & 1, pl.ds(r, 1)], out_hbm.at[pl.ds(row, 1)], ssem)

    def start_gather(b):
        def pad(r, _):
            inv_ref[b * MOE_ROWS + r] = 0
            return 0
        lax.fori_loop(blk_n[b], n_sub_of(b) << sub_shift, pad, 0)
        _for_rows(n_sub_of(b) << sub_shift, lambda r: gather(b, r).start(priority=1))

    def wait_gather(b):
        _for_rows(n_sub_of(b) << sub_shift, lambda r: gather(b, r).wait())

    def start_scatter(b):
        _for_rows(blk_n[b], lambda r: scatter(b, r).start(priority=1))

    def wait_scatter(b):
        _for_rows(blk_n[b], lambda r: scatter(b, r).wait())

    n_sub = n_sub_of(i)
    slot = i & 1

    @pl.when(j == 0)
    def _():
        @pl.when(i == 0)
        def _():
            def invert(a):
                inv_ref[dest[a]] = a
            _for_rows(dest.shape[0], invert)
            start_gather(0)
        wait_gather(i)

        def cast(sb, _):
            rs = pl.ds(pl.multiple_of(sb * MOE_SUB, MOE_SUB), MOE_SUB)
            xb_ref[rs, :] = xg_ref[rs, :].astype(BF16)
            return 0
        lax.fori_loop(0, n_sub, cast, 0)

        @pl.when(i + 1 < nb)
        def _():
            start_gather(i + 1)

    @pl.when(n_sub > 0)
    def _():
        wgb_ref[...] = wg_ref[0].astype(BF16)
        wub_ref[...] = wu_ref[0].astype(BF16)
        wdb_ref[...] = wd_ref[0].astype(BF16)

    def sub_block(sb, _):
        rs = pl.ds(pl.multiple_of(sb * MOE_SUB, MOE_SUB), MOE_SUB)
        xb = xb_ref[rs, :]
        gate = _dot(xb, wgb_ref[...])
        hid = (gate * _sigmoid(gate)) * _dot(xb, wub_ref[...])
        part = _dot(hid.astype(BF16), wdb_ref[...])

        @pl.when(j == 0)
        def _():
            acc_ref[slot, rs, :] = part

        @pl.when(j > 0)
        def _():
            acc_ref[slot, rs, :] = acc_ref[slot, rs, :] + part
        return 0
    lax.fori_loop(0, n_sub, sub_block, 0)

    @pl.when(j == nsplit - 1)
    def _():
        @pl.when(i > 0)
        def _():
            wait_scatter(i - 1)
        start_scatter(i)

        @pl.when(i == nb - 1)
        def _():
            wait_scatter(i)


def experts(hn, blk_e, blk_n, dest, w_gate, w_up, w_down):
    S, D = hn.shape
    E, _, DE = w_gate.shape
    nb = blk_e.shape[0]
    de = DE // MOE_SPLIT
    split = lambda i, j, bn: jnp.where(bn[i] > 0, j, MOE_SPLIT - 1)
    grid_spec = pltpu.PrefetchScalarGridSpec(
        num_scalar_prefetch=3,
        grid=(nb, MOE_SPLIT),
        in_specs=[pl.BlockSpec(memory_space=pl.ANY),
                  pl.BlockSpec((1, D, de), lambda i, j, be, bn, ds: (be[i], 0, split(i, j, bn))),
                  pl.BlockSpec((1, D, de), lambda i, j, be, bn, ds: (be[i], 0, split(i, j, bn))),
                  pl.BlockSpec((1, de, D), lambda i, j, be, bn, ds: (be[i], split(i, j, bn), 0))],
        out_specs=pl.BlockSpec(memory_space=pl.ANY),
        scratch_shapes=[pltpu.VMEM((MOE_ROWS, D), F32),
                        pltpu.VMEM((MOE_ROWS, D), BF16),
                        pltpu.VMEM((2, MOE_ROWS, D), F32),
                        pltpu.VMEM((D, de), BF16),
                        pltpu.VMEM((D, de), BF16),
                        pltpu.VMEM((de, D), BF16),
                        pltpu.SMEM((nb * MOE_ROWS,), jnp.int32),
                        pltpu.SemaphoreType.DMA(()),
                        pltpu.SemaphoreType.DMA(())],
    )
    return pl.pallas_call(
        _experts_kernel,
        out_shape=jax.ShapeDtypeStruct((TOP_K * S, D), F32),
        grid_spec=grid_spec,
        compiler_params=_params("arbitrary", "arbitrary"),
        name="experts",
    )(blk_e, blk_n, dest, hn, w_gate, w_up, w_down)


def _combine_kernel(*refs, has_norm):
    h_ref, y0_ref, y1_ref, w_ref = refs[:4]
    o_ref = refs[-1]
    w = w_ref[...]
    x = h_ref[...] + (y0_ref[0] * w[:, 0:1] + y1_ref[0] * w[:, 1:2])
    if has_norm:
        x = x * lax.rsqrt(jnp.mean(x * x, axis=-1, keepdims=True) + NORM_EPS) * refs[4][...]
    o_ref[...] = x


def combine(h, y_slots, slot_w, gain=None, *, tm=512):
    S, D = h.shape
    tm = min(tm, S)
    y3 = y_slots.reshape(TOP_K, S, D)
    in_specs = [pl.BlockSpec((tm, D), lambda i: (i, 0)),
                pl.BlockSpec((1, tm, D), lambda i: (0, i, 0)),
                pl.BlockSpec((1, tm, D), lambda i: (1, i, 0)),
                pl.BlockSpec((tm, LANES), lambda i: (i, 0))]
    args = [h, y3, y3, slot_w]
    if gain is not None:
        in_specs.append(pl.BlockSpec((1, D), lambda i: (0, 0)))
        args.append(gain.reshape(1, D).astype(F32))
    return pl.pallas_call(
        functools.partial(_combine_kernel, has_norm=gain is not None),
        out_shape=jax.ShapeDtypeStruct((S, D), F32),
        grid=(S // tm,),
        in_specs=in_specs,
        out_specs=pl.BlockSpec((tm, D), lambda i: (i, 0)),
        compiler_params=_params("parallel"),
        name="combine",
    )(*args)


def moe(h, gain, wg, bg, we, be, w_gate, w_up, w_down, final_gain=None):
    S = h.shape[0]
    hn, eid, slot_w, counts = router(h, gain, wg, bg, we, be)
    blk_e, blk_n, blk_start = block_plan(counts, S * TOP_K)
    dest = assignment_dest(eid, blk_start)[:, :TOP_K].reshape(S * TOP_K)
    y_slots = experts(hn, blk_e, blk_n, dest, w_gate, w_up, w_down)
    return combine(h, y_slots, slot_w, final_gain)


def kernel(x, mem, positions, mix_norm, w_in, shift_mu, decay_w0, decay_w2, aaa_a0, aaa_a2, gate_g2, k_k, k_a,
           r_k, lnx_w, lnx_b, diff_lambda, subln_w, w_out, mem_q_norm, mem_kv_norm, wq_mem, wk_mem, wv_mem,
           wo_mem, moe_norm, router_group_w, router_group_b, router_expert_w, router_expert_b, expert_gate,
           expert_up, expert_down, final_norm):
    B, S, D = x.shape
    depth = w_in.shape[0]
    outs = []
    for b in range(B):
        h = x[b]
        memb = mem[b]
        for l in range(depth):
            lambda_init = 0.8 - 0.6 * math.exp(-0.3 * l)
            w_in_b = w_in[l].astype(BF16)
            proj_r = matmul(h, w_in_b[:, :RWKV_COLS], gain=mix_norm[l], tn=RWKV_COLS // 2)
            proj_d = matmul(h, w_in_b[:, RWKV_COLS:], gain=mix_norm[l], tn=1024)
            r, k, v, lw, cs, a, g = rwkv_prep(proj_r, shift_mu[l], decay_w0[l], decay_w2[l], aaa_a0[l],
                                              aaa_a2[l], gate_g2[l])
            y_rwkv = rwkv_scan(r, k, v, lw, cs, a, g, k_k[l], k_a[l], r_k[l], lnx_w[l], lnx_b[l])
            qr, kr, vb = rope(proj_d, positions[b])
            y_diff = diff_flash(qr, kr, vb, diff_lambda[l], subln_w[l], lambda_init)
            mixed = jnp.concatenate([y_rwkv, y_diff], axis=-1)
            h = matmul(mixed, w_out[l].astype(BF16), residual=h)
            q = matmul(h, wq_mem[l].astype(BF16), gain=mem_q_norm[l], out_dtype=BF16)
            km = matmul(memb, wk_mem[l].astype(BF16), gain=mem_kv_norm[l], out_dtype=BF16)
            vm = matmul(memb, wv_mem[l].astype(BF16), gain=mem_kv_norm[l], out_dtype=BF16)
            o = mem_attn(q, km, vm)
            h = matmul(o, wo_mem[l].astype(BF16), residual=h)
            h = moe(h, moe_norm[l], router_group_w[l], router_group_b[l], router_expert_w[l], router_expert_b[l],
                    expert_gate[l], expert_up[l], expert_down[l], final_norm if l == depth - 1 else None)
        outs.append(h)
    return jnp.stack(outs, axis=0)
```

```python
import functools
import math

import numpy as np
import jax
import jax.numpy as jnp
from jax import lax
from jax.experimental import pallas as pl
from jax.experimental.pallas import tpu as pltpu

F32 = jnp.float32
BF16 = jnp.bfloat16
HIGHEST = lax.Precision.HIGHEST

RWKV_HEADS = 16
HEAD_N = 64
RWKV_WIDTH = RWKV_HEADS * HEAD_N
DECAY_LORA = 64
AAA_LORA = 64
GATE_LORA = 128
RWKV_COLS = 3 * RWKV_WIDTH + DECAY_LORA + AAA_LORA + GATE_LORA
DIFF_HEADS = 8
DIFF_QK = 64
DIFF_V = 128
DIFF_WIDTH = DIFF_HEADS * DIFF_V
ROPE_THETA = 10000.0
MEM_HEADS = 4
N_GROUPS = 8
EXPERTS_PER_GROUP = 8
N_EXPERTS = N_GROUPS * EXPERTS_PER_GROUP
TOP_K = 2
NORM_EPS = 1e-6
LNX_EPS = 64e-5
SUBLN_EPS = 1e-5

LANES = 128
VMEM_LIMIT = 56 * 1024 * 1024

CHUNK = 64
SCAN_ROWS = 256
SCAN_WIDTH = 512
MOE_ROWS = 512
MOE_SUB = 256
MOE_SPLIT = 4
MOE_DMA_UNROLL = 8


def _params(*sem):
    return pltpu.CompilerParams(dimension_semantics=sem, vmem_limit_bytes=VMEM_LIMIT)


def _mm_kernel(*refs, n_x, has_norm, has_res):
    it = iter(refs)
    x_refs = [next(it) for _ in range(n_x)]
    g_ref = next(it) if has_norm else None
    w_ref = next(it)
    r_ref = next(it) if has_res else None
    o_ref = next(it)
    xs_ref = next(it)

    @pl.when(pl.program_id(1) == 0)
    def _():
        if has_norm:
            x = x_refs[0][...].astype(F32)
            x = x * lax.rsqrt(jnp.mean(x * x, axis=-1, keepdims=True) + NORM_EPS) * g_ref[...]
            xs_ref[...] = x.astype(BF16)
        else:
            k0 = 0
            for x_ref in x_refs:
                xs_ref[:, k0:k0 + x_ref.shape[1]] = x_ref[...].astype(BF16)
                k0 += x_ref.shape[1]

    acc = jnp.dot(xs_ref[...], w_ref[...], preferred_element_type=F32)
    if has_res:
        acc = acc + r_ref[...]
    o_ref[...] = acc.astype(o_ref.dtype)


def matmul(x, w, *, gain=None, residual=None, out_dtype=F32, tm=512, tn=512):
    xs = list(x) if isinstance(x, (list, tuple)) else [x]
    assert gain is None or len(xs) == 1
    M = xs[0].shape[0]
    K, N = w.shape
    assert sum(p.shape[1] for p in xs) == K
    tm = min(tm, M)
    assert M % tm == 0 and N % tn == 0, (M, N, tm, tn)
    in_specs = [pl.BlockSpec((tm, p.shape[1]), lambda i, j: (i, 0)) for p in xs]
    args = list(xs)
    if gain is not None:
        in_specs.append(pl.BlockSpec((1, K), lambda i, j: (0, 0)))
        args.append(gain.reshape(1, K).astype(F32))
    in_specs.append(pl.BlockSpec((K, tn), lambda i, j: (0, j)))
    args.append(w)
    if residual is not None:
        in_specs.append(pl.BlockSpec((tm, tn), lambda i, j: (i, j)))
        args.append(residual)
    return pl.pallas_call(
        functools.partial(_mm_kernel, n_x=len(xs), has_norm=gain is not None, has_res=residual is not None),
        out_shape=jax.ShapeDtypeStruct((M, N), out_dtype),
        grid=(M // tm, N // tn),
        in_specs=in_specs,
        out_specs=pl.BlockSpec((tm, tn), lambda i, j: (i, j)),
        scratch_shapes=[pltpu.VMEM((tm, K), BF16)],
        compiler_params=_params("parallel", "arbitrary"),
        name="matmul",
    )(*args)


def _sigmoid(x):
    return 1.0 / (1.0 + jnp.exp(-x))


def _rwkv_prep_kernel(p_ref, pp_ref, mu_ref, w0_ref, w2_ref, a0_ref, a2_ref, g2_ref,
                      r_o, k_o, v_o, lw_o, cs_o, a_o, g_o):
    W = RWKV_WIDTH
    p = p_ref[...]
    last = jnp.where(pl.program_id(0) == 0, 0.0, pp_ref[7:8, :])
    prev = pltpu.roll(p, 1, axis=0)
    row = lax.broadcasted_iota(jnp.int32, p.shape, 0)
    prev = jnp.where(row == 0, last, prev)
    ps = p + (prev - p) * mu_ref[...]
    r_o[...] = ps[:, 0:W]
    k_o[...] = ps[:, W:2 * W]
    v_o[...] = ps[:, 2 * W:3 * W]
    o = 3 * W
    wd = ps[:, o:o + DECAY_LORA]
    ad = ps[:, o + DECAY_LORA:o + DECAY_LORA + AAA_LORA]
    gd = ps[:, o + DECAY_LORA + AAA_LORA:o + DECAY_LORA + AAA_LORA + GATE_LORA]
    z = w0_ref[...] + jnp.dot(jnp.tanh(wd).astype(BF16), w2_ref[...], preferred_element_type=F32)
    nz = -z
    softplus = jnp.maximum(nz, 0.0) + jnp.log(1.0 + jnp.exp(-jnp.abs(nz)))
    w_raw = -softplus - 0.5
    lw = -jnp.exp(w_raw)
    lw_o[...] = lw
    tm = p.shape[0]
    rr = lax.broadcasted_iota(jnp.int32, (tm, tm), 0)
    cc = lax.broadcasted_iota(jnp.int32, (tm, tm), 1)
    sh = CHUNK.bit_length() - 1
    tri = (((rr >> sh) == (cc >> sh)) & (rr >= cc)).astype(F32)
    cs_o[...] = _dot(tri, lw, HIGHEST)
    a_o[...] = _sigmoid(a0_ref[...] + jnp.dot(ad.astype(BF16), a2_ref[...], preferred_element_type=F32))
    g_o[...] = jnp.dot(_sigmoid(gd).astype(BF16), g2_ref[...], preferred_element_type=F32)


def rwkv_prep(p, mu, w0, w2, a0, a2, g2, *, tm=256):
    S = p.shape[0]
    tm = min(tm, S)
    W = RWKV_WIDTH
    row = lambda x: x.reshape(1, -1).astype(F32)
    full = lambda a: pl.BlockSpec(a.shape, lambda i: (0, 0))
    args = [p, p, row(mu), row(w0), w2.astype(BF16), row(a0), a2.astype(BF16), g2.astype(BF16)]
    in_specs = [pl.BlockSpec((tm, RWKV_COLS), lambda i: (i, 0)),
                pl.BlockSpec((8, RWKV_COLS), lambda i: (jnp.maximum(i * (tm // 8) - 1, 0), 0))]
    in_specs += [full(a) for a in args[2:]]
    out = jax.ShapeDtypeStruct((S, W), F32)
    return pl.pallas_call(
        _rwkv_prep_kernel,
        out_shape=[out] * 7,
        grid=(S // tm,),
        in_specs=in_specs,
        out_specs=[pl.BlockSpec((tm, W), lambda i: (i, 0))] * 7,
        compiler_params=_params("parallel"),
        name="rwkv_prep",
    )(*args)


def _dot_nt(a, b, precision=None):
    return lax.dot_general(a, b, (((1,), (1,)), ((), ())), preferred_element_type=F32, precision=precision)


def _dot_tn(a, b, precision=None):
    return lax.dot_general(a, b, (((0,), (0,)), ((), ())), preferred_element_type=F32, precision=precision)


def _dot(a, b, precision=None):
    return jnp.dot(a, b, preferred_element_type=F32, precision=precision)


def _rwkv_scan_kernel(r_ref, k_ref, v_ref, lw_ref, cs_ref, a_ref, g_ref, kk_ref, ka_ref, rk_ref, lnw_ref, lnb_ref,
                      y_ref, state_ref):
    C = CHUNK
    N = HEAD_N

    @pl.when(pl.program_id(1) == 0)
    def _():
        state_ref[...] = jnp.zeros_like(state_ref)

    row = lax.broadcasted_iota(jnp.int32, (C, C), 0)
    col = lax.broadcasted_iota(jnp.int32, (C, C), 1)
    eye = (row == col).astype(F32)
    strict = row > col
    incl = row >= col

    n_chunks = r_ref.shape[0] // C
    heads = r_ref.shape[1] // N
    items = [(j, q) for j in range(heads) for q in range(n_chunks)]
    G = range(len(items))

    def tile(ref, j, q):
        return ref[q * C:(q + 1) * C, j * N:(j + 1) * N]

    def par(ref, j):
        return ref[:, j * N:(j + 1) * N]

    r = [tile(r_ref, j, q) for j, q in items]
    v = [tile(v_ref, j, q) for j, q in items]
    k2, at, bt, kt, rt, pc = [], [], [], [], [], []
    for g, (j, q) in enumerate(items):
        k = tile(k_ref, j, q)
        a = tile(a_ref, j, q)
        cs = tile(cs_ref, j, q)
        kk = k * par(kk_ref, j)
        kk = kk / jnp.maximum(jnp.sqrt(jnp.sum(kk * kk, axis=-1, keepdims=True)), 1e-12)
        k2_ = k * (1.0 + (a - 1.0) * par(ka_ref, j))
        e_neg = jnp.exp(-cs)
        k2.append(k2_)
        at.append((-kk * jnp.exp(cs - tile(lw_ref, j, q))).astype(BF16))
        bt.append((kk * a) * e_neg)
        kt.append(k2_ * e_neg)
        rt.append(r[g] * jnp.exp(cs))
        pc.append(jnp.exp(cs[C - 1:C, :]))
    vb = [x.astype(BF16) for x in v]
    btb = [x.astype(BF16) for x in bt]
    ktb = [x.astype(BF16) for x in kt]
    rtb = [x.astype(BF16) for x in rt]
    n_ab = [jnp.where(strict, _dot_nt(at[g], btb[g]), 0.0) for g in G]
    a_ak = [jnp.where(strict, _dot_nt(at[g], ktb[g]), 0.0).astype(BF16) for g in G]
    a_rb = [jnp.where(incl, _dot_nt(rtb[g], btb[g]), 0.0).astype(BF16) for g in G]
    a_rk = [jnp.where(incl, _dot_nt(rtb[g], ktb[g]), 0.0).astype(BF16) for g in G]
    akv = [_dot(a_ak[g], vb[g]) for g in G]
    t = None
    b = 1
    while b < C:
        sh = (2 * b).bit_length() - 1
        low_left = ((row >> sh) == (col >> sh)) & ((row & b) != 0) & ((col & b) == 0)
        nb = [jnp.where(low_left, n_ab[g], 0.0) for g in G]
        if b == 1:
            t = [eye + nb[g] for g in G]
        else:
            tb = [t[g].astype(BF16) for g in G]
            z = [_dot(nb[g].astype(BF16), tb[g]).astype(BF16) for g in G]
            t = [t[g] + _dot(tb[g], z[g]) for g in G]
        b *= 2
    tb = [t[g].astype(BF16) for g in G]
    wm = [_dot(tb[g], at[g]).astype(BF16) for g in G]
    u0 = [_dot(tb[g], akv[g].astype(BF16)).astype(BF16) for g in G]
    rm = [(rt[g] + _dot(a_rb[g], wm[g])).astype(BF16) for g in G]
    y0 = [_dot(a_rb[g], u0[g]) + _dot(a_rk[g], vb[g]) for g in G]
    bp = [(bt[g] * pc[g]).astype(BF16) for g in G]
    kp = [(kt[g] * pc[g]).astype(BF16) for g in G]
    mp = [_dot_tn(wm[g], bp[g]).astype(BF16) for g in G]
    s_add = [_dot_tn(u0[g], bp[g]) + _dot_tn(vb[g], kp[g]) for g in G]

    for j in range(heads):
        s = state_ref[j]
        for q in range(n_chunks):
            g = j * n_chunks + q
            sb = s.astype(BF16)
            y = _dot_nt(rm[g], sb) + y0[g]
            s = s * pc[g] + _dot(sb, mp[g]) + s_add[g]
            mean = jnp.mean(y, axis=-1, keepdims=True)
            yc = y - mean
            var = jnp.mean(yc * yc, axis=-1, keepdims=True)
            yn = yc * lax.rsqrt(var + LNX_EPS) * par(lnw_ref, j) + par(lnb_ref, j)
            bonus = jnp.sum(r[g] * k2[g] * par(rk_ref, j), axis=-1, keepdims=True) * v[g]
            y_ref[q * C:(q + 1) * C, j * N:(j + 1) * N] = (yn + bonus) * tile(g_ref, j, q)
        state_ref[j] = s


def rwkv_scan(r, k, v, lw, cs, a, g, k_k, k_a, r_k, lnx_w, lnx_b, *, rows=SCAN_ROWS, width=SCAN_WIDTH):
    S, W = r.shape
    rows = min(rows, S)
    hp = width // HEAD_N
    row = lambda x: x.reshape(1, W).astype(F32)
    seq = pl.BlockSpec((rows, width), lambda h, c: (c, h))
    par = pl.BlockSpec((1, width), lambda h, c: (0, h))
    return pl.pallas_call(
        _rwkv_scan_kernel,
        out_shape=jax.ShapeDtypeStruct((S, W), F32),
        grid=(W // width, S // rows),
        in_specs=[seq] * 7 + [par] * 5,
        out_specs=seq,
        scratch_shapes=[pltpu.VMEM((hp, HEAD_N, HEAD_N), F32)],
        compiler_params=_params("parallel", "arbitrary"),
        name="rwkv_scan",
    )(r, k, v, lw, cs, a, g, row(k_k), row(k_a), row(r_k), row(lnx_w), row(lnx_b))


def _rope_kernel(q_ref, k_ref, v_ref, pos_ref, freq_ref, qo_ref, ko_ref, vo_ref):
    ang = pos_ref[...] * freq_ref[...]
    lane = lax.broadcasted_iota(jnp.int32, ang.shape, 1)
    first_half = (lane % DIFF_QK) < (DIFF_QK // 2)
    cos = jnp.cos(ang)
    sin = jnp.where(first_half, -1.0, 1.0) * jnp.sin(ang)
    half = DIFF_QK // 2
    scale = DIFF_QK ** -0.5
    for b in range(q_ref.shape[1] // LANES):
        ls = slice(b * LANES, (b + 1) * LANES)
        for src, dst, sc in ((q_ref, qo_ref, scale), (k_ref, ko_ref, 1.0)):
            x = src[:, ls]
            partner = jnp.where(first_half, pltpu.roll(x, LANES - half, axis=1), pltpu.roll(x, half, axis=1))
            dst[:, ls] = ((x * cos + partner * sin) * sc).astype(dst.dtype)
    vo_ref[...] = v_ref[...].astype(vo_ref.dtype)


def rope(proj_d, positions, *, tm=256):
    S = proj_d.shape[0]
    tm = min(tm, S)
    W = DIFF_WIDTH
    inv_freq = ROPE_THETA ** (-(jnp.arange(0, DIFF_QK, 2, dtype=F32) / DIFF_QK))
    freq = jnp.tile(inv_freq, LANES // (DIFF_QK // 2)).reshape(1, LANES)
    pos = positions.reshape(S, 1).astype(F32)
    out = jax.ShapeDtypeStruct((S, W), BF16)
    return pl.pallas_call(
        _rope_kernel,
        out_shape=[out] * 3,
        grid=(S // tm,),
        in_specs=[pl.BlockSpec((tm, W), lambda i: (i, 0)),
                  pl.BlockSpec((tm, W), lambda i: (i, 1)),
                  pl.BlockSpec((tm, W), lambda i: (i, 2)),
                  pl.BlockSpec((tm, 1), lambda i: (i, 0)),
                  pl.BlockSpec((1, LANES), lambda i: (0, 0))],
        out_specs=[pl.BlockSpec((tm, W), lambda i: (i, 0))] * 3,
        compiler_params=_params("parallel"),
        name="rope",
    )(proj_d, proj_d, proj_d, pos, freq)


def _diff_flash_kernel(qi_tab, ki_tab, q_ref, k_ref, v_ref, lam_ref, sw_ref, o_ref,
                       m_ref, l_ref, acc_ref, s_ref, p_ref, alpha_ref, *, tile, row_block, diag_parts, lambda_init):
    p = pl.program_id(1)
    qi = qi_tab[p]
    ki = ki_tab[p]

    @pl.when(ki == 0)
    def _():
        m_ref[...] = jnp.full_like(m_ref, -jnp.inf)
        l_ref[...] = jnp.zeros_like(l_ref)
        acc_ref[...] = jnp.zeros_like(acc_ref)

    def process(row0, nrows, kmax, masked):
        rows = slice(row0, row0 + nrows)
        q = q_ref[rows, :]
        k = k_ref[0:kmax, :]
        v = v_ref[0:kmax, :]
        lane = lax.broadcasted_iota(jnp.int32, q.shape, 1)
        reps = kmax // LANES
        for c in range(2):
            qc = jnp.where((lane >> (DIFF_QK.bit_length() - 1)) == c, q, jnp.zeros_like(q))
            s_ref[0:nrows, 0:kmax] = _dot_nt(qc, k)
            for r0 in range(0, nrows, row_block):
                ar = slice(row0 + r0, row0 + r0 + row_block)
                lr = slice(r0, r0 + row_block)
                s = s_ref[lr, 0:kmax]
                if masked:
                    qpos = lax.broadcasted_iota(jnp.int32, s.shape, 0) + (row0 + r0)
                    kpos = lax.broadcasted_iota(jnp.int32, s.shape, 1)
                    s = jnp.where(kpos <= qpos, s, -jnp.inf)
                m_old = m_ref[c, ar, :]
                m_new = jnp.maximum(m_old, jnp.max(s, axis=-1, keepdims=True))
                alpha = jnp.exp(m_old - m_new)
                pr = jnp.exp(s - jnp.concatenate([m_new] * reps, axis=1))
                l_ref[c, ar, :] = alpha * l_ref[c, ar, :] + jnp.sum(pr, axis=-1, keepdims=True)
                m_ref[c, ar, :] = m_new
                alpha_ref[lr, :] = alpha
                p_ref[lr, 0:kmax] = pr.astype(BF16)
            acc_ref[c, rows, :] = (alpha_ref[0:nrows, :] * acc_ref[c, rows, :]
                                   + _dot(p_ref[0:nrows, 0:kmax], v))

    @pl.when(ki < qi)
    def _():
        process(0, tile, tile, False)

    @pl.when(ki == qi)
    def _():
        part = tile // diag_parts
        for a in range(diag_parts):
            process(a * part, part, (a + 1) * part, True)
        lp = lam_ref[...]
        lam = (jnp.exp(jnp.sum(lp[0:1] * lp[1:2], axis=-1, keepdims=True))
               - jnp.exp(jnp.sum(lp[2:3] * lp[3:4], axis=-1, keepdims=True)) + lambda_init)
        o = acc_ref[0] / l_ref[0] - lam * (acc_ref[1] / l_ref[1])
        o = o * lax.rsqrt(jnp.mean(o * o, axis=-1, keepdims=True) + SUBLN_EPS)
        o_ref[...] = o * sw_ref[...] * (1.0 - lambda_init)


def diff_flash(q, k, v, lambdas, subln_w, lambda_init, *, tile=1024, row_block=32, diag_parts=4):
    S = q.shape[0]
    tile = min(tile, S)
    assert S % tile == 0 and (tile // diag_parts) % LANES == 0 and (tile // diag_parts) % row_block == 0
    pairs = [(qi, ki) for qi in range(S // tile) for ki in range(qi + 1)]
    qi_tab = jnp.asarray([p[0] for p in pairs], jnp.int32)
    ki_tab = jnp.asarray([p[1] for p in pairs], jnp.int32)
    grid_spec = pltpu.PrefetchScalarGridSpec(
        num_scalar_prefetch=2,
        grid=(DIFF_HEADS, len(pairs)),
        in_specs=[pl.BlockSpec((tile, LANES), lambda h, p, qt, kt: (qt[p], h)),
                  pl.BlockSpec((tile, LANES), lambda h, p, qt, kt: (kt[p], h)),
                  pl.BlockSpec((tile, LANES), lambda h, p, qt, kt: (kt[p], h)),
                  pl.BlockSpec((4, DIFF_QK), lambda h, p, qt, kt: (0, 0)),
                  pl.BlockSpec((1, DIFF_V), lambda h, p, qt, kt: (0, 0))],
        out_specs=pl.BlockSpec((tile, LANES), lambda h, p, qt, kt: (qt[p], h)),
        scratch_shapes=[pltpu.VMEM((2, tile, LANES), F32),
                        pltpu.VMEM((2, tile, LANES), F32),
                        pltpu.VMEM((2, tile, DIFF_V), F32),
                        pltpu.VMEM((tile, tile), F32),
                        pltpu.VMEM((tile, tile), BF16),
                        pltpu.VMEM((tile, LANES), F32)],
    )
    return pl.pallas_call(
        functools.partial(_diff_flash_kernel, tile=tile, row_block=row_block, diag_parts=diag_parts,
                          lambda_init=lambda_init),
        out_shape=jax.ShapeDtypeStruct((S, DIFF_WIDTH), F32),
        grid_spec=grid_spec,
        compiler_params=_params("parallel", "arbitrary"),
        name="diff_flash",
    )(qi_tab, ki_tab, q, k, v, lambdas.astype(F32), subln_w.reshape(1, DIFF_V).astype(F32))


def _mem_attn_kernel(q_ref, k_ref, v_ref, o_ref):
    D = q_ref.shape[1]
    hd = D // MEM_HEADS
    scale = hd ** -0.5
    for h in range(MEM_HEADS):
        ls = slice(h * hd, (h + 1) * hd)
        s = _dot_nt(q_ref[:, ls], k_ref[:, ls]) * scale
        s = s - jnp.max(s, axis=-1, keepdims=True)
        e = jnp.exp(s)
        pr = e / jnp.sum(e, axis=-1, keepdims=True)
        o_ref[:, ls] = _dot(pr.astype(BF16), v_ref[:, ls]).astype(o_ref.dtype)


def mem_attn(q, k, v, *, tq=512):
    S, D = q.shape
    M = k.shape[0]
    tq = min(tq, S)
    return pl.pallas_call(
        _mem_attn_kernel,
        out_shape=jax.ShapeDtypeStruct((S, D), BF16),
        grid=(S // tq,),
        in_specs=[pl.BlockSpec((tq, D), lambda i: (i, 0)),
                  pl.BlockSpec((M, D), lambda i: (0, 0)),
                  pl.BlockSpec((M, D), lambda i: (0, 0))],
        out_specs=pl.BlockSpec((tq, D), lambda i: (i, 0)),
        compiler_params=_params("parallel"),
        name="mem_attn",
    )(q, k, v)


def _first_argmax(x, lane, big):
    m = jnp.max(x, axis=-1, keepdims=True)
    idx = jnp.min(jnp.where(x == m, lane, big), axis=-1, keepdims=True)
    return m, idx


def _router_kernel(h_ref, gain_ref, wg_ref, bg_ref, we_ref, be_ref, hn_ref, eid_ref, ew_ref, cnt_ref):
    @pl.when(pl.program_id(0) == 0)
    def _():
        cnt_ref[...] = jnp.zeros_like(cnt_ref)

    x = h_ref[...]
    hn = x * lax.rsqrt(jnp.mean(x * x, axis=-1, keepdims=True) + NORM_EPS) * gain_ref[...]
    hn_ref[...] = hn
    g_logits = _dot(hn, wg_ref[...], HIGHEST) + bg_ref[...]
    e_logits = _dot(hn, we_ref[...], HIGHEST) + be_ref[...]
    tm = x.shape[0]
    lane_g = lax.broadcasted_iota(jnp.int32, (tm, N_GROUPS), 1)
    g_max, g_idx = _first_argmax(g_logits, lane_g, N_GROUPS)
    g_w = 1.0 / jnp.sum(jnp.exp(g_logits - g_max), axis=-1, keepdims=True)
    lane_e = lax.broadcasted_iota(jnp.int32, (tm, N_EXPERTS), 1)
    in_group = (lane_e // EXPERTS_PER_GROUP) == g_idx
    el = jnp.where(in_group, e_logits, -jnp.inf)
    e_max = jnp.max(el, axis=-1, keepdims=True)
    ex = jnp.exp(el - e_max)
    prob = ex / jnp.sum(ex, axis=-1, keepdims=True)
    prob = jnp.where(in_group, prob, -1.0)
    p1, i1 = _first_argmax(prob, lane_e, N_EXPERTS)
    p2, i2 = _first_argmax(jnp.where(lane_e == i1, -1.0, prob), lane_e, N_EXPERTS)
    tot = p1 + p2
    lane_o = lax.broadcasted_iota(jnp.int32, (tm, LANES), 1)
    eid_ref[...] = jnp.where(lane_o == 0, i1, jnp.where(lane_o == 1, i2, 0))
    ew_ref[...] = jnp.where(lane_o == 0, g_w * (p1 / tot), jnp.where(lane_o == 1, g_w * (p2 / tot), 0.0))
    chosen = ((lane_o == i1) | (lane_o == i2)).astype(F32)
    cnt_ref[...] = cnt_ref[...] + jnp.sum(chosen, axis=0, keepdims=True).astype(jnp.int32)


def router(h, gain, wg, bg, we, be, *, tm=512):
    S, D = h.shape
    tm = min(tm, S)
    full = lambda a: pl.BlockSpec(a.shape, lambda i: (0, 0))
    args = [h, gain.reshape(1, D).astype(F32), wg.astype(F32), bg.reshape(1, -1).astype(F32),
            we.astype(F32), be.reshape(1, -1).astype(F32)]
    return pl.pallas_call(
        _router_kernel,
        out_shape=[jax.ShapeDtypeStruct((S, D), F32),
                   jax.ShapeDtypeStruct((S, LANES), jnp.int32),
                   jax.ShapeDtypeStruct((S, LANES), F32),
                   jax.ShapeDtypeStruct((1, LANES), jnp.int32)],
        grid=(S // tm,),
        in_specs=[pl.BlockSpec((tm, D), lambda i: (i, 0))] + [full(a) for a in args[1:]],
        out_specs=[pl.BlockSpec((tm, D), lambda i: (i, 0)),
                   pl.BlockSpec((tm, LANES), lambda i: (i, 0)),
                   pl.BlockSpec((tm, LANES), lambda i: (i, 0)),
                   pl.BlockSpec((1, LANES), lambda i: (0, 0))],
        compiler_params=_params("arbitrary"),
        name="router",
    )(*args)


def _plan_kernel(cnt_ref, blk_e_ref, blk_n_ref, start_ref):
    nb = blk_e_ref.shape[0]
    sh = MOE_ROWS.bit_length() - 1

    def per_expert(e, b):
        n = cnt_ref[e]
        start_ref[e] = b

        def per_block(t, _):
            blk_e_ref[b + t] = e
            blk_n_ref[b + t] = jnp.minimum(n - t * MOE_ROWS, MOE_ROWS)
            return 0
        k = (n + (MOE_ROWS - 1)) >> sh
        lax.fori_loop(0, k, per_block, 0)
        return b + k
    total = lax.fori_loop(0, N_EXPERTS, per_expert, 0)

    def rest(e, _):
        start_ref[e] = total
        return 0
    lax.fori_loop(N_EXPERTS, start_ref.shape[0], rest, 0)
    last_e = blk_e_ref[jnp.maximum(total - 1, 0)]

    def tail(b, _):
        blk_e_ref[b] = last_e
        blk_n_ref[b] = 0
        return 0
    lax.fori_loop(total, nb, tail, 0)


def block_plan(counts, n_assign):
    nb = (n_assign + N_EXPERTS * (MOE_ROWS - 1)) // MOE_ROWS
    smem = pl.BlockSpec(memory_space=pltpu.SMEM)
    return pl.pallas_call(
        _plan_kernel,
        out_shape=[jax.ShapeDtypeStruct((nb,), jnp.int32), jax.ShapeDtypeStruct((nb,), jnp.int32),
                   jax.ShapeDtypeStruct((LANES,), jnp.int32)],
        in_specs=[smem],
        out_specs=[smem, smem, smem],
        name="block_plan",
    )(counts.reshape(LANES))


def _dest_kernel(eid_ref, start_ref, dest_ref, carry_ref):
    @pl.when(pl.program_id(0) == 0)
    def _():
        carry_ref[...] = jnp.zeros_like(carry_ref)

    eid = eid_ref[...]
    tm = eid.shape[0]
    lane = lax.broadcasted_iota(jnp.int32, (tm, LANES), 1)
    oh0 = lane == eid[:, 0:1]
    oh1 = lane == eid[:, 1:2]
    both = jnp.where(oh0 | oh1, 1.0, 0.0).astype(BF16)
    r = lax.broadcasted_iota(jnp.int32, (tm, tm), 0)
    c = lax.broadcasted_iota(jnp.int32, (tm, tm), 1)
    before = _dot(jnp.where(r > c, 1.0, 0.0).astype(BF16), both) + carry_ref[...]
    pos = before + (start_ref[...] * MOE_ROWS).astype(F32)
    d0 = jnp.sum(jnp.where(oh0, pos, 0.0), axis=-1, keepdims=True).astype(jnp.int32)
    d1 = jnp.sum(jnp.where(oh1, pos, 0.0), axis=-1, keepdims=True).astype(jnp.int32)
    dest_ref[...] = jnp.where(lane == 0, d0, jnp.where(lane == 1, d1, 0))
    carry_ref[...] = carry_ref[...] + jnp.sum(both.astype(F32), axis=0, keepdims=True)


def assignment_dest(eid, blk_start, *, tm=512):
    S = eid.shape[0]
    tm = min(tm, S)
    return pl.pallas_call(
        _dest_kernel,
        out_shape=jax.ShapeDtypeStruct((S, LANES), jnp.int32),
        grid=(S // tm,),
        in_specs=[pl.BlockSpec((tm, LANES), lambda i: (i, 0)), pl.BlockSpec((1, LANES), lambda i: (0, 0))],
        out_specs=pl.BlockSpec((tm, LANES), lambda i: (i, 0)),
        scratch_shapes=[pltpu.VMEM((1, LANES), F32)],
        compiler_params=_params("arbitrary"),
        name="assignment_dest",
    )(eid, blk_start.reshape(1, LANES))


def _for_rows(n, fn):
    sh = MOE_DMA_UNROLL.bit_length() - 1

    def group(t, _):
        for u in range(MOE_DMA_UNROLL):
            fn(t * MOE_DMA_UNROLL + u)
        return 0
    lax.fori_loop(0, n >> sh, group, 0)

    def one(r, _):
        fn(r)
        return 0
    lax.fori_loop((n >> sh) << sh, n, one, 0)


def _experts_kernel(blk_e, blk_n, dest, hn_hbm, wg_ref, wu_ref, wd_ref, out_hbm,
                    xg_ref, xb_ref, acc_ref, wgb_ref, wub_ref, wdb_ref, inv_ref, gsem, ssem):
    i = pl.program_id(0)
    j = pl.program_id(1)
    nb = pl.num_programs(0)
    nsplit = pl.num_programs(1)
    n_tok = hn_hbm.shape[0]
    sub_shift = MOE_SUB.bit_length() - 1
    k_shift = TOP_K.bit_length() - 1

    def n_sub_of(b):
        return (blk_n[b] + (MOE_SUB - 1)) >> sub_shift

    def gather(b, r):
        tok = inv_ref[b * MOE_ROWS + r] >> k_shift
        return pltpu.make_async_copy(hn_hbm.at[pl.ds(tok, 1)], xg_ref.at[pl.ds(r, 1)], gsem)

    def scatter(b, r):
        a = inv_ref[b * MOE_ROWS + r]
        row = (a & (TOP_K - 1)) * n_tok + (a >> k_shift)
        return pltpu.make_async_copy(acc_ref.at[b & 1, pl.ds(r, 1)], out_hbm.at[pl.ds(row, 1)], ssem)

    def start_gather(b):
        def pad(r, _):
            inv_ref[b * MOE_ROWS + r] = 0
            return 0
        lax.fori_loop(blk_n[b], n_sub_of(b) << sub_shift, pad, 0)
        _for_rows(n_sub_of(b) << sub_shift, lambda r: gather(b, r).start(priority=1))

    def wait_gather(b):
        n = pl.multiple_of(n_sub_of(b) << sub_shift, MOE_SUB)

        @pl.when(n > 0)
        def _():
            pltpu.make_async_copy(hn_hbm.at[pl.ds(0, n)], xg_ref.at[pl.ds(0, n)], gsem).wait()

    def start_scatter(b):
        _for_rows(blk_n[b], lambda r: scatter(b, r).start(priority=1))

    def wait_scatter(b):
        n8 = pl.multiple_of((blk_n[b] >> 3) << 3, 8)

        @pl.when(n8 > 0)
        def _():
            pltpu.make_async_copy(acc_ref.at[b & 1, pl.ds(0, n8)], out_hbm.at[pl.ds(0, n8)], ssem).wait()

        def one(r, _):
            scatter(b, r).wait()
            return 0
        lax.fori_loop(n8, blk_n[b], one, 0)

    n_sub = n_sub_of(i)
    slot = i & 1

    @pl.when(j == 0)
    def _():
        @pl.when(i == 0)
        def _():
            def invert(a):
                inv_ref[dest[a]] = a
            _for_rows(dest.shape[0], invert)
            start_gather(0)
        wait_gather(i)

        def cast(sb, _):
            rs = pl.ds(pl.multiple_of(sb * MOE_SUB, MOE_SUB), MOE_SUB)
            xb_ref[rs, :] = xg_ref[rs, :].astype(BF16)
            return 0
        lax.fori_loop(0, n_sub, cast, 0)

        @pl.when(i + 1 < nb)
        def _():
            start_gather(i + 1)

    @pl.when(n_sub > 0)
    def _():
        wgb_ref[...] = wg_ref[0].astype(BF16)
        wub_ref[...] = wu_ref[0].astype(BF16)
        wdb_ref[...] = wd_ref[0].astype(BF16)

    def sub_block(sb, _):
        rs = pl.ds(pl.multiple_of(sb * MOE_SUB, MOE_SUB), MOE_SUB)
        xb = xb_ref[rs, :]
        gate = _dot(xb, wgb_ref[...])
        hid = (gate * _sigmoid(gate)) * _dot(xb, wub_ref[...])
        part = _dot(hid.astype(BF16), wdb_ref[...])

        @pl.when(j == 0)
        def _():
            acc_ref[slot, rs, :] = part

        @pl.when(j > 0)
        def _():
            acc_ref[slot, rs, :] = acc_ref[slot, rs, :] + part
        return 0
    lax.fori_loop(0, n_sub, sub_block, 0)

    @pl.when(j == nsplit - 1)
    def _():
        @pl.when(i > 0)
        def _():
            wait_scatter(i - 1)
        start_scatter(i)

        @pl.when(i == nb - 1)
        def _():
            wait_scatter(i)


def experts(hn, blk_e, blk_n, dest, w_gate, w_up, w_down):
    S, D = hn.shape
    E, _, DE = w_gate.shape
    nb = blk_e.shape[0]
    de = DE // MOE_SPLIT
    split = lambda i, j, bn: jnp.where(bn[i] > 0, j, MOE_SPLIT - 1)
    grid_spec = pltpu.PrefetchScalarGridSpec(
        num_scalar_prefetch=3,
        grid=(nb, MOE_SPLIT),
        in_specs=[pl.BlockSpec(memory_space=pl.ANY),
                  pl.BlockSpec((1, D, de), lambda i, j, be, bn, ds: (be[i], 0, split(i, j, bn))),
                  pl.BlockSpec((1, D, de), lambda i, j, be, bn, ds: (be[i], 0, split(i, j, bn))),
                  pl.BlockSpec((1, de, D), lambda i, j, be, bn, ds: (be[i], split(i, j, bn), 0))],
        out_specs=pl.BlockSpec(memory_space=pl.ANY),
        scratch_shapes=[pltpu.VMEM((MOE_ROWS, D), F32),
                        pltpu.VMEM((MOE_ROWS, D), BF16),
                        pltpu.VMEM((2, MOE_ROWS, D), F32),
                        pltpu.VMEM((D, de), BF16),
                        pltpu.VMEM((D, de), BF16),
                        pltpu.VMEM((de, D), BF16),
                        pltpu.SMEM((nb * MOE_ROWS,), jnp.int32),
                        pltpu.SemaphoreType.DMA(()),
                        pltpu.SemaphoreType.DMA(())],
    )
    return pl.pallas_call(
        _experts_kernel,
        out_shape=jax.ShapeDtypeStruct((TOP_K * S, D), F32),
        grid_spec=grid_spec,
        compiler_params=_params("arbitrary", "arbitrary"),
        name="experts",
    )(blk_e, blk_n, dest, hn, w_gate, w_up, w_down)


def _combine_kernel(*refs, has_norm):
    h_ref, y0_ref, y1_ref, w_ref = refs[:4]
    o_ref = refs[-1]
    w = w_ref[...]
    x = h_ref[...] + (y0_ref[0] * w[:, 0:1] + y1_ref[0] * w[:, 1:2])
    if has_norm:
        x = x * lax.rsqrt(jnp.mean(x * x, axis=-1, keepdims=True) + NORM_EPS) * refs[4][...]
    o_ref[...] = x


def combine(h, y_slots, slot_w, gain=None, *, tm=512):
    S, D = h.shape
    tm = min(tm, S)
    y3 = y_slots.reshape(TOP_K, S, D)
    in_specs = [pl.BlockSpec((tm, D), lambda i: (i, 0)),
                pl.BlockSpec((1, tm, D), lambda i: (0, i, 0)),
                pl.BlockSpec((1, tm, D), lambda i: (1, i, 0)),
                pl.BlockSpec((tm, LANES), lambda i: (i, 0))]
    args = [h, y3, y3, slot_w]
    if gain is not None:
        in_specs.append(pl.BlockSpec((1, D), lambda i: (0, 0)))
        args.append(gain.reshape(1, D).astype(F32))
    return pl.pallas_call(
        functools.partial(_combine_kernel, has_norm=gain is not None),
        out_shape=jax.ShapeDtypeStruct((S, D), F32),
        grid=(S // tm,),
        in_specs=in_specs,
        out_specs=pl.BlockSpec((tm, D), lambda i: (i, 0)),
        compiler_params=_params("parallel"),
        name="combine",
    )(*args)


def moe(h, gain, wg, bg, we, be, w_gate, w_up, w_down, final_gain=None):
    S = h.shape[0]
    hn, eid, slot_w, counts = router(h, gain, wg, bg, we, be)
    blk_e, blk_n, blk_start = block_plan(counts, S * TOP_K)
    dest = assignment_dest(eid, blk_start)[:, :TOP_K].reshape(S * TOP_K)
    y_slots = experts(hn, blk_e, blk_n, dest, w_gate, w_up, w_down)
    return combine(h, y_slots, slot_w, final_gain)


def kernel(x, mem, positions, mix_norm, w_in, shift_mu, decay_w0, decay_w2, aaa_a0, aaa_a2, gate_g2, k_k, k_a,
           r_k, lnx_w, lnx_b, diff_lambda, subln_w, w_out, mem_q_norm, mem_kv_norm, wq_mem, wk_mem, wv_mem,
           wo_mem, moe_norm, router_group_w, router_group_b, router_expert_w, router_expert_b, expert_gate,
           expert_up, expert_down, final_norm):
    B, S, D = x.shape
    depth = w_in.shape[0]
    outs = []
    for b in range(B):
        h = x[b]
        memb = mem[b]
        for l in range(depth):
            lambda_init = 0.8 - 0.6 * math.exp(-0.3 * l)
            w_in_b = w_in[l].astype(BF16)
            proj_r = matmul(h, w_in_b[:, :RWKV_COLS], gain=mix_norm[l], tm=1024, tn=RWKV_COLS // 2)
            proj_d = matmul(h, w_in_b[:, RWKV_COLS:], gain=mix_norm[l], tm=1024, tn=1024)
            r, k, v, lw, cs, a, g = rwkv_prep(proj_r, shift_mu[l], decay_w0[l], decay_w2[l], aaa_a0[l],
                                              aaa_a2[l], gate_g2[l])
            y_rwkv = rwkv_scan(r, k, v, lw, cs, a, g, k_k[l], k_a[l], r_k[l], lnx_w[l], lnx_b[l])
            qr, kr, vb = rope(proj_d, positions[b])
            y_diff = diff_flash(qr, kr, vb, diff_lambda[l], subln_w[l], lambda_init)
            h = matmul([y_rwkv, y_diff], w_out[l].astype(BF16), residual=h, tm=1024)
            q = matmul(h, wq_mem[l].astype(BF16), gain=mem_q_norm[l], out_dtype=BF16, tm=1024)
            km = matmul(memb, wk_mem[l].astype(BF16), gain=mem_kv_norm[l], out_dtype=BF16)
            vm = matmul(memb, wv_mem[l].astype(BF16), gain=mem_kv_norm[l], out_dtype=BF16)
            o = mem_attn(q, km, vm)
            h = matmul(o, wo_mem[l].astype(BF16), residual=h, tm=1024)
            h = moe(h, moe_norm[l], router_group_w[l], router_group_b[l], router_expert_w[l], router_expert_b[l],
                    expert_gate[l], expert_up[l], expert_down[l], final_norm if l == depth - 1 else None)
        outs.append(h)
    return jnp.stack(outs, axis=0)
```

```python
import functools
import math

import numpy as np
import jax
import jax.numpy as jnp
from jax import lax
from jax.experimental import pallas as pl
from jax.experimental.pallas import tpu as pltpu

F32 = jnp.float32
BF16 = jnp.bfloat16
HIGHEST = lax.Precision.HIGHEST

RWKV_HEADS = 16
HEAD_N = 64
RWKV_WIDTH = RWKV_HEADS * HEAD_N
DECAY_LORA = 64
AAA_LORA = 64
GATE_LORA = 128
RWKV_COLS = 3 * RWKV_WIDTH + DECAY_LORA + AAA_LORA + GATE_LORA
DIFF_HEADS = 8
DIFF_QK = 64
DIFF_V = 128
DIFF_WIDTH = DIFF_HEADS * DIFF_V
ROPE_THETA = 10000.0
MEM_HEADS = 4
N_GROUPS = 8
EXPERTS_PER_GROUP = 8
N_EXPERTS = N_GROUPS * EXPERTS_PER_GROUP
TOP_K = 2
NORM_EPS = 1e-6
LNX_EPS = 64e-5
SUBLN_EPS = 1e-5

LANES = 128
VMEM_LIMIT = 56 * 1024 * 1024

CHUNK = 64
SCAN_ROWS = 256
SCAN_WIDTH = 512
MOE_ROWS = 512
MOE_SUB = 256
MOE_SPLIT = 2
MOE_DMA_UNROLL = 8


def _params(*sem):
    return pltpu.CompilerParams(dimension_semantics=sem, vmem_limit_bytes=VMEM_LIMIT)


def _mm_kernel(*refs, n_x, has_norm, has_res):
    it = iter(refs)
    x_refs = [next(it) for _ in range(n_x)]
    g_ref = next(it) if has_norm else None
    w_ref = next(it)
    r_ref = next(it) if has_res else None
    o_ref = next(it)
    xs_ref = next(it)

    @pl.when(pl.program_id(1) == 0)
    def _():
        if has_norm:
            x = x_refs[0][...].astype(F32)
            x = x * lax.rsqrt(jnp.mean(x * x, axis=-1, keepdims=True) + NORM_EPS) * g_ref[...]
            xs_ref[...] = x.astype(BF16)
        else:
            k0 = 0
            for x_ref in x_refs:
                xs_ref[:, k0:k0 + x_ref.shape[1]] = x_ref[...].astype(BF16)
                k0 += x_ref.shape[1]

    acc = jnp.dot(xs_ref[...], w_ref[...], preferred_element_type=F32)
    if has_res:
        acc = acc + r_ref[...]
    o_ref[...] = acc.astype(o_ref.dtype)


def matmul(x, w, *, gain=None, residual=None, out_dtype=F32, tm=512, tn=512):
    xs = list(x) if isinstance(x, (list, tuple)) else [x]
    assert gain is None or len(xs) == 1
    M = xs[0].shape[0]
    K, N = w.shape
    assert sum(p.shape[1] for p in xs) == K
    tm = min(tm, M)
    assert M % tm == 0 and N % tn == 0, (M, N, tm, tn)
    in_specs = [pl.BlockSpec((tm, p.shape[1]), lambda i, j: (i, 0)) for p in xs]
    args = list(xs)
    if gain is not None:
        in_specs.append(pl.BlockSpec((1, K), lambda i, j: (0, 0)))
        args.append(gain.reshape(1, K).astype(F32))
    in_specs.append(pl.BlockSpec((K, tn), lambda i, j: (0, j)))
    args.append(w)
    if residual is not None:
        in_specs.append(pl.BlockSpec((tm, tn), lambda i, j: (i, j)))
        args.append(residual)
    return pl.pallas_call(
        functools.partial(_mm_kernel, n_x=len(xs), has_norm=gain is not None, has_res=residual is not None),
        out_shape=jax.ShapeDtypeStruct((M, N), out_dtype),
        grid=(M // tm, N // tn),
        in_specs=in_specs,
        out_specs=pl.BlockSpec((tm, tn), lambda i, j: (i, j)),
        scratch_shapes=[pltpu.VMEM((tm, K), BF16)],
        compiler_params=_params("parallel", "arbitrary"),
        name="matmul",
    )(*args)


def _sigmoid(x):
    return 1.0 / (1.0 + jnp.exp(-x))


def _rwkv_prep_kernel(p_ref, pp_ref, mu_ref, w0_ref, w2_ref, a0_ref, a2_ref, g2_ref,
                      r_o, k_o, v_o, lw_o, cs_o, a_o, g_o):
    W = RWKV_WIDTH
    p = p_ref[...]
    last = jnp.where(pl.program_id(0) == 0, 0.0, pp_ref[7:8, :])
    prev = pltpu.roll(p, 1, axis=0)
    row = lax.broadcasted_iota(jnp.int32, p.shape, 0)
    prev = jnp.where(row == 0, last, prev)
    ps = p + (prev - p) * mu_ref[...]
    r_o[...] = ps[:, 0:W]
    k_o[...] = ps[:, W:2 * W]
    v_o[...] = ps[:, 2 * W:3 * W]
    o = 3 * W
    wd = ps[:, o:o + DECAY_LORA]
    ad = ps[:, o + DECAY_LORA:o + DECAY_LORA + AAA_LORA]
    gd = ps[:, o + DECAY_LORA + AAA_LORA:o + DECAY_LORA + AAA_LORA + GATE_LORA]
    z = w0_ref[...] + jnp.dot(jnp.tanh(wd).astype(BF16), w2_ref[...], preferred_element_type=F32)
    nz = -z
    softplus = jnp.maximum(nz, 0.0) + jnp.log(1.0 + jnp.exp(-jnp.abs(nz)))
    w_raw = -softplus - 0.5
    lw = -jnp.exp(w_raw)
    lw_o[...] = lw
    tm = p.shape[0]
    rr = lax.broadcasted_iota(jnp.int32, (tm, tm), 0)
    cc = lax.broadcasted_iota(jnp.int32, (tm, tm), 1)
    sh = CHUNK.bit_length() - 1
    tri = (((rr >> sh) == (cc >> sh)) & (rr >= cc)).astype(F32)
    cs_o[...] = _dot(tri, lw, HIGHEST)
    a_o[...] = _sigmoid(a0_ref[...] + jnp.dot(ad.astype(BF16), a2_ref[...], preferred_element_type=F32))
    g_o[...] = jnp.dot(_sigmoid(gd).astype(BF16), g2_ref[...], preferred_element_type=F32)


def rwkv_prep(p, mu, w0, w2, a0, a2, g2, *, tm=256):
    S = p.shape[0]
    tm = min(tm, S)
    W = RWKV_WIDTH
    row = lambda x: x.reshape(1, -1).astype(F32)
    full = lambda a: pl.BlockSpec(a.shape, lambda i: (0, 0))
    args = [p, p, row(mu), row(w0), w2.astype(BF16), row(a0), a2.astype(BF16), g2.astype(BF16)]
    in_specs = [pl.BlockSpec((tm, RWKV_COLS), lambda i: (i, 0)),
                pl.BlockSpec((8, RWKV_COLS), lambda i: (jnp.maximum(i * (tm // 8) - 1, 0), 0))]
    in_specs += [full(a) for a in args[2:]]
    out = jax.ShapeDtypeStruct((S, W), F32)
    return pl.pallas_call(
        _rwkv_prep_kernel,
        out_shape=[out] * 7,
        grid=(S // tm,),
        in_specs=in_specs,
        out_specs=[pl.BlockSpec((tm, W), lambda i: (i, 0))] * 7,
        compiler_params=_params("parallel"),
        name="rwkv_prep",
    )(*args)


def _dot_nt(a, b, precision=None):
    return lax.dot_general(a, b, (((1,), (1,)), ((), ())), preferred_element_type=F32, precision=precision)


def _dot_tn(a, b, precision=None):
    return lax.dot_general(a, b, (((0,), (0,)), ((), ())), preferred_element_type=F32, precision=precision)


def _dot(a, b, precision=None):
    return jnp.dot(a, b, preferred_element_type=F32, precision=precision)


def _rwkv_scan_kernel(r_ref, k_ref, v_ref, lw_ref, cs_ref, a_ref, g_ref, kk_ref, ka_ref, rk_ref, lnw_ref, lnb_ref,
                      y_ref, state_ref):
    C = CHUNK
    N = HEAD_N

    @pl.when(pl.program_id(1) == 0)
    def _():
        state_ref[...] = jnp.zeros_like(state_ref)

    row = lax.broadcasted_iota(jnp.int32, (C, C), 0)
    col = lax.broadcasted_iota(jnp.int32, (C, C), 1)
    eye = (row == col).astype(F32)
    strict = row > col
    incl = row >= col

    n_chunks = r_ref.shape[0] // C
    heads = r_ref.shape[1] // N
    items = [(j, q) for j in range(heads) for q in range(n_chunks)]
    G = range(len(items))

    def tile(ref, j, q):
        return ref[q * C:(q + 1) * C, j * N:(j + 1) * N]

    def par(ref, j):
        return ref[:, j * N:(j + 1) * N]

    r = [tile(r_ref, j, q) for j, q in items]
    v = [tile(v_ref, j, q) for j, q in items]
    k2, at, bt, kt, rt, pc = [], [], [], [], [], []
    for g, (j, q) in enumerate(items):
        k = tile(k_ref, j, q)
        a = tile(a_ref, j, q)
        cs = tile(cs_ref, j, q)
        kk = k * par(kk_ref, j)
        kk = kk / jnp.maximum(jnp.sqrt(jnp.sum(kk * kk, axis=-1, keepdims=True)), 1e-12)
        k2_ = k * (1.0 + (a - 1.0) * par(ka_ref, j))
        e_neg = jnp.exp(-cs)
        k2.append(k2_)
        at.append((-kk * jnp.exp(cs - tile(lw_ref, j, q))).astype(BF16))
        bt.append((kk * a) * e_neg)
        kt.append(k2_ * e_neg)
        rt.append(r[g] * jnp.exp(cs))
        pc.append(jnp.exp(cs[C - 1:C, :]))
    vb = [x.astype(BF16) for x in v]
    btb = [x.astype(BF16) for x in bt]
    ktb = [x.astype(BF16) for x in kt]
    rtb = [x.astype(BF16) for x in rt]
    n_ab = [jnp.where(strict, _dot_nt(at[g], btb[g]), 0.0) for g in G]
    a_ak = [jnp.where(strict, _dot_nt(at[g], ktb[g]), 0.0).astype(BF16) for g in G]
    a_rb = [jnp.where(incl, _dot_nt(rtb[g], btb[g]), 0.0).astype(BF16) for g in G]
    a_rk = [jnp.where(incl, _dot_nt(rtb[g], ktb[g]), 0.0).astype(BF16) for g in G]
    akv = [_dot(a_ak[g], vb[g]) for g in G]
    t = None
    b = 1
    while b < C:
        sh = (2 * b).bit_length() - 1
        low_left = ((row >> sh) == (col >> sh)) & ((row & b) != 0) & ((col & b) == 0)
        nb = [jnp.where(low_left, n_ab[g], 0.0) for g in G]
        if b == 1:
            t = [eye + nb[g] for g in G]
        else:
            tb = [t[g].astype(BF16) for g in G]
            z = [_dot(nb[g].astype(BF16), tb[g]).astype(BF16) for g in G]
            t = [t[g] + _dot(tb[g], z[g]) for g in G]
        b *= 2
    tb = [t[g].astype(BF16) for g in G]
    wm = [_dot(tb[g], at[g]).astype(BF16) for g in G]
    u0 = [_dot(tb[g], akv[g].astype(BF16)).astype(BF16) for g in G]
    rm = [(rt[g] + _dot(a_rb[g], wm[g])).astype(BF16) for g in G]
    y0 = [_dot(a_rb[g], u0[g]) + _dot(a_rk[g], vb[g]) for g in G]
    bp = [(bt[g] * pc[g]).astype(BF16) for g in G]
    kp = [(kt[g] * pc[g]).astype(BF16) for g in G]
    mp = [_dot_tn(wm[g], bp[g]).astype(BF16) for g in G]
    s_add = [_dot_tn(u0[g], bp[g]) + _dot_tn(vb[g], kp[g]) for g in G]

    for j in range(heads):
        s = state_ref[j]
        for q in range(n_chunks):
            g = j * n_chunks + q
            sb = s.astype(BF16)
            y = _dot_nt(rm[g], sb) + y0[g]
            s = s * pc[g] + _dot(sb, mp[g]) + s_add[g]
            mean = jnp.mean(y, axis=-1, keepdims=True)
            yc = y - mean
            var = jnp.mean(yc * yc, axis=-1, keepdims=True)
            yn = yc * lax.rsqrt(var + LNX_EPS) * par(lnw_ref, j) + par(lnb_ref, j)
            bonus = jnp.sum(r[g] * k2[g] * par(rk_ref, j), axis=-1, keepdims=True) * v[g]
            y_ref[q * C:(q + 1) * C, j * N:(j + 1) * N] = (yn + bonus) * tile(g_ref, j, q)
        state_ref[j] = s


def rwkv_scan(r, k, v, lw, cs, a, g, k_k, k_a, r_k, lnx_w, lnx_b, *, rows=SCAN_ROWS, width=SCAN_WIDTH):
    S, W = r.shape
    rows = min(rows, S)
    hp = width // HEAD_N
    row = lambda x: x.reshape(1, W).astype(F32)
    seq = pl.BlockSpec((rows, width), lambda h, c: (c, h))
    par = pl.BlockSpec((1, width), lambda h, c: (0, h))
    return pl.pallas_call(
        _rwkv_scan_kernel,
        out_shape=jax.ShapeDtypeStruct((S, W), F32),
        grid=(W // width, S // rows),
        in_specs=[seq] * 7 + [par] * 5,
        out_specs=seq,
        scratch_shapes=[pltpu.VMEM((hp, HEAD_N, HEAD_N), F32)],
        compiler_params=_params("parallel", "arbitrary"),
        name="rwkv_scan",
    )(r, k, v, lw, cs, a, g, row(k_k), row(k_a), row(r_k), row(lnx_w), row(lnx_b))


def _diff_proj_kernel(x_ref, g_ref, w_ref, pos_ref, freq_ref, o_ref, xs_ref, cos_ref, sin_ref):
    j = pl.program_id(1)
    half = DIFF_QK // 2

    def first_half(shape):
        return (lax.broadcasted_iota(jnp.int32, shape, 1) & (DIFF_QK - 1)) < half

    @pl.when(j == 0)
    def _():
        x = x_ref[...]
        x = x * lax.rsqrt(jnp.mean(x * x, axis=-1, keepdims=True) + NORM_EPS) * g_ref[...]
        xs_ref[...] = x.astype(BF16)
        ang = pos_ref[...] * freq_ref[...]
        cos_ref[...] = jnp.cos(ang)
        sin_ref[...] = jnp.where(first_half(ang.shape), -1.0, 1.0) * jnp.sin(ang)

    acc = jnp.dot(xs_ref[...], w_ref[...], preferred_element_type=F32)

    @pl.when(j < 2)
    def _():
        sc = jnp.where(j == 0, DIFF_QK ** -0.5, 1.0)
        cos = cos_ref[...]
        sin = sin_ref[...]
        fh = first_half(cos.shape)
        for b in range(acc.shape[1] // LANES):
            ls = slice(b * LANES, (b + 1) * LANES)
            x = acc[:, ls]
            partner = jnp.where(fh, pltpu.roll(x, LANES - half, axis=1), pltpu.roll(x, half, axis=1))
            o_ref[:, ls] = ((x * cos + partner * sin) * sc).astype(o_ref.dtype)

    @pl.when(j == 2)
    def _():
        o_ref[...] = acc.astype(o_ref.dtype)


def diff_proj(x, gain, w, positions, *, tm=1024):
    S, K = x.shape
    W = DIFF_WIDTH
    tm = min(tm, S)
    assert w.shape == (K, 3 * W) and S % tm == 0
    inv_freq = ROPE_THETA ** (-(jnp.arange(0, DIFF_QK, 2, dtype=F32) / DIFF_QK))
    freq = jnp.tile(inv_freq, LANES // (DIFF_QK // 2)).reshape(1, LANES)
    pos = positions.reshape(S, 1).astype(F32)
    return pl.pallas_call(
        _diff_proj_kernel,
        out_shape=jax.ShapeDtypeStruct((S, 3 * W), BF16),
        grid=(S // tm, 3),
        in_specs=[pl.BlockSpec((tm, K), lambda i, j: (i, 0)),
                  pl.BlockSpec((1, K), lambda i, j: (0, 0)),
                  pl.BlockSpec((K, W), lambda i, j: (0, j)),
                  pl.BlockSpec((tm, 1), lambda i, j: (i, 0)),
                  pl.BlockSpec((1, LANES), lambda i, j: (0, 0))],
        out_specs=pl.BlockSpec((tm, W), lambda i, j: (i, j)),
        scratch_shapes=[pltpu.VMEM((tm, K), BF16), pltpu.VMEM((tm, LANES), F32), pltpu.VMEM((tm, LANES), F32)],
        compiler_params=_params("parallel", "arbitrary"),
        name="diff_proj",
    )(x, gain.reshape(1, K).astype(F32), w, pos, freq)


def _diff_flash_kernel(qi_tab, ki_tab, q_ref, k_ref, v_ref, lam_ref, sw_ref, o_ref,
                       m_ref, l_ref, acc_ref, s_ref, p_ref, alpha_ref, *, tile, row_block, diag_parts, lambda_init):
    p = pl.program_id(1)
    qi = qi_tab[p]
    ki = ki_tab[p]

    @pl.when(ki == 0)
    def _():
        m_ref[...] = jnp.full_like(m_ref, -jnp.inf)
        l_ref[...] = jnp.zeros_like(l_ref)
        acc_ref[...] = jnp.zeros_like(acc_ref)

    def process(row0, nrows, kmax, masked):
        rows = slice(row0, row0 + nrows)
        q = q_ref[rows, :]
        k = k_ref[0:kmax, :]
        v = v_ref[0:kmax, :]
        lane = lax.broadcasted_iota(jnp.int32, q.shape, 1)
        reps = kmax // LANES
        for c in range(2):
            qc = jnp.where((lane >> (DIFF_QK.bit_length() - 1)) == c, q, jnp.zeros_like(q))
            s_ref[0:nrows, 0:kmax] = _dot_nt(qc, k)
            for r0 in range(0, nrows, row_block):
                ar = slice(row0 + r0, row0 + r0 + row_block)
                lr = slice(r0, r0 + row_block)
                s = s_ref[lr, 0:kmax]
                if masked:
                    qpos = lax.broadcasted_iota(jnp.int32, s.shape, 0) + (row0 + r0)
                    kpos = lax.broadcasted_iota(jnp.int32, s.shape, 1)
                    s = jnp.where(kpos <= qpos, s, -jnp.inf)
                m_old = m_ref[c, ar, :]
                m_new = jnp.maximum(m_old, jnp.max(s, axis=-1, keepdims=True))
                alpha = jnp.exp(m_old - m_new)
                pr = jnp.exp(s - jnp.concatenate([m_new] * reps, axis=1))
                l_ref[c, ar, :] = alpha * l_ref[c, ar, :] + jnp.sum(pr, axis=-1, keepdims=True)
                m_ref[c, ar, :] = m_new
                alpha_ref[lr, :] = alpha
                p_ref[lr, 0:kmax] = pr.astype(BF16)
            acc_ref[c, rows, :] = (alpha_ref[0:nrows, :] * acc_ref[c, rows, :]
                                   + _dot(p_ref[0:nrows, 0:kmax], v))

    @pl.when(ki < qi)
    def _():
        process(0, tile, tile, False)

    @pl.when(ki == qi)
    def _():
        part = tile // diag_parts
        for a in range(diag_parts):
            process(a * part, part, (a + 1) * part, True)
        lp = lam_ref[...]
        lam = (jnp.exp(jnp.sum(lp[0:1] * lp[1:2], axis=-1, keepdims=True))
               - jnp.exp(jnp.sum(lp[2:3] * lp[3:4], axis=-1, keepdims=True)) + lambda_init)
        o = acc_ref[0] / l_ref[0] - lam * (acc_ref[1] / l_ref[1])
        o = o * lax.rsqrt(jnp.mean(o * o, axis=-1, keepdims=True) + SUBLN_EPS)
        o_ref[...] = o * sw_ref[...] * (1.0 - lambda_init)


def diff_flash(qkv, lambdas, subln_w, lambda_init, *, tile=1024, row_block=32, diag_parts=4):
    S = qkv.shape[0]
    tile = min(tile, S)
    assert S % tile == 0 and (tile // diag_parts) % LANES == 0 and (tile // diag_parts) % row_block == 0
    pairs = [(qi, ki) for qi in range(S // tile) for ki in range(qi + 1)]
    qi_tab = jnp.asarray([p[0] for p in pairs], jnp.int32)
    ki_tab = jnp.asarray([p[1] for p in pairs], jnp.int32)
    grid_spec = pltpu.PrefetchScalarGridSpec(
        num_scalar_prefetch=2,
        grid=(DIFF_HEADS, len(pairs)),
        in_specs=[pl.BlockSpec((tile, LANES), lambda h, p, qt, kt: (qt[p], h)),
                  pl.BlockSpec((tile, LANES), lambda h, p, qt, kt: (kt[p], DIFF_HEADS + h)),
                  pl.BlockSpec((tile, LANES), lambda h, p, qt, kt: (kt[p], 2 * DIFF_HEADS + h)),
                  pl.BlockSpec((4, DIFF_QK), lambda h, p, qt, kt: (0, 0)),
                  pl.BlockSpec((1, DIFF_V), lambda h, p, qt, kt: (0, 0))],
        out_specs=pl.BlockSpec((tile, LANES), lambda h, p, qt, kt: (qt[p], h)),
        scratch_shapes=[pltpu.VMEM((2, tile, LANES), F32),
                        pltpu.VMEM((2, tile, LANES), F32),
                        pltpu.VMEM((2, tile, DIFF_V), F32),
                        pltpu.VMEM((tile, tile), F32),
                        pltpu.VMEM((tile, tile), BF16),
                        pltpu.VMEM((tile, LANES), F32)],
    )
    return pl.pallas_call(
        functools.partial(_diff_flash_kernel, tile=tile, row_block=row_block, diag_parts=diag_parts,
                          lambda_init=lambda_init),
        out_shape=jax.ShapeDtypeStruct((S, DIFF_WIDTH), F32),
        grid_spec=grid_spec,
        compiler_params=_params("parallel", "arbitrary"),
        name="diff_flash",
    )(qi_tab, ki_tab, qkv, qkv, qkv, lambdas.astype(F32), subln_w.reshape(1, DIFF_V).astype(F32))


def _mem_attn_kernel(x_ref, g_ref, w_ref, k_ref, v_ref, o_ref, xs_ref):
    @pl.when(pl.program_id(1) == 0)
    def _():
        x = x_ref[...]
        x = x * lax.rsqrt(jnp.mean(x * x, axis=-1, keepdims=True) + NORM_EPS) * g_ref[...]
        xs_ref[...] = x.astype(BF16)

    q = jnp.dot(xs_ref[...], w_ref[...], preferred_element_type=F32).astype(BF16)
    s = _dot_nt(q, k_ref[...]) * (q.shape[1] ** -0.5)
    s = s - jnp.max(s, axis=-1, keepdims=True)
    e = jnp.exp(s)
    pr = e / jnp.sum(e, axis=-1, keepdims=True)
    o_ref[...] = _dot(pr.astype(BF16), v_ref[...]).astype(o_ref.dtype)


def mem_attn(x, gain, wq, k, v, *, tm=1024):
    S, D = x.shape
    M = k.shape[0]
    hd = D // MEM_HEADS
    tm = min(tm, S)
    return pl.pallas_call(
        _mem_attn_kernel,
        out_shape=jax.ShapeDtypeStruct((S, D), BF16),
        grid=(S // tm, MEM_HEADS),
        in_specs=[pl.BlockSpec((tm, D), lambda i, j: (i, 0)),
                  pl.BlockSpec((1, D), lambda i, j: (0, 0)),
                  pl.BlockSpec((D, hd), lambda i, j: (0, j)),
                  pl.BlockSpec((M, hd), lambda i, j: (0, j)),
                  pl.BlockSpec((M, hd), lambda i, j: (0, j))],
        out_specs=pl.BlockSpec((tm, hd), lambda i, j: (i, j)),
        scratch_shapes=[pltpu.VMEM((tm, D), BF16)],
        compiler_params=_params("parallel", "arbitrary"),
        name="mem_attn",
    )(x, gain.reshape(1, D).astype(F32), wq, k, v)


def _first_argmax(x, lane, big):
    m = jnp.max(x, axis=-1, keepdims=True)
    idx = jnp.min(jnp.where(x == m, lane, big), axis=-1, keepdims=True)
    return m, idx


def _router_kernel(h_ref, gain_ref, wg_ref, bg_ref, we_ref, be_ref, hn_ref, eid_ref, ew_ref, cnt_ref):
    @pl.when(pl.program_id(0) == 0)
    def _():
        cnt_ref[...] = jnp.zeros_like(cnt_ref)

    x = h_ref[...]
    hn = x * lax.rsqrt(jnp.mean(x * x, axis=-1, keepdims=True) + NORM_EPS) * gain_ref[...]
    hn_ref[...] = hn
    hn_hi = hn.astype(BF16)
    hn_lo = (hn - hn_hi.astype(F32)).astype(BF16)

    def logits(w_ref, b_ref):
        w = w_ref[...]
        w_hi = w.astype(BF16)
        w_lo = (w - w_hi.astype(F32)).astype(BF16)
        return _dot(hn_hi, w_hi) + (_dot(hn_hi, w_lo) + _dot(hn_lo, w_hi)) + b_ref[...]
    g_logits = logits(wg_ref, bg_ref)
    e_logits = logits(we_ref, be_ref)
    tm = x.shape[0]
    lane_g = lax.broadcasted_iota(jnp.int32, (tm, N_GROUPS), 1)
    g_max, g_idx = _first_argmax(g_logits, lane_g, N_GROUPS)
    g_w = 1.0 / jnp.sum(jnp.exp(g_logits - g_max), axis=-1, keepdims=True)
    lane_e = lax.broadcasted_iota(jnp.int32, (tm, N_EXPERTS), 1)
    in_group = (lane_e // EXPERTS_PER_GROUP) == g_idx
    el = jnp.where(in_group, e_logits, -jnp.inf)
    e_max = jnp.max(el, axis=-1, keepdims=True)
    ex = jnp.exp(el - e_max)
    prob = ex / jnp.sum(ex, axis=-1, keepdims=True)
    prob = jnp.where(in_group, prob, -1.0)
    p1, i1 = _first_argmax(prob, lane_e, N_EXPERTS)
    p2, i2 = _first_argmax(jnp.where(lane_e == i1, -1.0, prob), lane_e, N_EXPERTS)
    tot = p1 + p2
    lane_o = lax.broadcasted_iota(jnp.int32, (tm, LANES), 1)
    eid_ref[...] = jnp.where(lane_o == 0, i1, jnp.where(lane_o == 1, i2, 0))
    ew_ref[...] = jnp.where(lane_o == 0, g_w * (p1 / tot), jnp.where(lane_o == 1, g_w * (p2 / tot), 0.0))
    chosen = ((lane_o == i1) | (lane_o == i2)).astype(F32)
    cnt_ref[...] = cnt_ref[...] + jnp.sum(chosen, axis=0, keepdims=True).astype(jnp.int32)


def router(h, gain, wg, bg, we, be, *, tm=512):
    S, D = h.shape
    tm = min(tm, S)
    full = lambda a: pl.BlockSpec(a.shape, lambda i: (0, 0))
    args = [h, gain.reshape(1, D).astype(F32), wg.astype(F32), bg.reshape(1, -1).astype(F32),
            we.astype(F32), be.reshape(1, -1).astype(F32)]
    return pl.pallas_call(
        _router_kernel,
        out_shape=[jax.ShapeDtypeStruct((S, D), F32),
                   jax.ShapeDtypeStruct((S, LANES), jnp.int32),
                   jax.ShapeDtypeStruct((S, LANES), F32),
                   jax.ShapeDtypeStruct((1, LANES), jnp.int32)],
        grid=(S // tm,),
        in_specs=[pl.BlockSpec((tm, D), lambda i: (i, 0))] + [full(a) for a in args[1:]],
        out_specs=[pl.BlockSpec((tm, D), lambda i: (i, 0)),
                   pl.BlockSpec((tm, LANES), lambda i: (i, 0)),
                   pl.BlockSpec((tm, LANES), lambda i: (i, 0)),
                   pl.BlockSpec((1, LANES), lambda i: (0, 0))],
        compiler_params=_params("arbitrary"),
        name="router",
    )(*args)


def _plan_kernel(cnt_ref, blk_e_ref, blk_n_ref, start_ref):
    nb = blk_e_ref.shape[0]
    sh = MOE_ROWS.bit_length() - 1

    def per_expert(e, b):
        n = cnt_ref[e]
        start_ref[e] = b

        def per_block(t, _):
            blk_e_ref[b + t] = e
            blk_n_ref[b + t] = jnp.minimum(n - t * MOE_ROWS, MOE_ROWS)
            return 0
        k = (n + (MOE_ROWS - 1)) >> sh
        lax.fori_loop(0, k, per_block, 0)
        return b + k
    total = lax.fori_loop(0, N_EXPERTS, per_expert, 0)

    def rest(e, _):
        start_ref[e] = total
        return 0
    lax.fori_loop(N_EXPERTS, start_ref.shape[0], rest, 0)
    last_e = blk_e_ref[jnp.maximum(total - 1, 0)]

    def tail(b, _):
        blk_e_ref[b] = last_e
        blk_n_ref[b] = 0
        return 0
    lax.fori_loop(total, nb, tail, 0)


def block_plan(counts, n_assign):
    nb = (n_assign + N_EXPERTS * (MOE_ROWS - 1)) // MOE_ROWS
    smem = pl.BlockSpec(memory_space=pltpu.SMEM)
    return pl.pallas_call(
        _plan_kernel,
        out_shape=[jax.ShapeDtypeStruct((nb,), jnp.int32), jax.ShapeDtypeStruct((nb,), jnp.int32),
                   jax.ShapeDtypeStruct((LANES,), jnp.int32)],
        in_specs=[smem],
        out_specs=[smem, smem, smem],
        name="block_plan",
    )(counts.reshape(LANES))


def _dest_kernel(eid_ref, start_ref, dest_ref, carry_ref):
    @pl.when(pl.program_id(0) == 0)
    def _():
        carry_ref[...] = jnp.zeros_like(carry_ref)

    eid = eid_ref[...]
    tm = eid.shape[0]
    lane = lax.broadcasted_iota(jnp.int32, (tm, LANES), 1)
    oh0 = lane == eid[:, 0:1]
    oh1 = lane == eid[:, 1:2]
    both = jnp.where(oh0 | oh1, 1.0, 0.0).astype(BF16)
    r = lax.broadcasted_iota(jnp.int32, (tm, tm), 0)
    c = lax.broadcasted_iota(jnp.int32, (tm, tm), 1)
    before = _dot(jnp.where(r > c, 1.0, 0.0).astype(BF16), both) + carry_ref[...]
    pos = before + (start_ref[...] * MOE_ROWS).astype(F32)
    d0 = jnp.sum(jnp.where(oh0, pos, 0.0), axis=-1, keepdims=True).astype(jnp.int32)
    d1 = jnp.sum(jnp.where(oh1, pos, 0.0), axis=-1, keepdims=True).astype(jnp.int32)
    dest_ref[...] = jnp.where(lane == 0, d0, jnp.where(lane == 1, d1, 0))
    carry_ref[...] = carry_ref[...] + jnp.sum(both.astype(F32), axis=0, keepdims=True)


def assignment_dest(eid, blk_start, *, tm=512):
    S = eid.shape[0]
    tm = min(tm, S)
    return pl.pallas_call(
        _dest_kernel,
        out_shape=jax.ShapeDtypeStruct((S, LANES), jnp.int32),
        grid=(S // tm,),
        in_specs=[pl.BlockSpec((tm, LANES), lambda i: (i, 0)), pl.BlockSpec((1, LANES), lambda i: (0, 0))],
        out_specs=pl.BlockSpec((tm, LANES), lambda i: (i, 0)),
        scratch_shapes=[pltpu.VMEM((1, LANES), F32)],
        compiler_params=_params("arbitrary"),
        name="assignment_dest",
    )(eid, blk_start.reshape(1, LANES))


def _for_rows(n, fn):
    sh = MOE_DMA_UNROLL.bit_length() - 1

    def group(t, _):
        for u in range(MOE_DMA_UNROLL):
            fn(t * MOE_DMA_UNROLL + u)
        return 0
    lax.fori_loop(0, n >> sh, group, 0)

    def one(r, _):
        fn(r)
        return 0
    lax.fori_loop((n >> sh) << sh, n, one, 0)


def _experts_kernel(blk_e, blk_n, dest, hn_hbm, wg_ref, wu_ref, wd_ref, out_hbm,
                    xg_ref, xb_ref, acc_ref, wgb_ref, wub_ref, wdb_ref, inv_ref, gsem, ssem):
    i = pl.program_id(0)
    j = pl.program_id(1)
    nb = pl.num_programs(0)
    nsplit = pl.num_programs(1)
    n_tok = hn_hbm.shape[0]
    sub_shift = MOE_SUB.bit_length() - 1
    k_shift = TOP_K.bit_length() - 1

    def n_sub_of(b):
        return (blk_n[b] + (MOE_SUB - 1)) >> sub_shift

    def gather(b, r):
        tok = inv_ref[b * MOE_ROWS + r] >> k_shift
        return pltpu.make_async_copy(hn_hbm.at[pl.ds(tok, 1)], xg_ref.at[pl.ds(r, 1)], gsem)

    def scatter(b, r):
        a = inv_ref[b * MOE_ROWS + r]
        row = (a & (TOP_K - 1)) * n_tok + (a >> k_shift)
        return pltpu.make_async_copy(acc_ref.at[b & 1, pl.ds(r, 1)], out_hbm.at[pl.ds(row, 1)], ssem)

    def start_gather(b):
        def pad(r, _):
            inv_ref[b * MOE_ROWS + r] = 0
            return 0
        lax.fori_loop(blk_n[b], n_sub_of(b) << sub_shift, pad, 0)
        _for_rows(n_sub_of(b) << sub_shift, lambda r: gather(b, r).start(priority=1))

    def wait_gather(b):
        n = pl.multiple_of(n_sub_of(b) << sub_shift, MOE_SUB)

        @pl.when(n > 0)
        def _():
            pltpu.make_async_copy(hn_hbm.at[pl.ds(0, n)], xg_ref.at[pl.ds(0, n)], gsem).wait()

    def start_scatter(b):
        _for_rows(blk_n[b], lambda r: scatter(b, r).start(priority=1))

    def wait_scatter(b):
        n8 = pl.multiple_of((blk_n[b] >> 3) << 3, 8)

        @pl.when(n8 > 0)
        def _():
            pltpu.make_async_copy(acc_ref.at[b & 1, pl.ds(0, n8)], out_hbm.at[pl.ds(0, n8)], ssem).wait()

        def one(r, _):
            scatter(b, r).wait()
            return 0
        lax.fori_loop(n8, blk_n[b], one, 0)

    n_sub = n_sub_of(i)
    slot = i & 1

    @pl.when(j == 0)
    def _():
        @pl.when(i == 0)
        def _():
            def invert(a):
                inv_ref[dest[a]] = a
            _for_rows(dest.shape[0], invert)
            start_gather(0)
        wait_gather(i)

        def cast(sb, _):
            rs = pl.ds(pl.multiple_of(sb * MOE_SUB, MOE_SUB), MOE_SUB)
            xb_ref[rs, :] = xg_ref[rs, :].astype(BF16)
            return 0
        lax.fori_loop(0, n_sub, cast, 0)

        @pl.when(i + 1 < nb)
        def _():
            start_gather(i + 1)

    @pl.when(n_sub > 0)
    def _():
        wgb_ref[...] = wg_ref[0].astype(BF16)
        wub_ref[...] = wu_ref[0].astype(BF16)
        wdb_ref[...] = wd_ref[0].astype(BF16)

    def sub_block(sb, _):
        rs = pl.ds(pl.multiple_of(sb * MOE_SUB, MOE_SUB), MOE_SUB)
        xb = xb_ref[rs, :]
        gate = _dot(xb, wgb_ref[...])
        hid = (gate * _sigmoid(gate)) * _dot(xb, wub_ref[...])
        part = _dot(hid.astype(BF16), wdb_ref[...])

        @pl.when(j == 0)
        def _():
            acc_ref[slot, rs, :] = part

        @pl.when(j > 0)
        def _():
            acc_ref[slot, rs, :] = acc_ref[slot, rs, :] + part
        return 0
    lax.fori_loop(0, n_sub, sub_block, 0)

    @pl.when(j == nsplit - 1)
    def _():
        @pl.when(i > 0)
        def _():
            wait_scatter(i - 1)
        start_scatter(i)

        @pl.when(i == nb - 1)
        def _():
            wait_scatter(i)


def experts(hn, blk_e, blk_n, dest, w_gate, w_up, w_down):
    S, D = hn.shape
    E, _, DE = w_gate.shape
    nb = blk_e.shape[0]
    de = DE // MOE_SPLIT
    split = lambda i, j, bn: jnp.where(bn[i] > 0, j, MOE_SPLIT - 1)
    grid_spec = pltpu.PrefetchScalarGridSpec(
        num_scalar_prefetch=3,
        grid=(nb, MOE_SPLIT),
        in_specs=[pl.BlockSpec(memory_space=pl.ANY),
                  pl.BlockSpec((1, D, de), lambda i, j, be, bn, ds: (be[i], 0, split(i, j, bn))),
                  pl.BlockSpec((1, D, de), lambda i, j, be, bn, ds: (be[i], 0, split(i, j, bn))),
                  pl.BlockSpec((1, de, D), lambda i, j, be, bn, ds: (be[i], split(i, j, bn), 0))],
        out_specs=pl.BlockSpec(memory_space=pl.ANY),
        scratch_shapes=[pltpu.VMEM((MOE_ROWS, D), F32),
                        pltpu.VMEM((MOE_ROWS, D), BF16),
                        pltpu.VMEM((2, MOE_ROWS, D), F32),
                        pltpu.VMEM((D, de), BF16),
                        pltpu.VMEM((D, de), BF16),
                        pltpu.VMEM((de, D), BF16),
                        pltpu.SMEM((nb * MOE_ROWS,), jnp.int32),
                        pltpu.SemaphoreType.DMA(()),
                        pltpu.SemaphoreType.DMA(())],
    )
    return pl.pallas_call(
        _experts_kernel,
        out_shape=jax.ShapeDtypeStruct((TOP_K * S, D), F32),
        grid_spec=grid_spec,
        compiler_params=_params("arbitrary", "arbitrary"),
        name="experts",
    )(blk_e, blk_n, dest, hn, w_gate, w_up, w_down)


def _combine_kernel(*refs, has_norm):
    h_ref, y0_ref, y1_ref, w_ref = refs[:4]
    o_ref = refs[-1]
    w = w_ref[...]
    x = h_ref[...] + (y0_ref[0] * w[:, 0:1] + y1_ref[0] * w[:, 1:2])
    if has_norm:
        x = x * lax.rsqrt(jnp.mean(x * x, axis=-1, keepdims=True) + NORM_EPS) * refs[4][...]
    o_ref[...] = x


def combine(h, y_slots, slot_w, gain=None, *, tm=512):
    S, D = h.shape
    tm = min(tm, S)
    y3 = y_slots.reshape(TOP_K, S, D)
    in_specs = [pl.BlockSpec((tm, D), lambda i: (i, 0)),
                pl.BlockSpec((1, tm, D), lambda i: (0, i, 0)),
                pl.BlockSpec((1, tm, D), lambda i: (1, i, 0)),
                pl.BlockSpec((tm, LANES), lambda i: (i, 0))]
    args = [h, y3, y3, slot_w]
    if gain is not None:
        in_specs.append(pl.BlockSpec((1, D), lambda i: (0, 0)))
        args.append(gain.reshape(1, D).astype(F32))
    return pl.pallas_call(
        functools.partial(_combine_kernel, has_norm=gain is not None),
        out_shape=jax.ShapeDtypeStruct((S, D), F32),
        grid=(S // tm,),
        in_specs=in_specs,
        out_specs=pl.BlockSpec((tm, D), lambda i: (i, 0)),
        compiler_params=_params("parallel"),
        name="combine",
    )(*args)


def moe(h, gain, wg, bg, we, be, w_gate, w_up, w_down, final_gain=None):
    S = h.shape[0]
    hn, eid, slot_w, counts = router(h, gain, wg, bg, we, be)
    blk_e, blk_n, blk_start = block_plan(counts, S * TOP_K)
    dest = assignment_dest(eid, blk_start)[:, :TOP_K].reshape(S * TOP_K)
    y_slots = experts(hn, blk_e, blk_n, dest, w_gate, w_up, w_down)
    return combine(h, y_slots, slot_w, final_gain)


def kernel(x, mem, positions, mix_norm, w_in, shift_mu, decay_w0, decay_w2, aaa_a0, aaa_a2, gate_g2, k_k, k_a,
           r_k, lnx_w, lnx_b, diff_lambda, subln_w, w_out, mem_q_norm, mem_kv_norm, wq_mem, wk_mem, wv_mem,
           wo_mem, moe_norm, router_group_w, router_group_b, router_expert_w, router_expert_b, expert_gate,
           expert_up, expert_down, final_norm):
    B, S, D = x.shape
    depth = w_in.shape[0]
    outs = []
    for b in range(B):
        h = x[b]
        memb = mem[b]
        for l in range(depth):
            lambda_init = 0.8 - 0.6 * math.exp(-0.3 * l)
            w_in_b = w_in[l].astype(BF16)
            proj_r = matmul(h, w_in_b[:, :RWKV_COLS], gain=mix_norm[l], tm=1024, tn=RWKV_COLS // 2)
            qkv = diff_proj(h, mix_norm[l], w_in_b[:, RWKV_COLS:], positions[b])
            r, k, v, lw, cs, a, g = rwkv_prep(proj_r, shift_mu[l], decay_w0[l], decay_w2[l], aaa_a0[l],
                                              aaa_a2[l], gate_g2[l])
            y_rwkv = rwkv_scan(r, k, v, lw, cs, a, g, k_k[l], k_a[l], r_k[l], lnx_w[l], lnx_b[l])
            y_diff = diff_flash(qkv, diff_lambda[l], subln_w[l], lambda_init)
            h = matmul([y_rwkv, y_diff], w_out[l].astype(BF16), residual=h, tm=1024)
            km = matmul(memb, wk_mem[l].astype(BF16), gain=mem_kv_norm[l], out_dtype=BF16)
            vm = matmul(memb, wv_mem[l].astype(BF16), gain=mem_kv_norm[l], out_dtype=BF16)
            o = mem_attn(h, mem_q_norm[l], wq_mem[l].astype(BF16), km, vm)
            h = matmul(o, wo_mem[l].astype(BF16), residual=h, tm=1024)
            h = moe(h, moe_norm[l], router_group_w[l], router_group_b[l], router_expert_w[l], router_expert_b[l],
                    expert_gate[l], expert_up[l], expert_down[l], final_norm if l == depth - 1 else None)
        outs.append(h)
    return jnp.stack(outs, axis=0)
```

```python
import functools
import math

import numpy as np
import jax
import jax.numpy as jnp
from jax import lax
from jax.experimental import pallas as pl
from jax.experimental.pallas import tpu as pltpu

F32 = jnp.float32
BF16 = jnp.bfloat16
HIGHEST = lax.Precision.HIGHEST

RWKV_HEADS = 16
HEAD_N = 64
RWKV_WIDTH = RWKV_HEADS * HEAD_N
DECAY_LORA = 64
AAA_LORA = 64
GATE_LORA = 128
RWKV_COLS = 3 * RWKV_WIDTH + DECAY_LORA + AAA_LORA + GATE_LORA
DIFF_HEADS = 8
DIFF_QK = 64
DIFF_V = 128
DIFF_WIDTH = DIFF_HEADS * DIFF_V
ROPE_THETA = 10000.0
MEM_HEADS = 4
N_GROUPS = 8
EXPERTS_PER_GROUP = 8
N_EXPERTS = N_GROUPS * EXPERTS_PER_GROUP
TOP_K = 2
NORM_EPS = 1e-6
LNX_EPS = 64e-5
SUBLN_EPS = 1e-5

LANES = 128
VMEM_LIMIT = 56 * 1024 * 1024

CHUNK = 64
SCAN_ROWS = 256
SCAN_WIDTH = 512
MOE_ROWS = 512
MOE_SUB = 256
MOE_SPLIT = 2
MOE_DMA_UNROLL = 8


def _params(*sem):
    return pltpu.CompilerParams(dimension_semantics=sem, vmem_limit_bytes=VMEM_LIMIT)


def _mm_kernel(*refs, n_x, has_norm, has_res):
    it = iter(refs)
    x_refs = [next(it) for _ in range(n_x)]
    g_ref = next(it) if has_norm else None
    w_ref = next(it)
    r_ref = next(it) if has_res else None
    o_ref = next(it)
    xs_ref = next(it)

    @pl.when(pl.program_id(1) == 0)
    def _():
        if has_norm:
            x = x_refs[0][...].astype(F32)
            x = x * lax.rsqrt(jnp.mean(x * x, axis=-1, keepdims=True) + NORM_EPS) * g_ref[...]
            xs_ref[...] = x.astype(BF16)
        else:
            k0 = 0
            for x_ref in x_refs:
                xs_ref[:, k0:k0 + x_ref.shape[1]] = x_ref[...].astype(BF16)
                k0 += x_ref.shape[1]

    acc = jnp.dot(xs_ref[...], w_ref[...], preferred_element_type=F32)
    if has_res:
        acc = acc + r_ref[...]
    o_ref[...] = acc.astype(o_ref.dtype)


def matmul(x, w, *, gain=None, residual=None, out_dtype=F32, tm=512, tn=512):
    xs = list(x) if isinstance(x, (list, tuple)) else [x]
    assert gain is None or len(xs) == 1
    M = xs[0].shape[0]
    K, N = w.shape
    assert sum(p.shape[1] for p in xs) == K
    tm = min(tm, M)
    assert M % tm == 0 and N % tn == 0, (M, N, tm, tn)
    in_specs = [pl.BlockSpec((tm, p.shape[1]), lambda i, j: (i, 0)) for p in xs]
    args = list(xs)
    if gain is not None:
        in_specs.append(pl.BlockSpec((1, K), lambda i, j: (0, 0)))
        args.append(gain.reshape(1, K).astype(F32))
    in_specs.append(pl.BlockSpec((K, tn), lambda i, j: (0, j)))
    args.append(w)
    if residual is not None:
        in_specs.append(pl.BlockSpec((tm, tn), lambda i, j: (i, j)))
        args.append(residual)
    return pl.pallas_call(
        functools.partial(_mm_kernel, n_x=len(xs), has_norm=gain is not None, has_res=residual is not None),
        out_shape=jax.ShapeDtypeStruct((M, N), out_dtype),
        grid=(M // tm, N // tn),
        in_specs=in_specs,
        out_specs=pl.BlockSpec((tm, tn), lambda i, j: (i, j)),
        scratch_shapes=[pltpu.VMEM((tm, K), BF16)],
        compiler_params=_params("parallel", "arbitrary"),
        name="matmul",
    )(*args)


def _sigmoid(x):
    return 1.0 / (1.0 + jnp.exp(-x))


def _rwkv_prep_kernel(p_ref, pp_ref, mu_ref, w0_ref, w2_ref, a0_ref, a2_ref, g2_ref, kk_ref, ka_ref, rk_ref,
                      hsum_ref, at_o, bt_o, kt_o, rt_o, v_o, g_o, bonus_o, pc_o):
    W = RWKV_WIDTH
    C = CHUNK
    p = p_ref[...]
    last = jnp.where(pl.program_id(0) == 0, 0.0, pp_ref[7:8, :])
    prev = pltpu.roll(p, 1, axis=0)
    row = lax.broadcasted_iota(jnp.int32, p.shape, 0)
    prev = jnp.where(row == 0, last, prev)
    ps = p + (prev - p) * mu_ref[...]
    r = ps[:, 0:W]
    k = ps[:, W:2 * W]
    v = ps[:, 2 * W:3 * W]
    o = 3 * W
    wd = ps[:, o:o + DECAY_LORA]
    ad = ps[:, o + DECAY_LORA:o + DECAY_LORA + AAA_LORA]
    gd = ps[:, o + DECAY_LORA + AAA_LORA:o + DECAY_LORA + AAA_LORA + GATE_LORA]
    z = w0_ref[...] + jnp.dot(jnp.tanh(wd).astype(BF16), w2_ref[...], preferred_element_type=F32)
    nz = -z
    softplus = jnp.maximum(nz, 0.0) + jnp.log(1.0 + jnp.exp(-jnp.abs(nz)))
    w_raw = -softplus - 0.5
    lw = -jnp.exp(w_raw)
    tm = p.shape[0]
    rr = lax.broadcasted_iota(jnp.int32, (tm, tm), 0)
    cc = lax.broadcasted_iota(jnp.int32, (tm, tm), 1)
    sh = C.bit_length() - 1
    tri = (((rr >> sh) == (cc >> sh)) & (rr >= cc)).astype(F32)
    cs = _dot(tri, lw, HIGHEST)
    a = _sigmoid(a0_ref[...] + jnp.dot(ad.astype(BF16), a2_ref[...], preferred_element_type=F32))
    g_o[...] = jnp.dot(_sigmoid(gd).astype(BF16), g2_ref[...], preferred_element_type=F32)

    def head_sum(x):
        hi = x.astype(BF16)
        lo = (x - hi.astype(F32)).astype(BF16)
        return _dot(hi, hsum_ref[...]) + _dot(lo, hsum_ref[...])

    kk = k * kk_ref[...]
    kk = kk / jnp.maximum(jnp.sqrt(head_sum(kk * kk)), 1e-12)
    k2 = k * (1.0 + (a - 1.0) * ka_ref[...])
    e_neg = jnp.exp(-cs)
    at_o[...] = (-kk * jnp.exp(cs - lw)).astype(at_o.dtype)
    bt_o[...] = (kk * a) * e_neg
    kt_o[...] = k2 * e_neg
    rt_o[...] = r * jnp.exp(cs)
    v_o[...] = v
    bonus_o[...] = head_sum(r * k2 * rk_ref[...]) * v
    for q in range(tm // C):
        pc_o[q] = jnp.exp(cs[q * C + C - 1:q * C + C, :])


def rwkv_prep(p, mu, w0, w2, a0, a2, g2, k_k, k_a, r_k, *, tm=256):
    S = p.shape[0]
    tm = min(tm, S)
    assert tm % CHUNK == 0
    W = RWKV_WIDTH
    row = lambda x: x.reshape(1, -1).astype(F32)
    full = lambda a: pl.BlockSpec(a.shape, lambda i: (0, 0))
    head = jnp.arange(W, dtype=jnp.int32) // HEAD_N
    hsum = (head[:, None] == head[None, :]).astype(BF16)
    args = [p, p, row(mu), row(w0), w2.astype(BF16), row(a0), a2.astype(BF16), g2.astype(BF16),
            row(k_k), row(k_a), row(r_k), hsum]
    in_specs = [pl.BlockSpec((tm, RWKV_COLS), lambda i: (i, 0)),
                pl.BlockSpec((8, RWKV_COLS), lambda i: (jnp.maximum(i * (tm // 8) - 1, 0), 0))]
    in_specs += [full(a) for a in args[2:]]
    f32 = jax.ShapeDtypeStruct((S, W), F32)
    blk = pl.BlockSpec((tm, W), lambda i: (i, 0))
    return pl.pallas_call(
        _rwkv_prep_kernel,
        out_shape=[jax.ShapeDtypeStruct((S, W), BF16)] + [f32] * 6 + [jax.ShapeDtypeStruct((S // CHUNK, 1, W), F32)],
        grid=(S // tm,),
        in_specs=in_specs,
        out_specs=[blk] * 7 + [pl.BlockSpec((tm // CHUNK, 1, W), lambda i: (i, 0, 0))],
        compiler_params=_params("parallel"),
        name="rwkv_prep",
    )(*args)


def _dot_nt(a, b, precision=None):
    return lax.dot_general(a, b, (((1,), (1,)), ((), ())), preferred_element_type=F32, precision=precision)


def _dot_tn(a, b, precision=None):
    return lax.dot_general(a, b, (((0,), (0,)), ((), ())), preferred_element_type=F32, precision=precision)


def _dot(a, b, precision=None):
    return jnp.dot(a, b, preferred_element_type=F32, precision=precision)


def _rwkv_scan_kernel(at_ref, bt_ref, kt_ref, rt_ref, v_ref, g_ref, bonus_ref, pc_ref, lnw_ref, lnb_ref,
                      y_ref, state_ref):
    C = CHUNK
    N = HEAD_N

    @pl.when(pl.program_id(1) == 0)
    def _():
        state_ref[...] = jnp.zeros_like(state_ref)

    row = lax.broadcasted_iota(jnp.int32, (C, C), 0)
    col = lax.broadcasted_iota(jnp.int32, (C, C), 1)
    eye = (row == col).astype(F32)
    strict = row > col
    incl = row >= col

    n_chunks = rt_ref.shape[0] // C
    heads = rt_ref.shape[1] // N
    items = [(j, q) for j in range(heads) for q in range(n_chunks)]
    G = range(len(items))

    def tile(ref, j, q):
        return ref[q * C:(q + 1) * C, j * N:(j + 1) * N]

    def par(ref, j):
        return ref[:, j * N:(j + 1) * N]

    at = [tile(at_ref, j, q) for j, q in items]
    bt = [tile(bt_ref, j, q) for j, q in items]
    kt = [tile(kt_ref, j, q) for j, q in items]
    rt = [tile(rt_ref, j, q) for j, q in items]
    v = [tile(v_ref, j, q) for j, q in items]
    pc = [pc_ref[q, :, j * N:(j + 1) * N] for j, q in items]
    vb = [x.astype(BF16) for x in v]
    btb = [x.astype(BF16) for x in bt]
    ktb = [x.astype(BF16) for x in kt]
    rtb = [x.astype(BF16) for x in rt]
    n_ab = [jnp.where(strict, _dot_nt(at[g], btb[g]), 0.0) for g in G]
    a_ak = [jnp.where(strict, _dot_nt(at[g], ktb[g]), 0.0).astype(BF16) for g in G]
    a_rb = [jnp.where(incl, _dot_nt(rtb[g], btb[g]), 0.0).astype(BF16) for g in G]
    a_rk = [jnp.where(incl, _dot_nt(rtb[g], ktb[g]), 0.0).astype(BF16) for g in G]
    akv = [_dot(a_ak[g], vb[g]) for g in G]
    t = None
    b = 1
    while b < C:
        sh = (2 * b).bit_length() - 1
        low_left = ((row >> sh) == (col >> sh)) & ((row & b) != 0) & ((col & b) == 0)
        nb = [jnp.where(low_left, n_ab[g], 0.0) for g in G]
        if b == 1:
            t = [eye + nb[g] for g in G]
        else:
            tb = [t[g].astype(BF16) for g in G]
            z = [_dot(nb[g].astype(BF16), tb[g]).astype(BF16) for g in G]
            t = [t[g] + _dot(tb[g], z[g]) for g in G]
        b *= 2
    tb = [t[g].astype(BF16) for g in G]
    wm = [_dot(tb[g], at[g]).astype(BF16) for g in G]
    u0 = [_dot(tb[g], akv[g].astype(BF16)).astype(BF16) for g in G]
    rm = [(rt[g] + _dot(a_rb[g], wm[g])).astype(BF16) for g in G]
    y0 = [_dot(a_rb[g], u0[g]) + _dot(a_rk[g], vb[g]) for g in G]
    bp = [(bt[g] * pc[g]).astype(BF16) for g in G]
    kp = [(kt[g] * pc[g]).astype(BF16) for g in G]
    mp = [_dot_tn(wm[g], bp[g]).astype(BF16) for g in G]
    s_add = [_dot_tn(u0[g], bp[g]) + _dot_tn(vb[g], kp[g]) for g in G]

    for j in range(heads):
        s = state_ref[j]
        for q in range(n_chunks):
            g = j * n_chunks + q
            sb = s.astype(BF16)
            y = _dot_nt(rm[g], sb) + y0[g]
            s = s * pc[g] + _dot(sb, mp[g]) + s_add[g]
            mean = jnp.mean(y, axis=-1, keepdims=True)
            yc = y - mean
            var = jnp.mean(yc * yc, axis=-1, keepdims=True)
            yn = yc * lax.rsqrt(var + LNX_EPS) * par(lnw_ref, j) + par(lnb_ref, j)
            y_ref[q * C:(q + 1) * C, j * N:(j + 1) * N] = (yn + tile(bonus_ref, j, q)) * tile(g_ref, j, q)
        state_ref[j] = s


def rwkv_scan(at, bt, kt, rt, v, g, bonus, pc, lnx_w, lnx_b, *, rows=SCAN_ROWS, width=SCAN_WIDTH):
    S, W = rt.shape
    rows = min(rows, S)
    hp = width // HEAD_N
    row = lambda x: x.reshape(1, W).astype(F32)
    seq = pl.BlockSpec((rows, width), lambda h, c: (c, h))
    par = pl.BlockSpec((1, width), lambda h, c: (0, h))
    return pl.pallas_call(
        _rwkv_scan_kernel,
        out_shape=jax.ShapeDtypeStruct((S, W), F32),
        grid=(W // width, S // rows),
        in_specs=[seq] * 7 + [pl.BlockSpec((rows // CHUNK, 1, width), lambda h, c: (c, 0, h))] + [par] * 2,
        out_specs=seq,
        scratch_shapes=[pltpu.VMEM((hp, HEAD_N, HEAD_N), F32)],
        compiler_params=_params("parallel", "arbitrary"),
        name="rwkv_scan",
    )(at, bt, kt, rt, v, g, bonus, pc, row(lnx_w), row(lnx_b))


def _diff_proj_kernel(x_ref, g_ref, w_ref, pos_ref, freq_ref, o_ref, xs_ref, cos_ref, sin_ref):
    j = pl.program_id(1)
    half = DIFF_QK // 2

    def first_half(shape):
        return (lax.broadcasted_iota(jnp.int32, shape, 1) & (DIFF_QK - 1)) < half

    @pl.when(j == 0)
    def _():
        x = x_ref[...]
        x = x * lax.rsqrt(jnp.mean(x * x, axis=-1, keepdims=True) + NORM_EPS) * g_ref[...]
        xs_ref[...] = x.astype(BF16)
        ang = pos_ref[...] * freq_ref[...]
        cos_ref[...] = jnp.cos(ang)
        sin_ref[...] = jnp.where(first_half(ang.shape), -1.0, 1.0) * jnp.sin(ang)

    acc = jnp.dot(xs_ref[...], w_ref[...], preferred_element_type=F32)

    @pl.when(j < 2)
    def _():
        sc = jnp.where(j == 0, DIFF_QK ** -0.5, 1.0)
        cos = cos_ref[...]
        sin = sin_ref[...]
        fh = first_half(cos.shape)
        for b in range(acc.shape[1] // LANES):
            ls = slice(b * LANES, (b + 1) * LANES)
            x = acc[:, ls]
            partner = jnp.where(fh, pltpu.roll(x, LANES - half, axis=1), pltpu.roll(x, half, axis=1))
            o_ref[:, ls] = ((x * cos + partner * sin) * sc).astype(o_ref.dtype)

    @pl.when(j == 2)
    def _():
        o_ref[...] = acc.astype(o_ref.dtype)


def diff_proj(x, gain, w, positions, *, tm=1024):
    S, K = x.shape
    W = DIFF_WIDTH
    tm = min(tm, S)
    assert w.shape == (K, 3 * W) and S % tm == 0
    inv_freq = ROPE_THETA ** (-(jnp.arange(0, DIFF_QK, 2, dtype=F32) / DIFF_QK))
    freq = jnp.tile(inv_freq, LANES // (DIFF_QK // 2)).reshape(1, LANES)
    pos = positions.reshape(S, 1).astype(F32)
    return pl.pallas_call(
        _diff_proj_kernel,
        out_shape=jax.ShapeDtypeStruct((S, 3 * W), BF16),
        grid=(S // tm, 3),
        in_specs=[pl.BlockSpec((tm, K), lambda i, j: (i, 0)),
                  pl.BlockSpec((1, K), lambda i, j: (0, 0)),
                  pl.BlockSpec((K, W), lambda i, j: (0, j)),
                  pl.BlockSpec((tm, 1), lambda i, j: (i, 0)),
                  pl.BlockSpec((1, LANES), lambda i, j: (0, 0))],
        out_specs=pl.BlockSpec((tm, W), lambda i, j: (i, j)),
        scratch_shapes=[pltpu.VMEM((tm, K), BF16), pltpu.VMEM((tm, LANES), F32), pltpu.VMEM((tm, LANES), F32)],
        compiler_params=_params("parallel", "arbitrary"),
        name="diff_proj",
    )(x, gain.reshape(1, K).astype(F32), w, pos, freq)


def _diff_flash_kernel(qi_tab, ki_tab, q_ref, k_ref, v_ref, lam_ref, sw_ref, o_ref,
                       m_ref, l_ref, acc_ref, s_ref, p_ref, alpha_ref, *, tile, row_block, diag_parts, lambda_init):
    p = pl.program_id(1)
    qi = qi_tab[p]
    ki = ki_tab[p]

    @pl.when(ki == 0)
    def _():
        m_ref[...] = jnp.full_like(m_ref, -jnp.inf)
        l_ref[...] = jnp.zeros_like(l_ref)
        acc_ref[...] = jnp.zeros_like(acc_ref)

    def process(row0, nrows, kmax, masked):
        rows = slice(row0, row0 + nrows)
        q = q_ref[rows, :]
        k = k_ref[0:kmax, :]
        v = v_ref[0:kmax, :]
        lane = lax.broadcasted_iota(jnp.int32, q.shape, 1)
        reps = kmax // LANES
        for c in range(2):
            qc = jnp.where((lane >> (DIFF_QK.bit_length() - 1)) == c, q, jnp.zeros_like(q))
            s_ref[0:nrows, 0:kmax] = _dot_nt(qc, k)
            for r0 in range(0, nrows, row_block):
                ar = slice(row0 + r0, row0 + r0 + row_block)
                lr = slice(r0, r0 + row_block)
                s = s_ref[lr, 0:kmax]
                if masked:
                    qpos = lax.broadcasted_iota(jnp.int32, s.shape, 0) + (row0 + r0)
                    kpos = lax.broadcasted_iota(jnp.int32, s.shape, 1)
                    s = jnp.where(kpos <= qpos, s, -jnp.inf)
                m_old = m_ref[c, ar, :]
                m_new = jnp.maximum(m_old, jnp.max(s, axis=-1, keepdims=True))
                alpha = jnp.exp(m_old - m_new)
                pr = jnp.exp(s - jnp.concatenate([m_new] * reps, axis=1))
                l_ref[c, ar, :] = alpha * l_ref[c, ar, :] + jnp.sum(pr, axis=-1, keepdims=True)
                m_ref[c, ar, :] = m_new
                alpha_ref[lr, :] = alpha
                p_ref[lr, 0:kmax] = pr.astype(BF16)
            acc_ref[c, rows, :] = (alpha_ref[0:nrows, :] * acc_ref[c, rows, :]
                                   + _dot(p_ref[0:nrows, 0:kmax], v))

    @pl.when(ki < qi)
    def _():
        process(0, tile, tile, False)

    @pl.when(ki == qi)
    def _():
        part = tile // diag_parts
        for a in range(diag_parts):
            process(a * part, part, (a + 1) * part, True)
        lp = lam_ref[...]
        lam = (jnp.exp(jnp.sum(lp[0:1] * lp[1:2], axis=-1, keepdims=True))
               - jnp.exp(jnp.sum(lp[2:3] * lp[3:4], axis=-1, keepdims=True)) + lambda_init)
        o = acc_ref[0] / l_ref[0] - lam * (acc_ref[1] / l_ref[1])
        o = o * lax.rsqrt(jnp.mean(o * o, axis=-1, keepdims=True) + SUBLN_EPS)
        o_ref[...] = o * sw_ref[...] * (1.0 - lambda_init)


def diff_flash(qkv, lambdas, subln_w, lambda_init, *, tile=1024, row_block=32, diag_parts=4):
    S = qkv.shape[0]
    tile = min(tile, S)
    assert S % tile == 0 and (tile // diag_parts) % LANES == 0 and (tile // diag_parts) % row_block == 0
    pairs = [(qi, ki) for qi in range(S // tile) for ki in range(qi + 1)]
    qi_tab = jnp.asarray([p[0] for p in pairs], jnp.int32)
    ki_tab = jnp.asarray([p[1] for p in pairs], jnp.int32)
    grid_spec = pltpu.PrefetchScalarGridSpec(
        num_scalar_prefetch=2,
        grid=(DIFF_HEADS, len(pairs)),
        in_specs=[pl.BlockSpec((tile, LANES), lambda h, p, qt, kt: (qt[p], h)),
                  pl.BlockSpec((tile, LANES), lambda h, p, qt, kt: (kt[p], DIFF_HEADS + h)),
                  pl.BlockSpec((tile, LANES), lambda h, p, qt, kt: (kt[p], 2 * DIFF_HEADS + h)),
                  pl.BlockSpec((4, DIFF_QK), lambda h, p, qt, kt: (0, 0)),
                  pl.BlockSpec((1, DIFF_V), lambda h, p, qt, kt: (0, 0))],
        out_specs=pl.BlockSpec((tile, LANES), lambda h, p, qt, kt: (qt[p], h)),
        scratch_shapes=[pltpu.VMEM((2, tile, LANES), F32),
                        pltpu.VMEM((2, tile, LANES), F32),
                        pltpu.VMEM((2, tile, DIFF_V), F32),
                        pltpu.VMEM((tile, tile), F32),
                        pltpu.VMEM((tile, tile), BF16),
                        pltpu.VMEM((tile, LANES), F32)],
    )
    return pl.pallas_call(
        functools.partial(_diff_flash_kernel, tile=tile, row_block=row_block, diag_parts=diag_parts,
                          lambda_init=lambda_init),
        out_shape=jax.ShapeDtypeStruct((S, DIFF_WIDTH), F32),
        grid_spec=grid_spec,
        compiler_params=_params("parallel", "arbitrary"),
        name="diff_flash",
    )(qi_tab, ki_tab, qkv, qkv, qkv, lambdas.astype(F32), subln_w.reshape(1, DIFF_V).astype(F32))


def _mem_attn_kernel(x_ref, g_ref, w_ref, k_ref, v_ref, o_ref, xs_ref):
    @pl.when(pl.program_id(1) == 0)
    def _():
        x = x_ref[...]
        x = x * lax.rsqrt(jnp.mean(x * x, axis=-1, keepdims=True) + NORM_EPS) * g_ref[...]
        xs_ref[...] = x.astype(BF16)

    q = jnp.dot(xs_ref[...], w_ref[...], preferred_element_type=F32).astype(BF16)
    s = _dot_nt(q, k_ref[...]) * (q.shape[1] ** -0.5)
    s = s - jnp.max(s, axis=-1, keepdims=True)
    e = jnp.exp(s)
    pr = e / jnp.sum(e, axis=-1, keepdims=True)
    o_ref[...] = _dot(pr.astype(BF16), v_ref[...]).astype(o_ref.dtype)


def mem_attn(x, gain, wq, k, v, *, tm=1024):
    S, D = x.shape
    M = k.shape[0]
    hd = D // MEM_HEADS
    tm = min(tm, S)
    return pl.pallas_call(
        _mem_attn_kernel,
        out_shape=jax.ShapeDtypeStruct((S, D), BF16),
        grid=(S // tm, MEM_HEADS),
        in_specs=[pl.BlockSpec((tm, D), lambda i, j: (i, 0)),
                  pl.BlockSpec((1, D), lambda i, j: (0, 0)),
                  pl.BlockSpec((D, hd), lambda i, j: (0, j)),
                  pl.BlockSpec((M, hd), lambda i, j: (0, j)),
                  pl.BlockSpec((M, hd), lambda i, j: (0, j))],
        out_specs=pl.BlockSpec((tm, hd), lambda i, j: (i, j)),
        scratch_shapes=[pltpu.VMEM((tm, D), BF16)],
        compiler_params=_params("parallel", "arbitrary"),
        name="mem_attn",
    )(x, gain.reshape(1, D).astype(F32), wq, k, v)


def _first_argmax(x, lane, big):
    m = jnp.max(x, axis=-1, keepdims=True)
    idx = jnp.min(jnp.where(x == m, lane, big), axis=-1, keepdims=True)
    return m, idx


def _router_kernel(h_ref, gain_ref, wg_ref, bg_ref, we_ref, be_ref, hn_ref, eid_ref, ew_ref, cnt_ref):
    @pl.when(pl.program_id(0) == 0)
    def _():
        cnt_ref[...] = jnp.zeros_like(cnt_ref)

    x = h_ref[...]
    hn = x * lax.rsqrt(jnp.mean(x * x, axis=-1, keepdims=True) + NORM_EPS) * gain_ref[...]
    hn_ref[...] = hn
    hn_hi = hn.astype(BF16)
    hn_lo = (hn - hn_hi.astype(F32)).astype(BF16)

    def logits(w_ref, b_ref):
        w = w_ref[...]
        w_hi = w.astype(BF16)
        w_lo = (w - w_hi.astype(F32)).astype(BF16)
        return _dot(hn_hi, w_hi) + (_dot(hn_hi, w_lo) + _dot(hn_lo, w_hi)) + b_ref[...]
    g_logits = logits(wg_ref, bg_ref)
    e_logits = logits(we_ref, be_ref)
    tm = x.shape[0]
    lane_g = lax.broadcasted_iota(jnp.int32, (tm, N_GROUPS), 1)
    g_max, g_idx = _first_argmax(g_logits, lane_g, N_GROUPS)
    g_w = 1.0 / jnp.sum(jnp.exp(g_logits - g_max), axis=-1, keepdims=True)
    lane_e = lax.broadcasted_iota(jnp.int32, (tm, N_EXPERTS), 1)
    in_group = (lane_e // EXPERTS_PER_GROUP) == g_idx
    el = jnp.where(in_group, e_logits, -jnp.inf)
    e_max = jnp.max(el, axis=-1, keepdims=True)
    ex = jnp.exp(el - e_max)
    prob = ex / jnp.sum(ex, axis=-1, keepdims=True)
    prob = jnp.where(in_group, prob, -1.0)
    p1, i1 = _first_argmax(prob, lane_e, N_EXPERTS)
    p2, i2 = _first_argmax(jnp.where(lane_e == i1, -1.0, prob), lane_e, N_EXPERTS)
    tot = p1 + p2
    lane_o = lax.broadcasted_iota(jnp.int32, (tm, LANES), 1)
    eid_ref[...] = jnp.where(lane_o == 0, i1, jnp.where(lane_o == 1, i2, 0))
    ew_ref[...] = jnp.where(lane_o == 0, g_w * (p1 / tot), jnp.where(lane_o == 1, g_w * (p2 / tot), 0.0))
    chosen = ((lane_o == i1) | (lane_o == i2)).astype(F32)
    cnt_ref[...] = cnt_ref[...] + jnp.sum(chosen, axis=0, keepdims=True).astype(jnp.int32)


def router(h, gain, wg, bg, we, be, *, tm=512):
    S, D = h.shape
    tm = min(tm, S)
    full = lambda a: pl.BlockSpec(a.shape, lambda i: (0, 0))
    args = [h, gain.reshape(1, D).astype(F32), wg.astype(F32), bg.reshape(1, -1).astype(F32),
            we.astype(F32), be.reshape(1, -1).astype(F32)]
    return pl.pallas_call(
        _router_kernel,
        out_shape=[jax.ShapeDtypeStruct((S, D), F32),
                   jax.ShapeDtypeStruct((S, LANES), jnp.int32),
                   jax.ShapeDtypeStruct((S, LANES), F32),
                   jax.ShapeDtypeStruct((1, LANES), jnp.int32)],
        grid=(S // tm,),
        in_specs=[pl.BlockSpec((tm, D), lambda i: (i, 0))] + [full(a) for a in args[1:]],
        out_specs=[pl.BlockSpec((tm, D), lambda i: (i, 0)),
                   pl.BlockSpec((tm, LANES), lambda i: (i, 0)),
                   pl.BlockSpec((tm, LANES), lambda i: (i, 0)),
                   pl.BlockSpec((1, LANES), lambda i: (0, 0))],
        compiler_params=_params("arbitrary"),
        name="router",
    )(*args)


def _plan_kernel(cnt_ref, blk_e_ref, blk_n_ref, start_ref):
    nb = blk_e_ref.shape[0]
    sh = MOE_ROWS.bit_length() - 1

    def per_expert(e, b):
        n = cnt_ref[e]
        start_ref[e] = b

        def per_block(t, _):
            blk_e_ref[b + t] = e
            blk_n_ref[b + t] = jnp.minimum(n - t * MOE_ROWS, MOE_ROWS)
            return 0
        k = (n + (MOE_ROWS - 1)) >> sh
        lax.fori_loop(0, k, per_block, 0)
        return b + k
    total = lax.fori_loop(0, N_EXPERTS, per_expert, 0)

    def rest(e, _):
        start_ref[e] = total
        return 0
    lax.fori_loop(N_EXPERTS, start_ref.shape[0], rest, 0)
    last_e = blk_e_ref[jnp.maximum(total - 1, 0)]

    def tail(b, _):
        blk_e_ref[b] = last_e
        blk_n_ref[b] = 0
        return 0
    lax.fori_loop(total, nb, tail, 0)


def block_plan(counts, n_assign):
    nb = (n_assign + N_EXPERTS * (MOE_ROWS - 1)) // MOE_ROWS
    smem = pl.BlockSpec(memory_space=pltpu.SMEM)
    return pl.pallas_call(
        _plan_kernel,
        out_shape=[jax.ShapeDtypeStruct((nb,), jnp.int32), jax.ShapeDtypeStruct((nb,), jnp.int32),
                   jax.ShapeDtypeStruct((LANES,), jnp.int32)],
        in_specs=[smem],
        out_specs=[smem, smem, smem],
        name="block_plan",
    )(counts.reshape(LANES))


def _dest_kernel(eid_ref, start_ref, dest_ref, carry_ref):
    @pl.when(pl.program_id(0) == 0)
    def _():
        carry_ref[...] = jnp.zeros_like(carry_ref)

    eid = eid_ref[...]
    tm = eid.shape[0]
    lane = lax.broadcasted_iota(jnp.int32, (tm, LANES), 1)
    oh0 = lane == eid[:, 0:1]
    oh1 = lane == eid[:, 1:2]
    both = jnp.where(oh0 | oh1, 1.0, 0.0).astype(BF16)
    r = lax.broadcasted_iota(jnp.int32, (tm, tm), 0)
    c = lax.broadcasted_iota(jnp.int32, (tm, tm), 1)
    before = _dot(jnp.where(r > c, 1.0, 0.0).astype(BF16), both) + carry_ref[...]
    pos = before + (start_ref[...] * MOE_ROWS).astype(F32)
    d0 = jnp.sum(jnp.where(oh0, pos, 0.0), axis=-1, keepdims=True).astype(jnp.int32)
    d1 = jnp.sum(jnp.where(oh1, pos, 0.0), axis=-1, keepdims=True).astype(jnp.int32)
    dest_ref[...] = jnp.where(lane == 0, d0, jnp.where(lane == 1, d1, 0))
    carry_ref[...] = carry_ref[...] + jnp.sum(both.astype(F32), axis=0, keepdims=True)


def assignment_dest(eid, blk_start, *, tm=512):
    S = eid.shape[0]
    tm = min(tm, S)
    return pl.pallas_call(
        _dest_kernel,
        out_shape=jax.ShapeDtypeStruct((S, LANES), jnp.int32),
        grid=(S // tm,),
        in_specs=[pl.BlockSpec((tm, LANES), lambda i: (i, 0)), pl.BlockSpec((1, LANES), lambda i: (0, 0))],
        out_specs=pl.BlockSpec((tm, LANES), lambda i: (i, 0)),
        scratch_shapes=[pltpu.VMEM((1, LANES), F32)],
        compiler_params=_params("arbitrary"),
        name="assignment_dest",
    )(eid, blk_start.reshape(1, LANES))


def _for_rows(n, fn):
    sh = MOE_DMA_UNROLL.bit_length() - 1

    def group(t, _):
        for u in range(MOE_DMA_UNROLL):
            fn(t * MOE_DMA_UNROLL + u)
        return 0
    lax.fori_loop(0, n >> sh, group, 0)

    def one(r, _):
        fn(r)
        return 0
    lax.fori_loop((n >> sh) << sh, n, one, 0)


def _experts_kernel(blk_e, blk_n, dest, hn_hbm, wg_ref, wu_ref, wd_ref, out_hbm,
                    xg_ref, xb_ref, acc_ref, wgb_ref, wub_ref, wdb_ref, inv_ref, gsem, ssem):
    i = pl.program_id(0)
    j = pl.program_id(1)
    nb = pl.num_programs(0)
    nsplit = pl.num_programs(1)
    n_tok = hn_hbm.shape[0]
    sub_shift = MOE_SUB.bit_length() - 1
    k_shift = TOP_K.bit_length() - 1

    def n_sub_of(b):
        return (blk_n[b] + (MOE_SUB - 1)) >> sub_shift

    def gather(b, r):
        tok = inv_ref[b * MOE_ROWS + r] >> k_shift
        return pltpu.make_async_copy(hn_hbm.at[pl.ds(tok, 1)], xg_ref.at[pl.ds(r, 1)], gsem)

    def scatter(b, r):
        a = inv_ref[b * MOE_ROWS + r]
        row = (a & (TOP_K - 1)) * n_tok + (a >> k_shift)
        return pltpu.make_async_copy(acc_ref.at[b & 1, pl.ds(r, 1)], out_hbm.at[pl.ds(row, 1)], ssem)

    def n_gather(b):
        return pl.multiple_of(((blk_n[b] + 7) >> 3) << 3, 8)

    def start_gather(b):
        def pad(r, _):
            inv_ref[b * MOE_ROWS + r] = 0
            return 0
        lax.fori_loop(blk_n[b], n_gather(b), pad, 0)
        _for_rows(n_gather(b), lambda r: gather(b, r).start(priority=1))

    def wait_gather(b):
        n = n_gather(b)

        @pl.when(n > 0)
        def _():
            pltpu.make_async_copy(hn_hbm.at[pl.ds(0, n)], xg_ref.at[pl.ds(0, n)], gsem).wait()

    def start_scatter(b):
        _for_rows(blk_n[b], lambda r: scatter(b, r).start(priority=1))

    def wait_scatter(b):
        n8 = pl.multiple_of((blk_n[b] >> 3) << 3, 8)

        @pl.when(n8 > 0)
        def _():
            pltpu.make_async_copy(acc_ref.at[b & 1, pl.ds(0, n8)], out_hbm.at[pl.ds(0, n8)], ssem).wait()

        def one(r, _):
            scatter(b, r).wait()
            return 0
        lax.fori_loop(n8, blk_n[b], one, 0)

    n_sub = n_sub_of(i)
    slot = i & 1

    @pl.when(j == 0)
    def _():
        @pl.when(i == 0)
        def _():
            def invert(a):
                inv_ref[dest[a]] = a
            _for_rows(dest.shape[0], invert)
            xg_ref[...] = jnp.zeros_like(xg_ref)
            start_gather(0)
        wait_gather(i)

        def cast(sb, _):
            rs = pl.ds(pl.multiple_of(sb * MOE_SUB, MOE_SUB), MOE_SUB)
            xb_ref[rs, :] = xg_ref[rs, :].astype(BF16)
            return 0
        lax.fori_loop(0, n_sub, cast, 0)

        @pl.when(i + 1 < nb)
        def _():
            start_gather(i + 1)

    @pl.when(n_sub > 0)
    def _():
        wgb_ref[...] = wg_ref[0].astype(BF16)
        wub_ref[...] = wu_ref[0].astype(BF16)
        wdb_ref[...] = wd_ref[0].astype(BF16)

    def sub_block(sb, _):
        rs = pl.ds(pl.multiple_of(sb * MOE_SUB, MOE_SUB), MOE_SUB)
        xb = xb_ref[rs, :]
        gate = _dot(xb, wgb_ref[...])
        hid = (gate * _sigmoid(gate)) * _dot(xb, wub_ref[...])
        part = _dot(hid.astype(BF16), wdb_ref[...])

        @pl.when(j == 0)
        def _():
            acc_ref[slot, rs, :] = part

        @pl.when(j > 0)
        def _():
            acc_ref[slot, rs, :] = acc_ref[slot, rs, :] + part
        return 0
    lax.fori_loop(0, n_sub, sub_block, 0)

    @pl.when(j == nsplit - 1)
    def _():
        @pl.when(i > 0)
        def _():
            wait_scatter(i - 1)
        start_scatter(i)

        @pl.when(i == nb - 1)
        def _():
            wait_scatter(i)


def experts(hn, blk_e, blk_n, dest, w_gate, w_up, w_down):
    S, D = hn.shape
    E, _, DE = w_gate.shape
    nb = blk_e.shape[0]
    de = DE // MOE_SPLIT
    split = lambda i, j, bn: jnp.where(bn[i] > 0, j, MOE_SPLIT - 1)
    grid_spec = pltpu.PrefetchScalarGridSpec(
        num_scalar_prefetch=3,
        grid=(nb, MOE_SPLIT),
        in_specs=[pl.BlockSpec(memory_space=pl.ANY),
                  pl.BlockSpec((1, D, de), lambda i, j, be, bn, ds: (be[i], 0, split(i, j, bn))),
                  pl.BlockSpec((1, D, de), lambda i, j, be, bn, ds: (be[i], 0, split(i, j, bn))),
                  pl.BlockSpec((1, de, D), lambda i, j, be, bn, ds: (be[i], split(i, j, bn), 0))],
        out_specs=pl.BlockSpec(memory_space=pl.ANY),
        scratch_shapes=[pltpu.VMEM((MOE_ROWS, D), F32),
                        pltpu.VMEM((MOE_ROWS, D), BF16),
                        pltpu.VMEM((2, MOE_ROWS, D), F32),
                        pltpu.VMEM((D, de), BF16),
                        pltpu.VMEM((D, de), BF16),
                        pltpu.VMEM((de, D), BF16),
                        pltpu.SMEM((nb * MOE_ROWS,), jnp.int32),
                        pltpu.SemaphoreType.DMA(()),
                        pltpu.SemaphoreType.DMA(())],
    )
    return pl.pallas_call(
        _experts_kernel,
        out_shape=jax.ShapeDtypeStruct((TOP_K * S, D), F32),
        grid_spec=grid_spec,
        compiler_params=_params("arbitrary", "arbitrary"),
        name="experts",
    )(blk_e, blk_n, dest, hn, w_gate, w_up, w_down)


def _combine_kernel(*refs, has_norm):
    h_ref, y0_ref, y1_ref, w_ref = refs[:4]
    o_ref = refs[-1]
    w = w_ref[...]
    x = h_ref[...] + (y0_ref[0] * w[:, 0:1] + y1_ref[0] * w[:, 1:2])
    if has_norm:
        x = x * lax.rsqrt(jnp.mean(x * x, axis=-1, keepdims=True) + NORM_EPS) * refs[4][...]
    o_ref[...] = x


def combine(h, y_slots, slot_w, gain=None, *, tm=512):
    S, D = h.shape
    tm = min(tm, S)
    y3 = y_slots.reshape(TOP_K, S, D)
    in_specs = [pl.BlockSpec((tm, D), lambda i: (i, 0)),
                pl.BlockSpec((1, tm, D), lambda i: (0, i, 0)),
                pl.BlockSpec((1, tm, D), lambda i: (1, i, 0)),
                pl.BlockSpec((tm, LANES), lambda i: (i, 0))]
    args = [h, y3, y3, slot_w]
    if gain is not None:
        in_specs.append(pl.BlockSpec((1, D), lambda i: (0, 0)))
        args.append(gain.reshape(1, D).astype(F32))
    return pl.pallas_call(
        functools.partial(_combine_kernel, has_norm=gain is not None),
        out_shape=jax.ShapeDtypeStruct((S, D), F32),
        grid=(S // tm,),
        in_specs=in_specs,
        out_specs=pl.BlockSpec((tm, D), lambda i: (i, 0)),
        compiler_params=_params("parallel"),
        name="combine",
    )(*args)


def moe(h, gain, wg, bg, we, be, w_gate, w_up, w_down, final_gain=None):
    S = h.shape[0]
    hn, eid, slot_w, counts = router(h, gain, wg, bg, we, be)
    blk_e, blk_n, blk_start = block_plan(counts, S * TOP_K)
    dest = assignment_dest(eid, blk_start)[:, :TOP_K].reshape(S * TOP_K)
    y_slots = experts(hn, blk_e, blk_n, dest, w_gate, w_up, w_down)
    return combine(h, y_slots, slot_w, final_gain)


def kernel(x, mem, positions, mix_norm, w_in, shift_mu, decay_w0, decay_w2, aaa_a0, aaa_a2, gate_g2, k_k, k_a,
           r_k, lnx_w, lnx_b, diff_lambda, subln_w, w_out, mem_q_norm, mem_kv_norm, wq_mem, wk_mem, wv_mem,
           wo_mem, moe_norm, router_group_w, router_group_b, router_expert_w, router_expert_b, expert_gate,
           expert_up, expert_down, final_norm):
    B, S, D = x.shape
    depth = w_in.shape[0]
    outs = []
    for b in range(B):
        h = x[b]
        memb = mem[b]
        for l in range(depth):
            lambda_init = 0.8 - 0.6 * math.exp(-0.3 * l)
            w_in_b = w_in[l].astype(BF16)
            proj_r = matmul(h, w_in_b[:, :RWKV_COLS], gain=mix_norm[l], tm=1024, tn=RWKV_COLS // 2)
            qkv = diff_proj(h, mix_norm[l], w_in_b[:, RWKV_COLS:], positions[b])
            pre = rwkv_prep(proj_r, shift_mu[l], decay_w0[l], decay_w2[l], aaa_a0[l], aaa_a2[l], gate_g2[l],
                            k_k[l], k_a[l], r_k[l])
            y_rwkv = rwkv_scan(*pre, lnx_w[l], lnx_b[l])
            y_diff = diff_flash(qkv, diff_lambda[l], subln_w[l], lambda_init)
            h = matmul([y_rwkv, y_diff], w_out[l].astype(BF16), residual=h, tm=1024)
            km = matmul(memb, wk_mem[l].astype(BF16), gain=mem_kv_norm[l], out_dtype=BF16)
            vm = matmul(memb, wv_mem[l].astype(BF16), gain=mem_kv_norm[l], out_dtype=BF16)
            o = mem_attn(h, mem_q_norm[l], wq_mem[l].astype(BF16), km, vm)
            h = matmul(o, wo_mem[l].astype(BF16), residual=h, tm=1024)
            h = moe(h, moe_norm[l], router_group_w[l], router_group_b[l], router_expert_w[l], router_expert_b[l],
                    expert_gate[l], expert_up[l], expert_down[l], final_norm if l == depth - 1 else None)
        outs.append(h)
    return jnp.stack(outs, axis=0)
```

```python
import functools
import math

import numpy as np
import jax
import jax.numpy as jnp
from jax import lax
from jax.experimental import pallas as pl
from jax.experimental.pallas import tpu as pltpu

F32 = jnp.float32
BF16 = jnp.bfloat16
HIGHEST = lax.Precision.HIGHEST

RWKV_HEADS = 16
HEAD_N = 64
RWKV_WIDTH = RWKV_HEADS * HEAD_N
DECAY_LORA = 64
AAA_LORA = 64
GATE_LORA = 128
RWKV_COLS = 3 * RWKV_WIDTH + DECAY_LORA + AAA_LORA + GATE_LORA
DIFF_HEADS = 8
DIFF_QK = 64
DIFF_V = 128
DIFF_WIDTH = DIFF_HEADS * DIFF_V
ROPE_THETA = 10000.0
MEM_HEADS = 4
N_GROUPS = 8
EXPERTS_PER_GROUP = 8
N_EXPERTS = N_GROUPS * EXPERTS_PER_GROUP
TOP_K = 2
NORM_EPS = 1e-6
LNX_EPS = 64e-5
SUBLN_EPS = 1e-5

LANES = 128
VMEM_LIMIT = 56 * 1024 * 1024

CHUNK = 64
SCAN_ROWS = 256
SCAN_WIDTH = 512
MOE_ROWS = 512
MOE_SUB = 256
MOE_TAIL = 128
MOE_SPLIT = 2
MOE_DMA_UNROLL = 8


def _params(*sem):
    return pltpu.CompilerParams(dimension_semantics=sem, vmem_limit_bytes=VMEM_LIMIT)


def _mm_kernel(*refs, n_x, has_norm, has_res):
    it = iter(refs)
    x_refs = [next(it) for _ in range(n_x)]
    g_ref = next(it) if has_norm else None
    w_ref = next(it)
    r_ref = next(it) if has_res else None
    o_ref = next(it)
    xs_ref = next(it)

    @pl.when(pl.program_id(1) == 0)
    def _():
        if has_norm:
            x = x_refs[0][...].astype(F32)
            x = x * lax.rsqrt(jnp.mean(x * x, axis=-1, keepdims=True) + NORM_EPS) * g_ref[...]
            xs_ref[...] = x.astype(BF16)
        else:
            k0 = 0
            for x_ref in x_refs:
                xs_ref[:, k0:k0 + x_ref.shape[1]] = x_ref[...].astype(BF16)
                k0 += x_ref.shape[1]

    acc = jnp.dot(xs_ref[...], w_ref[...], preferred_element_type=F32)
    if has_res:
        acc = acc + r_ref[...]
    o_ref[...] = acc.astype(o_ref.dtype)


def matmul(x, w, *, gain=None, residual=None, out_dtype=F32, tm=512, tn=512):
    xs = list(x) if isinstance(x, (list, tuple)) else [x]
    assert gain is None or len(xs) == 1
    M = xs[0].shape[0]
    K, N = w.shape
    assert sum(p.shape[1] for p in xs) == K
    tm = min(tm, M)
    assert M % tm == 0 and N % tn == 0, (M, N, tm, tn)
    in_specs = [pl.BlockSpec((tm, p.shape[1]), lambda i, j: (i, 0)) for p in xs]
    args = list(xs)
    if gain is not None:
        in_specs.append(pl.BlockSpec((1, K), lambda i, j: (0, 0)))
        args.append(gain.reshape(1, K).astype(F32))
    in_specs.append(pl.BlockSpec((K, tn), lambda i, j: (0, j)))
    args.append(w)
    if residual is not None:
        in_specs.append(pl.BlockSpec((tm, tn), lambda i, j: (i, j)))
        args.append(residual)
    return pl.pallas_call(
        functools.partial(_mm_kernel, n_x=len(xs), has_norm=gain is not None, has_res=residual is not None),
        out_shape=jax.ShapeDtypeStruct((M, N), out_dtype),
        grid=(M // tm, N // tn),
        in_specs=in_specs,
        out_specs=pl.BlockSpec((tm, tn), lambda i, j: (i, j)),
        scratch_shapes=[pltpu.VMEM((tm, K), BF16)],
        compiler_params=_params("parallel", "arbitrary"),
        name="matmul",
    )(*args)


def _sigmoid(x):
    return 1.0 / (1.0 + jnp.exp(-x))


def _rwkv_prep_kernel(p_ref, pp_ref, mu_ref, w0_ref, w2_ref, a0_ref, a2_ref, g2_ref, kk_ref, ka_ref, rk_ref,
                      hsum_ref, at_o, bt_o, kt_o, rt_o, v_o, g_o, bonus_o, pc_o):
    W = RWKV_WIDTH
    C = CHUNK
    p = p_ref[...]
    last = jnp.where(pl.program_id(0) == 0, 0.0, pp_ref[7:8, :])
    prev = pltpu.roll(p, 1, axis=0)
    row = lax.broadcasted_iota(jnp.int32, p.shape, 0)
    prev = jnp.where(row == 0, last, prev)
    ps = p + (prev - p) * mu_ref[...]
    r = ps[:, 0:W]
    k = ps[:, W:2 * W]
    v = ps[:, 2 * W:3 * W]
    o = 3 * W
    wd = ps[:, o:o + DECAY_LORA]
    ad = ps[:, o + DECAY_LORA:o + DECAY_LORA + AAA_LORA]
    gd = ps[:, o + DECAY_LORA + AAA_LORA:o + DECAY_LORA + AAA_LORA + GATE_LORA]
    z = w0_ref[...] + jnp.dot(jnp.tanh(wd).astype(BF16), w2_ref[...], preferred_element_type=F32)
    nz = -z
    softplus = jnp.maximum(nz, 0.0) + jnp.log(1.0 + jnp.exp(-jnp.abs(nz)))
    w_raw = -softplus - 0.5
    lw = -jnp.exp(w_raw)
    tm = p.shape[0]
    rr = lax.broadcasted_iota(jnp.int32, (tm, tm), 0)
    cc = lax.broadcasted_iota(jnp.int32, (tm, tm), 1)
    sh = C.bit_length() - 1
    tri = (((rr >> sh) == (cc >> sh)) & (rr >= cc)).astype(F32)
    cs = _dot(tri, lw, HIGHEST)
    a = _sigmoid(a0_ref[...] + jnp.dot(ad.astype(BF16), a2_ref[...], preferred_element_type=F32))
    g_o[...] = jnp.dot(_sigmoid(gd).astype(BF16), g2_ref[...], preferred_element_type=F32)

    def head_sum(x):
        hi = x.astype(BF16)
        lo = (x - hi.astype(F32)).astype(BF16)
        ones = hsum_ref[...]
        wb = ones.shape[0]
        cols = [slice(c, c + wb) for c in range(0, x.shape[1], wb)]
        return jnp.concatenate([_dot(hi[:, cs], ones) + _dot(lo[:, cs], ones) for cs in cols], axis=1)

    kk = k * kk_ref[...]
    kk = kk / jnp.maximum(jnp.sqrt(head_sum(kk * kk)), 1e-12)
    k2 = k * (1.0 + (a - 1.0) * ka_ref[...])
    e_neg = jnp.exp(-cs)
    at_o[...] = (-kk * jnp.exp(cs - lw)).astype(at_o.dtype)
    bt_o[...] = (kk * a) * e_neg
    kt_o[...] = k2 * e_neg
    rt_o[...] = r * jnp.exp(cs)
    v_o[...] = v
    bonus_o[...] = head_sum(r * k2 * rk_ref[...]) * v
    for q in range(tm // C):
        pc_o[q] = jnp.exp(cs[q * C + C - 1:q * C + C, :])


def rwkv_prep(p, mu, w0, w2, a0, a2, g2, k_k, k_a, r_k, *, tm=256):
    S = p.shape[0]
    tm = min(tm, S)
    assert tm % CHUNK == 0
    W = RWKV_WIDTH
    row = lambda x: x.reshape(1, -1).astype(F32)
    full = lambda a: pl.BlockSpec(a.shape, lambda i: (0, 0))
    head = jnp.arange(2 * LANES, dtype=jnp.int32) // HEAD_N
    hsum = (head[:, None] == head[None, :]).astype(BF16)
    args = [p, p, row(mu), row(w0), w2.astype(BF16), row(a0), a2.astype(BF16), g2.astype(BF16),
            row(k_k), row(k_a), row(r_k), hsum]
    in_specs = [pl.BlockSpec((tm, RWKV_COLS), lambda i: (i, 0)),
                pl.BlockSpec((8, RWKV_COLS), lambda i: (jnp.maximum(i * (tm // 8) - 1, 0), 0))]
    in_specs += [full(a) for a in args[2:]]
    f32 = jax.ShapeDtypeStruct((S, W), F32)
    blk = pl.BlockSpec((tm, W), lambda i: (i, 0))
    return pl.pallas_call(
        _rwkv_prep_kernel,
        out_shape=[jax.ShapeDtypeStruct((S, W), BF16)] + [f32] * 6 + [jax.ShapeDtypeStruct((S // CHUNK, 1, W), F32)],
        grid=(S // tm,),
        in_specs=in_specs,
        out_specs=[blk] * 7 + [pl.BlockSpec((tm // CHUNK, 1, W), lambda i: (i, 0, 0))],
        compiler_params=_params("parallel"),
        name="rwkv_prep",
    )(*args)


def _dot_nt(a, b, precision=None):
    return lax.dot_general(a, b, (((1,), (1,)), ((), ())), preferred_element_type=F32, precision=precision)


def _dot_tn(a, b, precision=None):
    return lax.dot_general(a, b, (((0,), (0,)), ((), ())), preferred_element_type=F32, precision=precision)


def _dot(a, b, precision=None):
    return jnp.dot(a, b, preferred_element_type=F32, precision=precision)


def _rwkv_scan_kernel(at_ref, bt_ref, kt_ref, rt_ref, v_ref, g_ref, bonus_ref, pc_ref, lnw_ref, lnb_ref,
                      y_ref, state_ref):
    C = CHUNK
    N = HEAD_N

    @pl.when(pl.program_id(1) == 0)
    def _():
        state_ref[...] = jnp.zeros_like(state_ref)

    row = lax.broadcasted_iota(jnp.int32, (C, C), 0)
    col = lax.broadcasted_iota(jnp.int32, (C, C), 1)
    eye = (row == col).astype(F32)
    strict = row > col
    incl = row >= col

    n_chunks = rt_ref.shape[0] // C
    heads = rt_ref.shape[1] // N
    items = [(j, q) for j in range(heads) for q in range(n_chunks)]
    G = range(len(items))

    def tile(ref, j, q):
        return ref[q * C:(q + 1) * C, j * N:(j + 1) * N]

    def par(ref, j):
        return ref[:, j * N:(j + 1) * N]

    at = [tile(at_ref, j, q) for j, q in items]
    bt = [tile(bt_ref, j, q) for j, q in items]
    kt = [tile(kt_ref, j, q) for j, q in items]
    rt = [tile(rt_ref, j, q) for j, q in items]
    v = [tile(v_ref, j, q) for j, q in items]
    pc = [pc_ref[q, :, j * N:(j + 1) * N] for j, q in items]
    vb = [x.astype(BF16) for x in v]
    btb = [x.astype(BF16) for x in bt]
    ktb = [x.astype(BF16) for x in kt]
    rtb = [x.astype(BF16) for x in rt]
    n_ab = [jnp.where(strict, _dot_nt(at[g], btb[g]), 0.0) for g in G]
    a_ak = [jnp.where(strict, _dot_nt(at[g], ktb[g]), 0.0).astype(BF16) for g in G]
    a_rb = [jnp.where(incl, _dot_nt(rtb[g], btb[g]), 0.0).astype(BF16) for g in G]
    a_rk = [jnp.where(incl, _dot_nt(rtb[g], ktb[g]), 0.0).astype(BF16) for g in G]
    akv = [_dot(a_ak[g], vb[g]) for g in G]
    t = None
    b = 1
    while b < C:
        sh = (2 * b).bit_length() - 1
        low_left = ((row >> sh) == (col >> sh)) & ((row & b) != 0) & ((col & b) == 0)
        nb = [jnp.where(low_left, n_ab[g], 0.0) for g in G]
        if b == 1:
            t = [eye + nb[g] for g in G]
        else:
            tb = [t[g].astype(BF16) for g in G]
            z = [_dot(nb[g].astype(BF16), tb[g]).astype(BF16) for g in G]
            t = [t[g] + _dot(tb[g], z[g]) for g in G]
        b *= 2
    tb = [t[g].astype(BF16) for g in G]
    wm = [_dot(tb[g], at[g]).astype(BF16) for g in G]
    u0 = [_dot(tb[g], akv[g].astype(BF16)).astype(BF16) for g in G]
    rm = [(rt[g] + _dot(a_rb[g], wm[g])).astype(BF16) for g in G]
    y0 = [_dot(a_rb[g], u0[g]) + _dot(a_rk[g], vb[g]) for g in G]
    bp = [(bt[g] * pc[g]).astype(BF16) for g in G]
    kp = [(kt[g] * pc[g]).astype(BF16) for g in G]
    mp = [_dot_tn(wm[g], bp[g]).astype(BF16) for g in G]
    s_add = [_dot_tn(u0[g], bp[g]) + _dot_tn(vb[g], kp[g]) for g in G]

    for j in range(heads):
        s = state_ref[j]
        for q in range(n_chunks):
            g = j * n_chunks + q
            sb = s.astype(BF16)
            y = _dot_nt(rm[g], sb) + y0[g]
            s = s * pc[g] + _dot(sb, mp[g]) + s_add[g]
            mean = jnp.mean(y, axis=-1, keepdims=True)
            yc = y - mean
            var = jnp.mean(yc * yc, axis=-1, keepdims=True)
            yn = yc * lax.rsqrt(var + LNX_EPS) * par(lnw_ref, j) + par(lnb_ref, j)
            y_ref[q * C:(q + 1) * C, j * N:(j + 1) * N] = (yn + tile(bonus_ref, j, q)) * tile(g_ref, j, q)
        state_ref[j] = s


def rwkv_scan(at, bt, kt, rt, v, g, bonus, pc, lnx_w, lnx_b, *, rows=SCAN_ROWS, width=SCAN_WIDTH):
    S, W = rt.shape
    rows = min(rows, S)
    hp = width // HEAD_N
    row = lambda x: x.reshape(1, W).astype(F32)
    seq = pl.BlockSpec((rows, width), lambda h, c: (c, h))
    par = pl.BlockSpec((1, width), lambda h, c: (0, h))
    return pl.pallas_call(
        _rwkv_scan_kernel,
        out_shape=jax.ShapeDtypeStruct((S, W), F32),
        grid=(W // width, S // rows),
        in_specs=[seq] * 7 + [pl.BlockSpec((rows // CHUNK, 1, width), lambda h, c: (c, 0, h))] + [par] * 2,
        out_specs=seq,
        scratch_shapes=[pltpu.VMEM((hp, HEAD_N, HEAD_N), F32)],
        compiler_params=_params("parallel", "arbitrary"),
        name="rwkv_scan",
    )(at, bt, kt, rt, v, g, bonus, pc, row(lnx_w), row(lnx_b))


def _diff_proj_kernel(x_ref, g_ref, w_ref, pos_ref, freq_ref, o_ref, xs_ref, cos_ref, sin_ref):
    j = pl.program_id(1)
    half = DIFF_QK // 2

    def first_half(shape):
        return (lax.broadcasted_iota(jnp.int32, shape, 1) & (DIFF_QK - 1)) < half

    @pl.when(j == 0)
    def _():
        x = x_ref[...]
        x = x * lax.rsqrt(jnp.mean(x * x, axis=-1, keepdims=True) + NORM_EPS) * g_ref[...]
        xs_ref[...] = x.astype(BF16)
        ang = pos_ref[...] * freq_ref[...]
        cos_ref[...] = jnp.cos(ang)
        sin_ref[...] = jnp.where(first_half(ang.shape), -1.0, 1.0) * jnp.sin(ang)

    acc = jnp.dot(xs_ref[...], w_ref[...], preferred_element_type=F32)

    @pl.when(j < 2)
    def _():
        sc = jnp.where(j == 0, DIFF_QK ** -0.5, 1.0)
        cos = cos_ref[...]
        sin = sin_ref[...]
        fh = first_half(cos.shape)
        for b in range(acc.shape[1] // LANES):
            ls = slice(b * LANES, (b + 1) * LANES)
            x = acc[:, ls]
            partner = jnp.where(fh, pltpu.roll(x, LANES - half, axis=1), pltpu.roll(x, half, axis=1))
            o_ref[:, ls] = ((x * cos + partner * sin) * sc).astype(o_ref.dtype)

    @pl.when(j == 2)
    def _():
        o_ref[...] = acc.astype(o_ref.dtype)


def diff_proj(x, gain, w, positions, *, tm=1024):
    S, K = x.shape
    W = DIFF_WIDTH
    tm = min(tm, S)
    assert w.shape == (K, 3 * W) and S % tm == 0
    inv_freq = ROPE_THETA ** (-(jnp.arange(0, DIFF_QK, 2, dtype=F32) / DIFF_QK))
    freq = jnp.tile(inv_freq, LANES // (DIFF_QK // 2)).reshape(1, LANES)
    pos = positions.reshape(S, 1).astype(F32)
    return pl.pallas_call(
        _diff_proj_kernel,
        out_shape=jax.ShapeDtypeStruct((S, 3 * W), BF16),
        grid=(S // tm, 3),
        in_specs=[pl.BlockSpec((tm, K), lambda i, j: (i, 0)),
                  pl.BlockSpec((1, K), lambda i, j: (0, 0)),
                  pl.BlockSpec((K, W), lambda i, j: (0, j)),
                  pl.BlockSpec((tm, 1), lambda i, j: (i, 0)),
                  pl.BlockSpec((1, LANES), lambda i, j: (0, 0))],
        out_specs=pl.BlockSpec((tm, W), lambda i, j: (i, j)),
        scratch_shapes=[pltpu.VMEM((tm, K), BF16), pltpu.VMEM((tm, LANES), F32), pltpu.VMEM((tm, LANES), F32)],
        compiler_params=_params("parallel", "arbitrary"),
        name="diff_proj",
    )(x, gain.reshape(1, K).astype(F32), w, pos, freq)


def _diff_flash_kernel(qi_tab, ki_tab, q_ref, k_ref, v_ref, lam_ref, sw_ref, o_ref,
                       m_ref, l_ref, acc_ref, s_ref, p_ref, alpha_ref, *, tile, row_block, diag_parts, lambda_init):
    p = pl.program_id(1)
    qi = qi_tab[p]
    ki = ki_tab[p]

    @pl.when(ki == 0)
    def _():
        m_ref[...] = jnp.full_like(m_ref, -jnp.inf)
        l_ref[...] = jnp.zeros_like(l_ref)
        acc_ref[...] = jnp.zeros_like(acc_ref)

    def process(row0, nrows, kmax, masked):
        rows = slice(row0, row0 + nrows)
        q = q_ref[rows, :]
        k = k_ref[0:kmax, :]
        v = v_ref[0:kmax, :]
        lane = lax.broadcasted_iota(jnp.int32, q.shape, 1)
        reps = kmax // LANES
        for c in range(2):
            qc = jnp.where((lane >> (DIFF_QK.bit_length() - 1)) == c, q, jnp.zeros_like(q))
            s_ref[0:nrows, 0:kmax] = _dot_nt(qc, k)
            for r0 in range(0, nrows, row_block):
                ar = slice(row0 + r0, row0 + r0 + row_block)
                lr = slice(r0, r0 + row_block)
                s = s_ref[lr, 0:kmax]
                if masked:
                    qpos = lax.broadcasted_iota(jnp.int32, s.shape, 0) + (row0 + r0)
                    kpos = lax.broadcasted_iota(jnp.int32, s.shape, 1)
                    s = jnp.where(kpos <= qpos, s, -jnp.inf)
                m_old = m_ref[c, ar, :]
                m_new = jnp.maximum(m_old, jnp.max(s, axis=-1, keepdims=True))
                alpha = jnp.exp(m_old - m_new)
                pr = jnp.exp(s - jnp.concatenate([m_new] * reps, axis=1))
                l_ref[c, ar, :] = alpha * l_ref[c, ar, :] + jnp.sum(pr, axis=-1, keepdims=True)
                m_ref[c, ar, :] = m_new
                alpha_ref[lr, :] = alpha
                p_ref[lr, 0:kmax] = pr.astype(BF16)
            acc_ref[c, rows, :] = (alpha_ref[0:nrows, :] * acc_ref[c, rows, :]
                                   + _dot(p_ref[0:nrows, 0:kmax], v))

    @pl.when(ki < qi)
    def _():
        process(0, tile, tile, False)

    @pl.when(ki == qi)
    def _():
        part = tile // diag_parts
        for a in range(diag_parts):
            process(a * part, part, (a + 1) * part, True)
        lp = lam_ref[...]
        lam = (jnp.exp(jnp.sum(lp[0:1] * lp[1:2], axis=-1, keepdims=True))
               - jnp.exp(jnp.sum(lp[2:3] * lp[3:4], axis=-1, keepdims=True)) + lambda_init)
        o = acc_ref[0] / l_ref[0] - lam * (acc_ref[1] / l_ref[1])
        o = o * lax.rsqrt(jnp.mean(o * o, axis=-1, keepdims=True) + SUBLN_EPS)
        o_ref[...] = o * sw_ref[...] * (1.0 - lambda_init)


def diff_flash(qkv, lambdas, subln_w, lambda_init, *, tile=1024, row_block=32, diag_parts=2):
    S = qkv.shape[0]
    tile = min(tile, S)
    assert S % tile == 0 and (tile // diag_parts) % LANES == 0 and (tile // diag_parts) % row_block == 0
    pairs = [(qi, ki) for qi in range(S // tile) for ki in range(qi + 1)]
    qi_tab = jnp.asarray([p[0] for p in pairs], jnp.int32)
    ki_tab = jnp.asarray([p[1] for p in pairs], jnp.int32)
    grid_spec = pltpu.PrefetchScalarGridSpec(
        num_scalar_prefetch=2,
        grid=(DIFF_HEADS, len(pairs)),
        in_specs=[pl.BlockSpec((tile, LANES), lambda h, p, qt, kt: (qt[p], h)),
                  pl.BlockSpec((tile, LANES), lambda h, p, qt, kt: (kt[p], DIFF_HEADS + h)),
                  pl.BlockSpec((tile, LANES), lambda h, p, qt, kt: (kt[p], 2 * DIFF_HEADS + h)),
                  pl.BlockSpec((4, DIFF_QK), lambda h, p, qt, kt: (0, 0)),
                  pl.BlockSpec((1, DIFF_V), lambda h, p, qt, kt: (0, 0))],
        out_specs=pl.BlockSpec((tile, LANES), lambda h, p, qt, kt: (qt[p], h)),
        scratch_shapes=[pltpu.VMEM((2, tile, LANES), F32),
                        pltpu.VMEM((2, tile, LANES), F32),
                        pltpu.VMEM((2, tile, DIFF_V), F32),
                        pltpu.VMEM((tile, tile), F32),
                        pltpu.VMEM((tile, tile), BF16),
                        pltpu.VMEM((tile, LANES), F32)],
    )
    return pl.pallas_call(
        functools.partial(_diff_flash_kernel, tile=tile, row_block=row_block, diag_parts=diag_parts,
                          lambda_init=lambda_init),
        out_shape=jax.ShapeDtypeStruct((S, DIFF_WIDTH), F32),
        grid_spec=grid_spec,
        compiler_params=_params("parallel", "arbitrary"),
        name="diff_flash",
    )(qi_tab, ki_tab, qkv, qkv, qkv, lambdas.astype(F32), subln_w.reshape(1, DIFF_V).astype(F32))


def _mem_attn_kernel(x_ref, g_ref, w_ref, k_ref, v_ref, o_ref, xs_ref):
    @pl.when(pl.program_id(1) == 0)
    def _():
        x = x_ref[...]
        x = x * lax.rsqrt(jnp.mean(x * x, axis=-1, keepdims=True) + NORM_EPS) * g_ref[...]
        xs_ref[...] = x.astype(BF16)

    q = jnp.dot(xs_ref[...], w_ref[...], preferred_element_type=F32).astype(BF16)
    s = _dot_nt(q, k_ref[...]) * (q.shape[1] ** -0.5)
    s = s - jnp.max(s, axis=-1, keepdims=True)
    e = jnp.exp(s)
    pr = e / jnp.sum(e, axis=-1, keepdims=True)
    o_ref[...] = _dot(pr.astype(BF16), v_ref[...]).astype(o_ref.dtype)


def mem_attn(x, gain, wq, k, v, *, tm=1024):
    S, D = x.shape
    M = k.shape[0]
    hd = D // MEM_HEADS
    tm = min(tm, S)
    return pl.pallas_call(
        _mem_attn_kernel,
        out_shape=jax.ShapeDtypeStruct((S, D), BF16),
        grid=(S // tm, MEM_HEADS),
        in_specs=[pl.BlockSpec((tm, D), lambda i, j: (i, 0)),
                  pl.BlockSpec((1, D), lambda i, j: (0, 0)),
                  pl.BlockSpec((D, hd), lambda i, j: (0, j)),
                  pl.BlockSpec((M, hd), lambda i, j: (0, j)),
                  pl.BlockSpec((M, hd), lambda i, j: (0, j))],
        out_specs=pl.BlockSpec((tm, hd), lambda i, j: (i, j)),
        scratch_shapes=[pltpu.VMEM((tm, D), BF16)],
        compiler_params=_params("parallel", "arbitrary"),
        name="mem_attn",
    )(x, gain.reshape(1, D).astype(F32), wq, k, v)


def _first_argmax(x, lane, big):
    m = jnp.max(x, axis=-1, keepdims=True)
    idx = jnp.min(jnp.where(x == m, lane, big), axis=-1, keepdims=True)
    return m, idx


def _router_kernel(h_ref, gain_ref, wg_ref, bg_ref, we_ref, be_ref, hn_ref, eid_ref, ew_ref, cnt_ref):
    @pl.when(pl.program_id(0) == 0)
    def _():
        cnt_ref[...] = jnp.zeros_like(cnt_ref)

    x = h_ref[...]
    hn = x * lax.rsqrt(jnp.mean(x * x, axis=-1, keepdims=True) + NORM_EPS) * gain_ref[...]
    hn_ref[...] = hn
    hn_hi = hn.astype(BF16)
    hn_lo = (hn - hn_hi.astype(F32)).astype(BF16)

    def logits(w_ref, b_ref):
        w = w_ref[...]
        w_hi = w.astype(BF16)
        w_lo = (w - w_hi.astype(F32)).astype(BF16)
        return _dot(hn_hi, w_hi) + (_dot(hn_hi, w_lo) + _dot(hn_lo, w_hi)) + b_ref[...]
    g_logits = logits(wg_ref, bg_ref)
    e_logits = logits(we_ref, be_ref)
    tm = x.shape[0]
    lane_g = lax.broadcasted_iota(jnp.int32, (tm, N_GROUPS), 1)
    g_max, g_idx = _first_argmax(g_logits, lane_g, N_GROUPS)
    g_w = 1.0 / jnp.sum(jnp.exp(g_logits - g_max), axis=-1, keepdims=True)
    lane_e = lax.broadcasted_iota(jnp.int32, (tm, N_EXPERTS), 1)
    in_group = (lane_e // EXPERTS_PER_GROUP) == g_idx
    el = jnp.where(in_group, e_logits, -jnp.inf)
    e_max = jnp.max(el, axis=-1, keepdims=True)
    ex = jnp.exp(el - e_max)
    prob = ex / jnp.sum(ex, axis=-1, keepdims=True)
    prob = jnp.where(in_group, prob, -1.0)
    p1, i1 = _first_argmax(prob, lane_e, N_EXPERTS)
    p2, i2 = _first_argmax(jnp.where(lane_e == i1, -1.0, prob), lane_e, N_EXPERTS)
    tot = p1 + p2
    lane_o = lax.broadcasted_iota(jnp.int32, (tm, LANES), 1)
    eid_ref[...] = jnp.where(lane_o == 0, i1, jnp.where(lane_o == 1, i2, 0))
    ew_ref[...] = jnp.where(lane_o == 0, g_w * (p1 / tot), jnp.where(lane_o == 1, g_w * (p2 / tot), 0.0))
    chosen = ((lane_o == i1) | (lane_o == i2)).astype(F32)
    cnt_ref[...] = cnt_ref[...] + jnp.sum(chosen, axis=0, keepdims=True).astype(jnp.int32)


def router(h, gain, wg, bg, we, be, *, tm=512):
    S, D = h.shape
    tm = min(tm, S)
    full = lambda a: pl.BlockSpec(a.shape, lambda i: (0, 0))
    args = [h, gain.reshape(1, D).astype(F32), wg.astype(F32), bg.reshape(1, -1).astype(F32),
            we.astype(F32), be.reshape(1, -1).astype(F32)]
    return pl.pallas_call(
        _router_kernel,
        out_shape=[jax.ShapeDtypeStruct((S, D), F32),
                   jax.ShapeDtypeStruct((S, LANES), jnp.int32),
                   jax.ShapeDtypeStruct((S, LANES), F32),
                   jax.ShapeDtypeStruct((1, LANES), jnp.int32)],
        grid=(S // tm,),
        in_specs=[pl.BlockSpec((tm, D), lambda i: (i, 0))] + [full(a) for a in args[1:]],
        out_specs=[pl.BlockSpec((tm, D), lambda i: (i, 0)),
                   pl.BlockSpec((tm, LANES), lambda i: (i, 0)),
                   pl.BlockSpec((tm, LANES), lambda i: (i, 0)),
                   pl.BlockSpec((1, LANES), lambda i: (0, 0))],
        compiler_params=_params("arbitrary"),
        name="router",
    )(*args)


def _plan_kernel(cnt_ref, blk_e_ref, blk_n_ref, start_ref):
    nb = blk_e_ref.shape[0]
    sh = MOE_ROWS.bit_length() - 1

    def per_expert(e, b):
        n = cnt_ref[e]
        start_ref[e] = b

        def per_block(t, _):
            blk_e_ref[b + t] = e
            blk_n_ref[b + t] = jnp.minimum(n - t * MOE_ROWS, MOE_ROWS)
            return 0
        k = (n + (MOE_ROWS - 1)) >> sh
        lax.fori_loop(0, k, per_block, 0)
        return b + k
    total = lax.fori_loop(0, N_EXPERTS, per_expert, 0)

    def rest(e, _):
        start_ref[e] = total
        return 0
    lax.fori_loop(N_EXPERTS, start_ref.shape[0], rest, 0)
    last_e = blk_e_ref[jnp.maximum(total - 1, 0)]

    def tail(b, _):
        blk_e_ref[b] = last_e
        blk_n_ref[b] = 0
        return 0
    lax.fori_loop(total, nb, tail, 0)


def block_plan(counts, n_assign):
    nb = (n_assign + N_EXPERTS * (MOE_ROWS - 1)) // MOE_ROWS
    smem = pl.BlockSpec(memory_space=pltpu.SMEM)
    return pl.pallas_call(
        _plan_kernel,
        out_shape=[jax.ShapeDtypeStruct((nb,), jnp.int32), jax.ShapeDtypeStruct((nb,), jnp.int32),
                   jax.ShapeDtypeStruct((LANES,), jnp.int32)],
        in_specs=[smem],
        out_specs=[smem, smem, smem],
        name="block_plan",
    )(counts.reshape(LANES))


def _dest_kernel(eid_ref, start_ref, dest_ref, carry_ref):
    @pl.when(pl.program_id(0) == 0)
    def _():
        carry_ref[...] = jnp.zeros_like(carry_ref)

    eid = eid_ref[...]
    tm = eid.shape[0]
    lane = lax.broadcasted_iota(jnp.int32, (tm, LANES), 1)
    oh0 = lane == eid[:, 0:1]
    oh1 = lane == eid[:, 1:2]
    both = jnp.where(oh0 | oh1, 1.0, 0.0).astype(BF16)
    r = lax.broadcasted_iota(jnp.int32, (tm, tm), 0)
    c = lax.broadcasted_iota(jnp.int32, (tm, tm), 1)
    before = _dot(jnp.where(r > c, 1.0, 0.0).astype(BF16), both) + carry_ref[...]
    pos = before + (start_ref[...] * MOE_ROWS).astype(F32)
    d0 = jnp.sum(jnp.where(oh0, pos, 0.0), axis=-1, keepdims=True).astype(jnp.int32)
    d1 = jnp.sum(jnp.where(oh1, pos, 0.0), axis=-1, keepdims=True).astype(jnp.int32)
    dest_ref[...] = jnp.where(lane == 0, d0, jnp.where(lane == 1, d1, 0))
    carry_ref[...] = carry_ref[...] + jnp.sum(both.astype(F32), axis=0, keepdims=True)


def assignment_dest(eid, blk_start, *, tm=512):
    S = eid.shape[0]
    tm = min(tm, S)
    return pl.pallas_call(
        _dest_kernel,
        out_shape=jax.ShapeDtypeStruct((S, LANES), jnp.int32),
        grid=(S // tm,),
        in_specs=[pl.BlockSpec((tm, LANES), lambda i: (i, 0)), pl.BlockSpec((1, LANES), lambda i: (0, 0))],
        out_specs=pl.BlockSpec((tm, LANES), lambda i: (i, 0)),
        scratch_shapes=[pltpu.VMEM((1, LANES), F32)],
        compiler_params=_params("arbitrary"),
        name="assignment_dest",
    )(eid, blk_start.reshape(1, LANES))


def _for_rows(n, fn):
    sh = MOE_DMA_UNROLL.bit_length() - 1

    def group(t, _):
        for u in range(MOE_DMA_UNROLL):
            fn(t * MOE_DMA_UNROLL + u)
        return 0
    lax.fori_loop(0, n >> sh, group, 0)

    def one(r, _):
        fn(r)
        return 0
    lax.fori_loop((n >> sh) << sh, n, one, 0)


def _experts_kernel(blk_e, blk_n, dest, hn_hbm, wg_ref, wu_ref, wd_ref, out_hbm,
                    xg_ref, xb_ref, acc_ref, wgb_ref, wub_ref, wdb_ref, inv_ref, gsem, ssem):
    i = pl.program_id(0)
    j = pl.program_id(1)
    nb = pl.num_programs(0)
    nsplit = pl.num_programs(1)
    n_tok = hn_hbm.shape[0]
    k_shift = TOP_K.bit_length() - 1

    def gather(b, r):
        tok = inv_ref[b * MOE_ROWS + r] >> k_shift
        return pltpu.make_async_copy(hn_hbm.at[pl.ds(tok, 1)], xg_ref.at[pl.ds(r, 1)], gsem)

    def scatter(b, r):
        a = inv_ref[b * MOE_ROWS + r]
        row = (a & (TOP_K - 1)) * n_tok + (a >> k_shift)
        return pltpu.make_async_copy(acc_ref.at[b & 1, pl.ds(r, 1)], out_hbm.at[pl.ds(row, 1)], ssem)

    def n_gather(b):
        return pl.multiple_of(((blk_n[b] + 7) >> 3) << 3, 8)

    def start_gather(b):
        def pad(r, _):
            inv_ref[b * MOE_ROWS + r] = 0
            return 0
        lax.fori_loop(blk_n[b], n_gather(b), pad, 0)
        _for_rows(n_gather(b), lambda r: gather(b, r).start(priority=1))

    def wait_gather(b):
        n = n_gather(b)

        @pl.when(n > 0)
        def _():
            pltpu.make_async_copy(hn_hbm.at[pl.ds(0, n)], xg_ref.at[pl.ds(0, n)], gsem).wait()

    def start_scatter(b):
        _for_rows(blk_n[b], lambda r: scatter(b, r).start(priority=1))

    def wait_scatter(b):
        n8 = pl.multiple_of((blk_n[b] >> 3) << 3, 8)

        @pl.when(n8 > 0)
        def _():
            pltpu.make_async_copy(acc_ref.at[b & 1, pl.ds(0, n8)], out_hbm.at[pl.ds(0, n8)], ssem).wait()

        def one(r, _):
            scatter(b, r).wait()
            return 0
        lax.fori_loop(n8, blk_n[b], one, 0)

    n_tail_units = (blk_n[i] + (MOE_TAIL - 1)) >> (MOE_TAIL.bit_length() - 1)
    slot = i & 1

    @pl.when(j == 0)
    def _():
        @pl.when(i == 0)
        def _():
            def invert(a):
                inv_ref[dest[a]] = a
            _for_rows(dest.shape[0], invert)
            xg_ref[...] = jnp.zeros_like(xg_ref)
            start_gather(0)
        wait_gather(i)

        def cast(sb, _):
            rs = pl.ds(pl.multiple_of(sb * MOE_TAIL, MOE_TAIL), MOE_TAIL)
            xb_ref[rs, :] = xg_ref[rs, :].astype(BF16)
            return 0
        lax.fori_loop(0, n_tail_units, cast, 0)

        @pl.when(i + 1 < nb)
        def _():
            start_gather(i + 1)

    @pl.when(n_tail_units > 0)
    def _():
        wgb_ref[...] = wg_ref[0].astype(BF16)
        wub_ref[...] = wu_ref[0].astype(BF16)
        wdb_ref[...] = wd_ref[0].astype(BF16)

    def compute(row0, nrows):
        rs = pl.ds(row0, nrows)
        xb = xb_ref[rs, :]
        gate = _dot(xb, wgb_ref[...])
        hid = (gate * _sigmoid(gate)) * _dot(xb, wub_ref[...])
        part = _dot(hid.astype(BF16), wdb_ref[...])

        @pl.when(j == 0)
        def _():
            acc_ref[slot, rs, :] = part

        @pl.when(j > 0)
        def _():
            acc_ref[slot, rs, :] = acc_ref[slot, rs, :] + part

    def sub_block(sb, _):
        compute(pl.multiple_of(sb * MOE_SUB, MOE_SUB), MOE_SUB)
        return 0
    n_full = n_tail_units >> 1
    lax.fori_loop(0, n_full, sub_block, 0)

    @pl.when((n_tail_units & 1) == 1)
    def _():
        compute(pl.multiple_of(n_full * MOE_SUB, MOE_TAIL), MOE_TAIL)

    @pl.when(j == nsplit - 1)
    def _():
        @pl.when(i > 0)
        def _():
            wait_scatter(i - 1)
        start_scatter(i)

        @pl.when(i == nb - 1)
        def _():
            wait_scatter(i)


def experts(hn, blk_e, blk_n, dest, w_gate, w_up, w_down):
    S, D = hn.shape
    E, _, DE = w_gate.shape
    nb = blk_e.shape[0]
    de = DE // MOE_SPLIT
    split = lambda i, j, bn: jnp.where(bn[i] > 0, j, MOE_SPLIT - 1)
    grid_spec = pltpu.PrefetchScalarGridSpec(
        num_scalar_prefetch=3,
        grid=(nb, MOE_SPLIT),
        in_specs=[pl.BlockSpec(memory_space=pl.ANY),
                  pl.BlockSpec((1, D, de), lambda i, j, be, bn, ds: (be[i], 0, split(i, j, bn))),
                  pl.BlockSpec((1, D, de), lambda i, j, be, bn, ds: (be[i], 0, split(i, j, bn))),
                  pl.BlockSpec((1, de, D), lambda i, j, be, bn, ds: (be[i], split(i, j, bn), 0))],
        out_specs=pl.BlockSpec(memory_space=pl.ANY),
        scratch_shapes=[pltpu.VMEM((MOE_ROWS, D), F32),
                        pltpu.VMEM((MOE_ROWS, D), BF16),
                        pltpu.VMEM((2, MOE_ROWS, D), F32),
                        pltpu.VMEM((D, de), BF16),
                        pltpu.VMEM((D, de), BF16),
                        pltpu.VMEM((de, D), BF16),
                        pltpu.SMEM((nb * MOE_ROWS,), jnp.int32),
                        pltpu.SemaphoreType.DMA(()),
                        pltpu.SemaphoreType.DMA(())],
    )
    return pl.pallas_call(
        _experts_kernel,
        out_shape=jax.ShapeDtypeStruct((TOP_K * S, D), F32),
        grid_spec=grid_spec,
        compiler_params=_params("arbitrary", "arbitrary"),
        name="experts",
    )(blk_e, blk_n, dest, hn, w_gate, w_up, w_down)


def _combine_kernel(*refs, has_norm):
    h_ref, y0_ref, y1_ref, w_ref = refs[:4]
    o_ref = refs[-1]
    w = w_ref[...]
    x = h_ref[...] + (y0_ref[0] * w[:, 0:1] + y1_ref[0] * w[:, 1:2])
    if has_norm:
        x = x * lax.rsqrt(jnp.mean(x * x, axis=-1, keepdims=True) + NORM_EPS) * refs[4][...]
    o_ref[...] = x


def combine(h, y_slots, slot_w, gain=None, *, tm=512):
    S, D = h.shape
    tm = min(tm, S)
    y3 = y_slots.reshape(TOP_K, S, D)
    in_specs = [pl.BlockSpec((tm, D), lambda i: (i, 0)),
                pl.BlockSpec((1, tm, D), lambda i: (0, i, 0)),
                pl.BlockSpec((1, tm, D), lambda i: (1, i, 0)),
                pl.BlockSpec((tm, LANES), lambda i: (i, 0))]
    args = [h, y3, y3, slot_w]
    if gain is not None:
        in_specs.append(pl.BlockSpec((1, D), lambda i: (0, 0)))
        args.append(gain.reshape(1, D).astype(F32))
    return pl.pallas_call(
        functools.partial(_combine_kernel, has_norm=gain is not None),
        out_shape=jax.ShapeDtypeStruct((S, D), F32),
        grid=(S // tm,),
        in_specs=in_specs,
        out_specs=pl.BlockSpec((tm, D), lambda i: (i, 0)),
        compiler_params=_params("parallel"),
        name="combine",
    )(*args)


def moe(h, gain, wg, bg, we, be, w_gate, w_up, w_down, final_gain=None):
    S = h.shape[0]
    hn, eid, slot_w, counts = router(h, gain, wg, bg, we, be)
    blk_e, blk_n, blk_start = block_plan(counts, S * TOP_K)
    dest = assignment_dest(eid, blk_start)[:, :TOP_K].reshape(S * TOP_K)
    y_slots = experts(hn, blk_e, blk_n, dest, w_gate, w_up, w_down)
    return combine(h, y_slots, slot_w, final_gain)


def kernel(x, mem, positions, mix_norm, w_in, shift_mu, decay_w0, decay_w2, aaa_a0, aaa_a2, gate_g2, k_k, k_a,
           r_k, lnx_w, lnx_b, diff_lambda, subln_w, w_out, mem_q_norm, mem_kv_norm, wq_mem, wk_mem, wv_mem,
           wo_mem, moe_norm, router_group_w, router_group_b, router_expert_w, router_expert_b, expert_gate,
           expert_up, expert_down, final_norm):
    B, S, D = x.shape
    depth = w_in.shape[0]
    outs = []
    for b in range(B):
        h = x[b]
        memb = mem[b]
        for l in range(depth):
            lambda_init = 0.8 - 0.6 * math.exp(-0.3 * l)
            w_in_b = w_in[l].astype(BF16)
            proj_r = matmul(h, w_in_b[:, :RWKV_COLS], gain=mix_norm[l], tm=1024, tn=RWKV_COLS // 2)
            qkv = diff_proj(h, mix_norm[l], w_in_b[:, RWKV_COLS:], positions[b])
            pre = rwkv_prep(proj_r, shift_mu[l], decay_w0[l], decay_w2[l], aaa_a0[l], aaa_a2[l], gate_g2[l],
                            k_k[l], k_a[l], r_k[l])
            y_rwkv = rwkv_scan(*pre, lnx_w[l], lnx_b[l])
            y_diff = diff_flash(qkv, diff_lambda[l], subln_w[l], lambda_init)
            h = matmul([y_rwkv, y_diff], w_out[l].astype(BF16), residual=h, tm=1024)
            km = matmul(memb, wk_mem[l].astype(BF16), gain=mem_kv_norm[l], out_dtype=BF16)
            vm = matmul(memb, wv_mem[l].astype(BF16), gain=mem_kv_norm[l], out_dtype=BF16)
            o = mem_attn(h, mem_q_norm[l], wq_mem[l].astype(BF16), km, vm)
            h = matmul(o, wo_mem[l].astype(BF16), residual=h, tm=1024)
            h = moe(h, moe_norm[l], router_group_w[l], router_group_b[l], router_expert_w[l], router_expert_b[l],
                    expert_gate[l], expert_up[l], expert_down[l], final_norm if l == depth - 1 else None)
        outs.append(h)
    return jnp.stack(outs, axis=0)
```

```python
import functools
import math

import numpy as np
import jax
import jax.numpy as jnp
from jax import lax
from jax.experimental import pallas as pl
from jax.experimental.pallas import tpu as pltpu

F32 = jnp.float32
BF16 = jnp.bfloat16
HIGHEST = lax.Precision.HIGHEST

RWKV_HEADS = 16
HEAD_N = 64
RWKV_WIDTH = RWKV_HEADS * HEAD_N
DECAY_LORA = 64
AAA_LORA = 64
GATE_LORA = 128
RWKV_COLS = 3 * RWKV_WIDTH + DECAY_LORA + AAA_LORA + GATE_LORA
DIFF_HEADS = 8
DIFF_QK = 64
DIFF_V = 128
DIFF_WIDTH = DIFF_HEADS * DIFF_V
ROPE_THETA = 10000.0
MEM_HEADS = 4
N_GROUPS = 8
EXPERTS_PER_GROUP = 8
N_EXPERTS = N_GROUPS * EXPERTS_PER_GROUP
TOP_K = 2
NORM_EPS = 1e-6
LNX_EPS = 64e-5
SUBLN_EPS = 1e-5

LANES = 128
MXU_WIDTH = 256
VMEM_LIMIT = 56 * 1024 * 1024

CHUNK = 64
SCAN_ROWS = 256
SCAN_WIDTH = 512
MOE_ROWS = 512
MOE_SUB = 256
MOE_TAIL = 128
MOE_SPLIT = 2
MOE_DMA_UNROLL = 8


def _params(*sem):
    return pltpu.CompilerParams(dimension_semantics=sem, vmem_limit_bytes=VMEM_LIMIT)


def _mm_kernel(*refs, n_x, has_norm, has_res):
    it = iter(refs)
    x_refs = [next(it) for _ in range(n_x)]
    g_ref = next(it) if has_norm else None
    w_ref = next(it)
    r_ref = next(it) if has_res else None
    o_ref = next(it)
    xs_ref = next(it)

    @pl.when(pl.program_id(1) == 0)
    def _():
        if has_norm:
            x = x_refs[0][...].astype(F32)
            x = x * lax.rsqrt(jnp.mean(x * x, axis=-1, keepdims=True) + NORM_EPS) * g_ref[...]
            xs_ref[...] = x.astype(BF16)
        else:
            k0 = 0
            for x_ref in x_refs:
                xs_ref[:, k0:k0 + x_ref.shape[1]] = x_ref[...].astype(BF16)
                k0 += x_ref.shape[1]

    acc = jnp.dot(xs_ref[...], w_ref[...], preferred_element_type=F32)
    if has_res:
        acc = acc + r_ref[...]
    o_ref[...] = acc.astype(o_ref.dtype)


def matmul(x, w, *, gain=None, residual=None, out_dtype=F32, tm=512, tn=512):
    xs = list(x) if isinstance(x, (list, tuple)) else [x]
    assert gain is None or len(xs) == 1
    M = xs[0].shape[0]
    K, N = w.shape
    assert sum(p.shape[1] for p in xs) == K
    tm = min(tm, M)
    assert M % tm == 0 and N % tn == 0, (M, N, tm, tn)
    in_specs = [pl.BlockSpec((tm, p.shape[1]), lambda i, j: (i, 0)) for p in xs]
    args = list(xs)
    if gain is not None:
        in_specs.append(pl.BlockSpec((1, K), lambda i, j: (0, 0)))
        args.append(gain.reshape(1, K).astype(F32))
    in_specs.append(pl.BlockSpec((K, tn), lambda i, j: (0, j)))
    args.append(w)
    if residual is not None:
        in_specs.append(pl.BlockSpec((tm, tn), lambda i, j: (i, j)))
        args.append(residual)
    return pl.pallas_call(
        functools.partial(_mm_kernel, n_x=len(xs), has_norm=gain is not None, has_res=residual is not None),
        out_shape=jax.ShapeDtypeStruct((M, N), out_dtype),
        grid=(M // tm, N // tn),
        in_specs=in_specs,
        out_specs=pl.BlockSpec((tm, tn), lambda i, j: (i, j)),
        scratch_shapes=[pltpu.VMEM((tm, K), BF16)],
        compiler_params=_params("parallel", "arbitrary"),
        name="matmul",
    )(*args)


def _sigmoid(x):
    return 1.0 / (1.0 + jnp.exp(-x))


def _rwkv_prep_kernel(p_ref, pp_ref, mu_ref, w0_ref, w2_ref, a0_ref, a2_ref, g2_ref, kk_ref, ka_ref, rk_ref,
                      hsum_ref, at_o, bt_o, kt_o, rt_o, v_o, g_o, bonus_o, pc_o):
    W = RWKV_WIDTH
    C = CHUNK
    p = p_ref[...]
    last = jnp.where(pl.program_id(0) == 0, 0.0, pp_ref[7:8, :])
    prev = pltpu.roll(p, 1, axis=0)
    row = lax.broadcasted_iota(jnp.int32, p.shape, 0)
    prev = jnp.where(row == 0, last, prev)
    ps = p + (prev - p) * mu_ref[...]
    r = ps[:, 0:W]
    k = ps[:, W:2 * W]
    v = ps[:, 2 * W:3 * W]
    o = 3 * W
    wd = ps[:, o:o + DECAY_LORA]
    ad = ps[:, o + DECAY_LORA:o + DECAY_LORA + AAA_LORA]
    gd = ps[:, o + DECAY_LORA + AAA_LORA:o + DECAY_LORA + AAA_LORA + GATE_LORA]
    z = w0_ref[...] + jnp.dot(jnp.tanh(wd).astype(BF16), w2_ref[...], preferred_element_type=F32)
    nz = -z
    softplus = jnp.maximum(nz, 0.0) + jnp.log(1.0 + jnp.exp(-jnp.abs(nz)))
    w_raw = -softplus - 0.5
    lw = -jnp.exp(w_raw)
    tm = p.shape[0]
    rr = lax.broadcasted_iota(jnp.int32, (tm, tm), 0)
    cc = lax.broadcasted_iota(jnp.int32, (tm, tm), 1)
    sh = C.bit_length() - 1
    tri = (((rr >> sh) == (cc >> sh)) & (rr >= cc)).astype(F32)
    cs = _dot(tri, lw, HIGHEST)
    a = _sigmoid(a0_ref[...] + jnp.dot(ad.astype(BF16), a2_ref[...], preferred_element_type=F32))
    g_o[...] = jnp.dot(_sigmoid(gd).astype(BF16), g2_ref[...], preferred_element_type=F32)

    def head_sum(x):
        hi = x.astype(BF16)
        lo = (x - hi.astype(F32)).astype(BF16)
        ones = hsum_ref[...]
        wb = ones.shape[0]
        cols = [slice(c, c + wb) for c in range(0, x.shape[1], wb)]
        return jnp.concatenate([_dot(hi[:, cs], ones) + _dot(lo[:, cs], ones) for cs in cols], axis=1)

    kk = k * kk_ref[...]
    kk = kk / jnp.maximum(jnp.sqrt(head_sum(kk * kk)), 1e-12)
    k2 = k * (1.0 + (a - 1.0) * ka_ref[...])
    e_neg = jnp.exp(-cs)
    at_o[...] = (-kk * jnp.exp(cs - lw)).astype(at_o.dtype)
    bt_o[...] = (kk * a) * e_neg
    kt_o[...] = k2 * e_neg
    rt_o[...] = r * jnp.exp(cs)
    v_o[...] = v
    bonus_o[...] = head_sum(r * k2 * rk_ref[...]) * v
    for q in range(tm // C):
        pc_o[q] = jnp.exp(cs[q * C + C - 1:q * C + C, :])


def rwkv_prep(p, mu, w0, w2, a0, a2, g2, k_k, k_a, r_k, *, tm=256):
    S = p.shape[0]
    tm = min(tm, S)
    assert tm % CHUNK == 0
    W = RWKV_WIDTH
    row = lambda x: x.reshape(1, -1).astype(F32)
    full = lambda a: pl.BlockSpec(a.shape, lambda i: (0, 0))
    head = jnp.arange(2 * LANES, dtype=jnp.int32) // HEAD_N
    hsum = (head[:, None] == head[None, :]).astype(BF16)
    args = [p, p, row(mu), row(w0), w2.astype(BF16), row(a0), a2.astype(BF16), g2.astype(BF16),
            row(k_k), row(k_a), row(r_k), hsum]
    in_specs = [pl.BlockSpec((tm, RWKV_COLS), lambda i: (i, 0)),
                pl.BlockSpec((8, RWKV_COLS), lambda i: (jnp.maximum(i * (tm // 8) - 1, 0), 0))]
    in_specs += [full(a) for a in args[2:]]
    f32 = jax.ShapeDtypeStruct((S, W), F32)
    blk = pl.BlockSpec((tm, W), lambda i: (i, 0))
    return pl.pallas_call(
        _rwkv_prep_kernel,
        out_shape=[jax.ShapeDtypeStruct((S, W), BF16)] + [f32] * 6 + [jax.ShapeDtypeStruct((S // CHUNK, 1, W), F32)],
        grid=(S // tm,),
        in_specs=in_specs,
        out_specs=[blk] * 7 + [pl.BlockSpec((tm // CHUNK, 1, W), lambda i: (i, 0, 0))],
        compiler_params=_params("parallel"),
        name="rwkv_prep",
    )(*args)


def _dot_nt(a, b, precision=None):
    return lax.dot_general(a, b, (((1,), (1,)), ((), ())), preferred_element_type=F32, precision=precision)


def _dot_tn(a, b, precision=None):
    return lax.dot_general(a, b, (((0,), (0,)), ((), ())), preferred_element_type=F32, precision=precision)


def _dot(a, b, precision=None):
    return jnp.dot(a, b, preferred_element_type=F32, precision=precision)


def _rwkv_scan_kernel(at_ref, bt_ref, kt_ref, rt_ref, v_ref, g_ref, bonus_ref, pc_ref, lnw_ref, lnb_ref,
                      y_ref, state_ref, yf_ref):
    C = CHUNK
    N = HEAD_N

    @pl.when(pl.program_id(1) == 0)
    def _():
        state_ref[...] = jnp.zeros_like(state_ref)

    row = lax.broadcasted_iota(jnp.int32, (C, C), 0)
    col = lax.broadcasted_iota(jnp.int32, (C, C), 1)
    eye = (row == col).astype(F32)
    strict = row > col
    incl = row >= col

    n_chunks = rt_ref.shape[0] // C
    heads = rt_ref.shape[1] // N
    items = [(j, q) for j in range(heads) for q in range(n_chunks)]
    G = range(len(items))

    def tile(ref, j, q):
        return ref[q * C:(q + 1) * C, j * N:(j + 1) * N]

    def par(ref, j):
        return ref[:, j * N:(j + 1) * N]

    at = [tile(at_ref, j, q) for j, q in items]
    bt = [tile(bt_ref, j, q) for j, q in items]
    kt = [tile(kt_ref, j, q) for j, q in items]
    rt = [tile(rt_ref, j, q) for j, q in items]
    v = [tile(v_ref, j, q) for j, q in items]
    pc = [pc_ref[q, :, j * N:(j + 1) * N] for j, q in items]
    vb = [x.astype(BF16) for x in v]
    btb = [x.astype(BF16) for x in bt]
    ktb = [x.astype(BF16) for x in kt]
    rtb = [x.astype(BF16) for x in rt]
    n_ab = [jnp.where(strict, _dot_nt(at[g], btb[g]), 0.0) for g in G]
    a_ak = [jnp.where(strict, _dot_nt(at[g], ktb[g]), 0.0).astype(BF16) for g in G]
    a_rb = [jnp.where(incl, _dot_nt(rtb[g], btb[g]), 0.0).astype(BF16) for g in G]
    a_rk = [jnp.where(incl, _dot_nt(rtb[g], ktb[g]), 0.0).astype(BF16) for g in G]
    akv = [_dot(a_ak[g], vb[g]) for g in G]
    t = None
    b = 1
    while b < C:
        sh = (2 * b).bit_length() - 1
        low_left = ((row >> sh) == (col >> sh)) & ((row & b) != 0) & ((col & b) == 0)
        nb = [jnp.where(low_left, n_ab[g], 0.0) for g in G]
        if b == 1:
            t = [eye + nb[g] for g in G]
        else:
            tb = [t[g].astype(BF16) for g in G]
            z = [_dot(nb[g].astype(BF16), tb[g]).astype(BF16) for g in G]
            t = [t[g] + _dot(tb[g], z[g]) for g in G]
        b *= 2
    tb = [t[g].astype(BF16) for g in G]
    wm = [_dot(tb[g], at[g]).astype(BF16) for g in G]
    u0 = [_dot(tb[g], akv[g].astype(BF16)).astype(BF16) for g in G]
    rm = [(rt[g] + _dot(a_rb[g], wm[g])).astype(BF16) for g in G]
    y0 = [_dot(a_rb[g], u0[g]) + _dot(a_rk[g], vb[g]) for g in G]
    bp = [(bt[g] * pc[g]).astype(BF16) for g in G]
    kp = [(kt[g] * pc[g]).astype(BF16) for g in G]
    mp = [_dot_tn(wm[g], bp[g]).astype(BF16) for g in G]
    s_add = [_dot_tn(u0[g], bp[g]) + _dot_tn(vb[g], kp[g]) for g in G]

    for j in range(heads):
        s = state_ref[j]
        for q in range(n_chunks):
            g = j * n_chunks + q
            sb = s.astype(BF16)
            y = _dot_nt(rm[g], sb) + y0[g]
            s = s * pc[g] + _dot(sb, mp[g]) + s_add[g]
            mean = jnp.mean(y, axis=-1, keepdims=True)
            yc = y - mean
            var = jnp.mean(yc * yc, axis=-1, keepdims=True)
            yn = yc * lax.rsqrt(var + LNX_EPS) * par(lnw_ref, j) + par(lnb_ref, j)
            yf_ref[q * C:(q + 1) * C, j * N:(j + 1) * N] = (yn + tile(bonus_ref, j, q)) * tile(g_ref, j, q)
        state_ref[j] = s
    y_ref[...] = yf_ref[...].astype(y_ref.dtype)


def rwkv_scan(at, bt, kt, rt, v, g, bonus, pc, lnx_w, lnx_b, *, rows=SCAN_ROWS, width=SCAN_WIDTH):
    S, W = rt.shape
    rows = min(rows, S)
    hp = width // HEAD_N
    row = lambda x: x.reshape(1, W).astype(F32)
    seq = pl.BlockSpec((rows, width), lambda h, c: (c, h))
    par = pl.BlockSpec((1, width), lambda h, c: (0, h))
    return pl.pallas_call(
        _rwkv_scan_kernel,
        out_shape=jax.ShapeDtypeStruct((S, W), BF16),
        grid=(W // width, S // rows),
        in_specs=[seq] * 7 + [pl.BlockSpec((rows // CHUNK, 1, width), lambda h, c: (c, 0, h))] + [par] * 2,
        out_specs=seq,
        scratch_shapes=[pltpu.VMEM((hp, HEAD_N, HEAD_N), F32), pltpu.VMEM((rows, width), F32)],
        compiler_params=_params("parallel", "arbitrary"),
        name="rwkv_scan",
    )(at, bt, kt, rt, v, g, bonus, pc, row(lnx_w), row(lnx_b))


def _diff_proj_kernel(x_ref, g_ref, w_ref, pos_ref, freq_ref, o_ref, xs_ref, cos_ref, sin_ref):
    j = pl.program_id(1)
    half = DIFF_QK // 2

    def first_half(shape):
        return (lax.broadcasted_iota(jnp.int32, shape, 1) & (DIFF_QK - 1)) < half

    @pl.when(j == 0)
    def _():
        x = x_ref[...]
        x = x * lax.rsqrt(jnp.mean(x * x, axis=-1, keepdims=True) + NORM_EPS) * g_ref[...]
        xs_ref[...] = x.astype(BF16)
        ang = pos_ref[...] * freq_ref[...]
        cos_ref[...] = jnp.cos(ang)
        sin_ref[...] = jnp.where(first_half(ang.shape), -1.0, 1.0) * jnp.sin(ang)

    @pl.when(j < 2)
    def _():
        sc = jnp.where(j == 0, DIFF_QK ** -0.5, 1.0)
        cos = cos_ref[...]
        sin = sin_ref[...]
        fh = first_half(cos.shape)
        for c0 in range(0, o_ref.shape[1], MXU_WIDTH):
            acc = jnp.dot(xs_ref[...], w_ref[:, c0:c0 + MXU_WIDTH], preferred_element_type=F32)
            for b0 in range(0, MXU_WIDTH, LANES):
                x = acc[:, b0:b0 + LANES]
                partner = jnp.where(fh, pltpu.roll(x, LANES - half, axis=1), pltpu.roll(x, half, axis=1))
                o_ref[:, c0 + b0:c0 + b0 + LANES] = ((x * cos + partner * sin) * sc).astype(o_ref.dtype)

    @pl.when(j == 2)
    def _():
        o_ref[...] = jnp.dot(xs_ref[...], w_ref[...], preferred_element_type=F32).astype(o_ref.dtype)


def diff_proj(x, gain, w, positions, *, tm=1024):
    S, K = x.shape
    W = DIFF_WIDTH
    tm = min(tm, S)
    assert w.shape == (K, 3 * W) and S % tm == 0
    inv_freq = ROPE_THETA ** (-(jnp.arange(0, DIFF_QK, 2, dtype=F32) / DIFF_QK))
    freq = jnp.tile(inv_freq, LANES // (DIFF_QK // 2)).reshape(1, LANES)
    pos = positions.reshape(S, 1).astype(F32)
    return pl.pallas_call(
        _diff_proj_kernel,
        out_shape=jax.ShapeDtypeStruct((S, 3 * W), BF16),
        grid=(S // tm, 3),
        in_specs=[pl.BlockSpec((tm, K), lambda i, j: (i, 0)),
                  pl.BlockSpec((1, K), lambda i, j: (0, 0)),
                  pl.BlockSpec((K, W), lambda i, j: (0, j)),
                  pl.BlockSpec((tm, 1), lambda i, j: (i, 0)),
                  pl.BlockSpec((1, LANES), lambda i, j: (0, 0))],
        out_specs=pl.BlockSpec((tm, W), lambda i, j: (i, j)),
        scratch_shapes=[pltpu.VMEM((tm, K), BF16), pltpu.VMEM((tm, LANES), F32), pltpu.VMEM((tm, LANES), F32)],
        compiler_params=_params("parallel", "arbitrary"),
        name="diff_proj",
    )(x, gain.reshape(1, K).astype(F32), w, pos, freq)


def _diff_flash_kernel(qi_tab, ki_tab, q_ref, k_ref, v_ref, lam_ref, sw_ref, o_ref,
                       m_ref, l_ref, acc_ref, s_ref, p_ref, alpha_ref, *, tile, row_block, diag_parts, lambda_init):
    p = pl.program_id(1)
    qi = qi_tab[p]
    ki = ki_tab[p]

    @pl.when(ki == 0)
    def _():
        m_ref[...] = jnp.full_like(m_ref, -jnp.inf)
        l_ref[...] = jnp.zeros_like(l_ref)
        acc_ref[...] = jnp.zeros_like(acc_ref)

    def process(row0, nrows, kmax, masked):
        rows = slice(row0, row0 + nrows)
        q = q_ref[rows, :]
        k = k_ref[0:kmax, :]
        v = v_ref[0:kmax, :]
        lane = lax.broadcasted_iota(jnp.int32, q.shape, 1)
        reps = kmax // LANES
        for c in range(2):
            qc = jnp.where((lane >> (DIFF_QK.bit_length() - 1)) == c, q, jnp.zeros_like(q))
            s_ref[0:nrows, 0:kmax] = _dot_nt(qc, k)
            for r0 in range(0, nrows, row_block):
                ar = slice(row0 + r0, row0 + r0 + row_block)
                lr = slice(r0, r0 + row_block)
                s = s_ref[lr, 0:kmax]
                if masked:
                    qpos = lax.broadcasted_iota(jnp.int32, s.shape, 0) + (row0 + r0)
                    kpos = lax.broadcasted_iota(jnp.int32, s.shape, 1)
                    s = jnp.where(kpos <= qpos, s, -jnp.inf)
                m_old = m_ref[c, ar, :]
                m_new = jnp.maximum(m_old, jnp.max(s, axis=-1, keepdims=True))
                alpha = jnp.exp(m_old - m_new)
                pr = jnp.exp(s - jnp.concatenate([m_new] * reps, axis=1))
                l_ref[c, ar, :] = alpha * l_ref[c, ar, :] + jnp.sum(pr, axis=-1, keepdims=True)
                m_ref[c, ar, :] = m_new
                alpha_ref[lr, :] = alpha
                p_ref[lr, 0:kmax] = pr.astype(BF16)
            acc_ref[c, rows, :] = (alpha_ref[0:nrows, :] * acc_ref[c, rows, :]
                                   + _dot(p_ref[0:nrows, 0:kmax], v))

    @pl.when(ki < qi)
    def _():
        process(0, tile, tile, False)

    @pl.when(ki == qi)
    def _():
        part = tile // diag_parts
        for a in range(diag_parts):
            process(a * part, part, (a + 1) * part, True)
        lp = lam_ref[...]
        lam = (jnp.exp(jnp.sum(lp[0:1] * lp[1:2], axis=-1, keepdims=True))
               - jnp.exp(jnp.sum(lp[2:3] * lp[3:4], axis=-1, keepdims=True)) + lambda_init)
        o = acc_ref[0] / l_ref[0] - lam * (acc_ref[1] / l_ref[1])
        o = o * lax.rsqrt(jnp.mean(o * o, axis=-1, keepdims=True) + SUBLN_EPS)
        o_ref[...] = (o * sw_ref[...] * (1.0 - lambda_init)).astype(o_ref.dtype)


def diff_flash(qkv, lambdas, subln_w, lambda_init, *, tile=1024, row_block=32, diag_parts=2):
    S = qkv.shape[0]
    tile = min(tile, S)
    assert S % tile == 0 and (tile // diag_parts) % LANES == 0 and (tile // diag_parts) % row_block == 0
    pairs = [(qi, ki) for qi in range(S // tile) for ki in range(qi + 1)]
    qi_tab = jnp.asarray([p[0] for p in pairs], jnp.int32)
    ki_tab = jnp.asarray([p[1] for p in pairs], jnp.int32)
    grid_spec = pltpu.PrefetchScalarGridSpec(
        num_scalar_prefetch=2,
        grid=(DIFF_HEADS, len(pairs)),
        in_specs=[pl.BlockSpec((tile, LANES), lambda h, p, qt, kt: (qt[p], h)),
                  pl.BlockSpec((tile, LANES), lambda h, p, qt, kt: (kt[p], DIFF_HEADS + h)),
                  pl.BlockSpec((tile, LANES), lambda h, p, qt, kt: (kt[p], 2 * DIFF_HEADS + h)),
                  pl.BlockSpec((4, DIFF_QK), lambda h, p, qt, kt: (0, 0)),
                  pl.BlockSpec((1, DIFF_V), lambda h, p, qt, kt: (0, 0))],
        out_specs=pl.BlockSpec((tile, LANES), lambda h, p, qt, kt: (qt[p], h)),
        scratch_shapes=[pltpu.VMEM((2, tile, LANES), F32),
                        pltpu.VMEM((2, tile, LANES), F32),
                        pltpu.VMEM((2, tile, DIFF_V), F32),
                        pltpu.VMEM((tile, tile), F32),
                        pltpu.VMEM((tile, tile), BF16),
                        pltpu.VMEM((tile, LANES), F32)],
    )
    return pl.pallas_call(
        functools.partial(_diff_flash_kernel, tile=tile, row_block=row_block, diag_parts=diag_parts,
                          lambda_init=lambda_init),
        out_shape=jax.ShapeDtypeStruct((S, DIFF_WIDTH), BF16),
        grid_spec=grid_spec,
        compiler_params=_params("parallel", "arbitrary"),
        name="diff_flash",
    )(qi_tab, ki_tab, qkv, qkv, qkv, lambdas.astype(F32), subln_w.reshape(1, DIFF_V).astype(F32))


def _mem_attn_kernel(x_ref, g_ref, w_ref, k_ref, v_ref, o_ref, xs_ref):
    @pl.when(pl.program_id(1) == 0)
    def _():
        x = x_ref[...]
        x = x * lax.rsqrt(jnp.mean(x * x, axis=-1, keepdims=True) + NORM_EPS) * g_ref[...]
        xs_ref[...] = x.astype(BF16)

    q = jnp.dot(xs_ref[...], w_ref[...], preferred_element_type=F32).astype(BF16)
    s = _dot_nt(q, k_ref[...]) * (q.shape[1] ** -0.5)
    s = s - jnp.max(s, axis=-1, keepdims=True)
    e = jnp.exp(s)
    pr = e / jnp.sum(e, axis=-1, keepdims=True)
    o_ref[...] = _dot(pr.astype(BF16), v_ref[...]).astype(o_ref.dtype)


def mem_attn(x, gain, wq, k, v, *, tm=1024):
    S, D = x.shape
    M = k.shape[0]
    hd = D // MEM_HEADS
    tm = min(tm, S)
    return pl.pallas_call(
        _mem_attn_kernel,
        out_shape=jax.ShapeDtypeStruct((S, D), BF16),
        grid=(S // tm, MEM_HEADS),
        in_specs=[pl.BlockSpec((tm, D), lambda i, j: (i, 0)),
                  pl.BlockSpec((1, D), lambda i, j: (0, 0)),
                  pl.BlockSpec((D, hd), lambda i, j: (0, j)),
                  pl.BlockSpec((M, hd), lambda i, j: (0, j)),
                  pl.BlockSpec((M, hd), lambda i, j: (0, j))],
        out_specs=pl.BlockSpec((tm, hd), lambda i, j: (i, j)),
        scratch_shapes=[pltpu.VMEM((tm, D), BF16)],
        compiler_params=_params("parallel", "arbitrary"),
        name="mem_attn",
    )(x, gain.reshape(1, D).astype(F32), wq, k, v)


def _first_argmax(x, lane, big):
    m = jnp.max(x, axis=-1, keepdims=True)
    idx = jnp.min(jnp.where(x == m, lane, big), axis=-1, keepdims=True)
    return m, idx


def _router_kernel(h_ref, gain_ref, wg_ref, bg_ref, we_ref, be_ref, hn_ref, eid_ref, ew_ref, cnt_ref):
    @pl.when(pl.program_id(0) == 0)
    def _():
        cnt_ref[...] = jnp.zeros_like(cnt_ref)

    x = h_ref[...]
    hn = x * lax.rsqrt(jnp.mean(x * x, axis=-1, keepdims=True) + NORM_EPS) * gain_ref[...]
    hn_ref[...] = hn
    hn_hi = hn.astype(BF16)
    hn_lo = (hn - hn_hi.astype(F32)).astype(BF16)

    def logits(w_ref, b_ref):
        w = w_ref[...]
        w_hi = w.astype(BF16)
        w_lo = (w - w_hi.astype(F32)).astype(BF16)
        return _dot(hn_hi, w_hi) + (_dot(hn_hi, w_lo) + _dot(hn_lo, w_hi)) + b_ref[...]
    g_logits = logits(wg_ref, bg_ref)
    e_logits = logits(we_ref, be_ref)
    tm = x.shape[0]
    lane_g = lax.broadcasted_iota(jnp.int32, (tm, N_GROUPS), 1)
    g_max, g_idx = _first_argmax(g_logits, lane_g, N_GROUPS)
    g_w = 1.0 / jnp.sum(jnp.exp(g_logits - g_max), axis=-1, keepdims=True)
    lane_e = lax.broadcasted_iota(jnp.int32, (tm, N_EXPERTS), 1)
    in_group = (lane_e // EXPERTS_PER_GROUP) == g_idx
    el = jnp.where(in_group, e_logits, -jnp.inf)
    e_max = jnp.max(el, axis=-1, keepdims=True)
    ex = jnp.exp(el - e_max)
    prob = ex / jnp.sum(ex, axis=-1, keepdims=True)
    prob = jnp.where(in_group, prob, -1.0)
    p1, i1 = _first_argmax(prob, lane_e, N_EXPERTS)
    p2, i2 = _first_argmax(jnp.where(lane_e == i1, -1.0, prob), lane_e, N_EXPERTS)
    tot = p1 + p2
    lane_o = lax.broadcasted_iota(jnp.int32, (tm, LANES), 1)
    eid_ref[...] = jnp.where(lane_o == 0, i1, jnp.where(lane_o == 1, i2, 0))
    ew_ref[...] = jnp.where(lane_o == 0, g_w * (p1 / tot), jnp.where(lane_o == 1, g_w * (p2 / tot), 0.0))
    chosen = ((lane_o == i1) | (lane_o == i2)).astype(F32)
    cnt_ref[...] = cnt_ref[...] + jnp.sum(chosen, axis=0, keepdims=True).astype(jnp.int32)


def router(h, gain, wg, bg, we, be, *, tm=512):
    S, D = h.shape
    tm = min(tm, S)
    full = lambda a: pl.BlockSpec(a.shape, lambda i: (0, 0))
    args = [h, gain.reshape(1, D).astype(F32), wg.astype(F32), bg.reshape(1, -1).astype(F32),
            we.astype(F32), be.reshape(1, -1).astype(F32)]
    return pl.pallas_call(
        _router_kernel,
        out_shape=[jax.ShapeDtypeStruct((S, D), F32),
                   jax.ShapeDtypeStruct((S, LANES), jnp.int32),
                   jax.ShapeDtypeStruct((S, LANES), F32),
                   jax.ShapeDtypeStruct((1, LANES), jnp.int32)],
        grid=(S // tm,),
        in_specs=[pl.BlockSpec((tm, D), lambda i: (i, 0))] + [full(a) for a in args[1:]],
        out_specs=[pl.BlockSpec((tm, D), lambda i: (i, 0)),
                   pl.BlockSpec((tm, LANES), lambda i: (i, 0)),
                   pl.BlockSpec((tm, LANES), lambda i: (i, 0)),
                   pl.BlockSpec((1, LANES), lambda i: (0, 0))],
        compiler_params=_params("arbitrary"),
        name="router",
    )(*args)


def _plan_kernel(cnt_ref, blk_e_ref, blk_n_ref, start_ref):
    nb = blk_e_ref.shape[0]
    sh = MOE_ROWS.bit_length() - 1

    def per_expert(e, b):
        n = cnt_ref[e]
        start_ref[e] = b

        def per_block(t, _):
            blk_e_ref[b + t] = e
            blk_n_ref[b + t] = jnp.minimum(n - t * MOE_ROWS, MOE_ROWS)
            return 0
        k = (n + (MOE_ROWS - 1)) >> sh
        lax.fori_loop(0, k, per_block, 0)
        return b + k
    total = lax.fori_loop(0, N_EXPERTS, per_expert, 0)

    def rest(e, _):
        start_ref[e] = total
        return 0
    lax.fori_loop(N_EXPERTS, start_ref.shape[0], rest, 0)
    last_e = blk_e_ref[jnp.maximum(total - 1, 0)]

    def tail(b, _):
        blk_e_ref[b] = last_e
        blk_n_ref[b] = 0
        return 0
    lax.fori_loop(total, nb, tail, 0)


def block_plan(counts, n_assign):
    nb = (n_assign + N_EXPERTS * (MOE_ROWS - 1)) // MOE_ROWS
    smem = pl.BlockSpec(memory_space=pltpu.SMEM)
    return pl.pallas_call(
        _plan_kernel,
        out_shape=[jax.ShapeDtypeStruct((nb,), jnp.int32), jax.ShapeDtypeStruct((nb,), jnp.int32),
                   jax.ShapeDtypeStruct((LANES,), jnp.int32)],
        in_specs=[smem],
        out_specs=[smem, smem, smem],
        name="block_plan",
    )(counts.reshape(LANES))


def _dest_kernel(eid_ref, start_ref, dest_ref, carry_ref):
    @pl.when(pl.program_id(0) == 0)
    def _():
        carry_ref[...] = jnp.zeros_like(carry_ref)

    eid = eid_ref[...]
    tm = eid.shape[0]
    lane = lax.broadcasted_iota(jnp.int32, (tm, LANES), 1)
    oh0 = lane == eid[:, 0:1]
    oh1 = lane == eid[:, 1:2]
    both = jnp.where(oh0 | oh1, 1.0, 0.0).astype(BF16)
    r = lax.broadcasted_iota(jnp.int32, (tm, tm), 0)
    c = lax.broadcasted_iota(jnp.int32, (tm, tm), 1)
    before = _dot(jnp.where(r > c, 1.0, 0.0).astype(BF16), both) + carry_ref[...]
    pos = before + (start_ref[...] * MOE_ROWS).astype(F32)
    d0 = jnp.sum(jnp.where(oh0, pos, 0.0), axis=-1, keepdims=True).astype(jnp.int32)
    d1 = jnp.sum(jnp.where(oh1, pos, 0.0), axis=-1, keepdims=True).astype(jnp.int32)
    dest_ref[...] = jnp.where(lane == 0, d0, jnp.where(lane == 1, d1, 0))
    carry_ref[...] = carry_ref[...] + jnp.sum(both.astype(F32), axis=0, keepdims=True)


def assignment_dest(eid, blk_start, *, tm=512):
    S = eid.shape[0]
    tm = min(tm, S)
    return pl.pallas_call(
        _dest_kernel,
        out_shape=jax.ShapeDtypeStruct((S, LANES), jnp.int32),
        grid=(S // tm,),
        in_specs=[pl.BlockSpec((tm, LANES), lambda i: (i, 0)), pl.BlockSpec((1, LANES), lambda i: (0, 0))],
        out_specs=pl.BlockSpec((tm, LANES), lambda i: (i, 0)),
        scratch_shapes=[pltpu.VMEM((1, LANES), F32)],
        compiler_params=_params("arbitrary"),
        name="assignment_dest",
    )(eid, blk_start.reshape(1, LANES))


def _for_rows(n, fn):
    sh = MOE_DMA_UNROLL.bit_length() - 1

    def group(t, _):
        for u in range(MOE_DMA_UNROLL):
            fn(t * MOE_DMA_UNROLL + u)
        return 0
    lax.fori_loop(0, n >> sh, group, 0)

    def one(r, _):
        fn(r)
        return 0
    lax.fori_loop((n >> sh) << sh, n, one, 0)


def _experts_kernel(blk_e, blk_n, dest, hn_hbm, wg_ref, wu_ref, wd_ref, out_hbm,
                    xg_ref, xb_ref, acc_ref, wgb_ref, wub_ref, wdb_ref, inv_ref, gsem, ssem):
    i = pl.program_id(0)
    j = pl.program_id(1)
    nb = pl.num_programs(0)
    nsplit = pl.num_programs(1)
    n_tok = hn_hbm.shape[0]
    k_shift = TOP_K.bit_length() - 1

    def gather(b, r):
        tok = inv_ref[b * MOE_ROWS + r] >> k_shift
        return pltpu.make_async_copy(hn_hbm.at[pl.ds(tok, 1)], xg_ref.at[pl.ds(r, 1)], gsem)

    def scatter(b, r):
        a = inv_ref[b * MOE_ROWS + r]
        row = (a & (TOP_K - 1)) * n_tok + (a >> k_shift)
        return pltpu.make_async_copy(acc_ref.at[b & 1, pl.ds(r, 1)], out_hbm.at[pl.ds(row, 1)], ssem)

    def n_gather(b):
        return pl.multiple_of(((blk_n[b] + 7) >> 3) << 3, 8)

    def start_gather(b):
        def pad(r, _):
            inv_ref[b * MOE_ROWS + r] = 0
            return 0
        lax.fori_loop(blk_n[b], n_gather(b), pad, 0)
        _for_rows(n_gather(b), lambda r: gather(b, r).start(priority=1))

    def wait_gather(b):
        n = n_gather(b)

        @pl.when(n > 0)
        def _():
            pltpu.make_async_copy(hn_hbm.at[pl.ds(0, n)], xg_ref.at[pl.ds(0, n)], gsem).wait()

    def start_scatter(b):
        _for_rows(blk_n[b], lambda r: scatter(b, r).start(priority=1))

    def wait_scatter(b):
        n8 = pl.multiple_of((blk_n[b] >> 3) << 3, 8)

        @pl.when(n8 > 0)
        def _():
            pltpu.make_async_copy(acc_ref.at[b & 1, pl.ds(0, n8)], out_hbm.at[pl.ds(0, n8)], ssem).wait()

        def one(r, _):
            scatter(b, r).wait()
            return 0
        lax.fori_loop(n8, blk_n[b], one, 0)

    n_tail_units = (blk_n[i] + (MOE_TAIL - 1)) >> (MOE_TAIL.bit_length() - 1)
    slot = i & 1

    @pl.when(j == 0)
    def _():
        @pl.when(i == 0)
        def _():
            def invert(a):
                inv_ref[dest[a]] = a
            _for_rows(dest.shape[0], invert)
            xg_ref[...] = jnp.zeros_like(xg_ref)
            start_gather(0)
        wait_gather(i)

        def cast(sb, _):
            rs = pl.ds(pl.multiple_of(sb * MOE_TAIL, MOE_TAIL), MOE_TAIL)
            xb_ref[rs, :] = xg_ref[rs, :].astype(BF16)
            return 0
        lax.fori_loop(0, n_tail_units, cast, 0)

        @pl.when(i + 1 < nb)
        def _():
            start_gather(i + 1)

    @pl.when(n_tail_units > 0)
    def _():
        wgb_ref[...] = wg_ref[0].astype(BF16)
        wub_ref[...] = wu_ref[0].astype(BF16)
        wdb_ref[...] = wd_ref[0].astype(BF16)

    def compute(row0, nrows):
        rs = pl.ds(row0, nrows)
        xb = xb_ref[rs, :]
        gate = _dot(xb, wgb_ref[...])
        hid = (gate * _sigmoid(gate)) * _dot(xb, wub_ref[...])
        part = _dot(hid.astype(BF16), wdb_ref[...])

        @pl.when(j == 0)
        def _():
            acc_ref[slot, rs, :] = part

        @pl.when(j > 0)
        def _():
            acc_ref[slot, rs, :] = acc_ref[slot, rs, :] + part

    def sub_block(sb, _):
        compute(pl.multiple_of(sb * MOE_SUB, MOE_SUB), MOE_SUB)
        return 0
    n_full = n_tail_units >> 1
    lax.fori_loop(0, n_full, sub_block, 0)

    @pl.when((n_tail_units & 1) == 1)
    def _():
        compute(pl.multiple_of(n_full * MOE_SUB, MOE_TAIL), MOE_TAIL)

    @pl.when(j == nsplit - 1)
    def _():
        @pl.when(i > 0)
        def _():
            wait_scatter(i - 1)
        start_scatter(i)

        @pl.when(i == nb - 1)
        def _():
            wait_scatter(i)


def experts(hn, blk_e, blk_n, dest, w_gate, w_up, w_down):
    S, D = hn.shape
    E, _, DE = w_gate.shape
    nb = blk_e.shape[0]
    de = DE // MOE_SPLIT
    split = lambda i, j, bn: jnp.where(bn[i] > 0, j, MOE_SPLIT - 1)
    grid_spec = pltpu.PrefetchScalarGridSpec(
        num_scalar_prefetch=3,
        grid=(nb, MOE_SPLIT),
        in_specs=[pl.BlockSpec(memory_space=pl.ANY),
                  pl.BlockSpec((1, D, de), lambda i, j, be, bn, ds: (be[i], 0, split(i, j, bn))),
                  pl.BlockSpec((1, D, de), lambda i, j, be, bn, ds: (be[i], 0, split(i, j, bn))),
                  pl.BlockSpec((1, de, D), lambda i, j, be, bn, ds: (be[i], split(i, j, bn), 0))],
        out_specs=pl.BlockSpec(memory_space=pl.ANY),
        scratch_shapes=[pltpu.VMEM((MOE_ROWS, D), F32),
                        pltpu.VMEM((MOE_ROWS, D), BF16),
                        pltpu.VMEM((2, MOE_ROWS, D), F32),
                        pltpu.VMEM((D, de), BF16),
                        pltpu.VMEM((D, de), BF16),
                        pltpu.VMEM((de, D), BF16),
                        pltpu.SMEM((nb * MOE_ROWS,), jnp.int32),
                        pltpu.SemaphoreType.DMA(()),
                        pltpu.SemaphoreType.DMA(())],
    )
    return pl.pallas_call(
        _experts_kernel,
        out_shape=jax.ShapeDtypeStruct((TOP_K * S, D), F32),
        grid_spec=grid_spec,
        compiler_params=_params("arbitrary", "arbitrary"),
        name="experts",
    )(blk_e, blk_n, dest, hn, w_gate, w_up, w_down)


def _combine_kernel(*refs, has_norm):
    h_ref, y0_ref, y1_ref, w_ref = refs[:4]
    o_ref = refs[-1]
    w = w_ref[...]
    x = h_ref[...] + (y0_ref[0] * w[:, 0:1] + y1_ref[0] * w[:, 1:2])
    if has_norm:
        x = x * lax.rsqrt(jnp.mean(x * x, axis=-1, keepdims=True) + NORM_EPS) * refs[4][...]
    o_ref[...] = x


def combine(h, y_slots, slot_w, gain=None, *, tm=512):
    S, D = h.shape
    tm = min(tm, S)
    y3 = y_slots.reshape(TOP_K, S, D)
    in_specs = [pl.BlockSpec((tm, D), lambda i: (i, 0)),
                pl.BlockSpec((1, tm, D), lambda i: (0, i, 0)),
                pl.BlockSpec((1, tm, D), lambda i: (1, i, 0)),
                pl.BlockSpec((tm, LANES), lambda i: (i, 0))]
    args = [h, y3, y3, slot_w]
    if gain is not None:
        in_specs.append(pl.BlockSpec((1, D), lambda i: (0, 0)))
        args.append(gain.reshape(1, D).astype(F32))
    return pl.pallas_call(
        functools.partial(_combine_kernel, has_norm=gain is not None),
        out_shape=jax.ShapeDtypeStruct((S, D), F32),
        grid=(S // tm,),
        in_specs=in_specs,
        out_specs=pl.BlockSpec((tm, D), lambda i: (i, 0)),
        compiler_params=_params("parallel"),
        name="combine",
    )(*args)


def moe(h, gain, wg, bg, we, be, w_gate, w_up, w_down, final_gain=None):
    S = h.shape[0]
    hn, eid, slot_w, counts = router(h, gain, wg, bg, we, be)
    blk_e, blk_n, blk_start = block_plan(counts, S * TOP_K)
    dest = assignment_dest(eid, blk_start)[:, :TOP_K].reshape(S * TOP_K)
    y_slots = experts(hn, blk_e, blk_n, dest, w_gate, w_up, w_down)
    return combine(h, y_slots, slot_w, final_gain)


def kernel(x, mem, positions, mix_norm, w_in, shift_mu, decay_w0, decay_w2, aaa_a0, aaa_a2, gate_g2, k_k, k_a,
           r_k, lnx_w, lnx_b, diff_lambda, subln_w, w_out, mem_q_norm, mem_kv_norm, wq_mem, wk_mem, wv_mem,
           wo_mem, moe_norm, router_group_w, router_group_b, router_expert_w, router_expert_b, expert_gate,
           expert_up, expert_down, final_norm):
    B, S, D = x.shape
    depth = w_in.shape[0]
    outs = []
    for b in range(B):
        h = x[b]
        memb = mem[b]
        for l in range(depth):
            lambda_init = 0.8 - 0.6 * math.exp(-0.3 * l)
            w_in_b = w_in[l].astype(BF16)
            proj_r = matmul(h, w_in_b[:, :RWKV_COLS], gain=mix_norm[l], tm=1024, tn=RWKV_COLS // 2)
            qkv = diff_proj(h, mix_norm[l], w_in_b[:, RWKV_COLS:], positions[b])
            pre = rwkv_prep(proj_r, shift_mu[l], decay_w0[l], decay_w2[l], aaa_a0[l], aaa_a2[l], gate_g2[l],
                            k_k[l], k_a[l], r_k[l])
            y_rwkv = rwkv_scan(*pre, lnx_w[l], lnx_b[l])
            y_diff = diff_flash(qkv, diff_lambda[l], subln_w[l], lambda_init)
            h = matmul([y_rwkv, y_diff], w_out[l].astype(BF16), residual=h, tm=1024, tn=1024)
            km = matmul(memb, wk_mem[l].astype(BF16), gain=mem_kv_norm[l], out_dtype=BF16)
            vm = matmul(memb, wv_mem[l].astype(BF16), gain=mem_kv_norm[l], out_dtype=BF16)
            o = mem_attn(h, mem_q_norm[l], wq_mem[l].astype(BF16), km, vm)
            h = matmul(o, wo_mem[l].astype(BF16), residual=h, tm=1024, tn=1024)
            h = moe(h, moe_norm[l], router_group_w[l], router_group_b[l], router_expert_w[l], router_expert_b[l],
                    expert_gate[l], expert_up[l], expert_down[l], final_norm if l == depth - 1 else None)
        outs.append(h)
    return jnp.stack(outs, axis=0)
```

```python
import functools
import math

import numpy as np
import jax
import jax.numpy as jnp
from jax import lax
from jax.experimental import pallas as pl
from jax.experimental.pallas import tpu as pltpu

F32 = jnp.float32
BF16 = jnp.bfloat16
HIGHEST = lax.Precision.HIGHEST

RWKV_HEADS = 16
HEAD_N = 64
RWKV_WIDTH = RWKV_HEADS * HEAD_N
DECAY_LORA = 64
AAA_LORA = 64
GATE_LORA = 128
RWKV_COLS = 3 * RWKV_WIDTH + DECAY_LORA + AAA_LORA + GATE_LORA
DIFF_HEADS = 8
DIFF_QK = 64
DIFF_V = 128
DIFF_WIDTH = DIFF_HEADS * DIFF_V
ROPE_THETA = 10000.0
MEM_HEADS = 4
N_GROUPS = 8
EXPERTS_PER_GROUP = 8
N_EXPERTS = N_GROUPS * EXPERTS_PER_GROUP
TOP_K = 2
NORM_EPS = 1e-6
LNX_EPS = 64e-5
SUBLN_EPS = 1e-5

LANES = 128
MXU_WIDTH = 256
VMEM_LIMIT = 56 * 1024 * 1024

CHUNK = 64
SCAN_ROWS = 256
SCAN_WIDTH = 512
MOE_ROWS = 512
MOE_UNIT = 128
MOE_SPLIT = 2
MOE_DMA_UNROLL = 8


def _params(*sem):
    return pltpu.CompilerParams(dimension_semantics=sem, vmem_limit_bytes=VMEM_LIMIT)


def _mm_kernel(*refs, n_x, has_norm, has_res):
    it = iter(refs)
    x_refs = [next(it) for _ in range(n_x)]
    g_ref = next(it) if has_norm else None
    w_ref = next(it)
    r_ref = next(it) if has_res else None
    o_ref = next(it)
    xs_ref = next(it)

    @pl.when(pl.program_id(1) == 0)
    def _():
        if has_norm:
            x = x_refs[0][...].astype(F32)
            x = x * lax.rsqrt(jnp.mean(x * x, axis=-1, keepdims=True) + NORM_EPS) * g_ref[...]
            xs_ref[...] = x.astype(BF16)
        else:
            k0 = 0
            for x_ref in x_refs:
                xs_ref[:, k0:k0 + x_ref.shape[1]] = x_ref[...].astype(BF16)
                k0 += x_ref.shape[1]

    acc = jnp.dot(xs_ref[...], w_ref[...], preferred_element_type=F32)
    if has_res:
        acc = acc + r_ref[...]
    o_ref[...] = acc.astype(o_ref.dtype)


def matmul(x, w, *, gain=None, residual=None, out_dtype=F32, tm=512, tn=512):
    xs = list(x) if isinstance(x, (list, tuple)) else [x]
    assert gain is None or len(xs) == 1
    M = xs[0].shape[0]
    K, N = w.shape
    assert sum(p.shape[1] for p in xs) == K
    tm = min(tm, M)
    assert M % tm == 0 and N % tn == 0, (M, N, tm, tn)
    in_specs = [pl.BlockSpec((tm, p.shape[1]), lambda i, j: (i, 0)) for p in xs]
    args = list(xs)
    if gain is not None:
        in_specs.append(pl.BlockSpec((1, K), lambda i, j: (0, 0)))
        args.append(gain.reshape(1, K).astype(F32))
    in_specs.append(pl.BlockSpec((K, tn), lambda i, j: (0, j)))
    args.append(w)
    if residual is not None:
        in_specs.append(pl.BlockSpec((tm, tn), lambda i, j: (i, j)))
        args.append(residual)
    return pl.pallas_call(
        functools.partial(_mm_kernel, n_x=len(xs), has_norm=gain is not None, has_res=residual is not None),
        out_shape=jax.ShapeDtypeStruct((M, N), out_dtype),
        grid=(M // tm, N // tn),
        in_specs=in_specs,
        out_specs=pl.BlockSpec((tm, tn), lambda i, j: (i, j)),
        scratch_shapes=[pltpu.VMEM((tm, K), BF16)],
        compiler_params=_params("parallel", "arbitrary"),
        name="matmul",
    )(*args)


def _sigmoid(x):
    return 1.0 / (1.0 + jnp.exp(-x))


def _rwkv_prep_kernel(p_ref, pp_ref, mu_ref, w0_ref, w2_ref, a0_ref, a2_ref, g2_ref, kk_ref, ka_ref, rk_ref,
                      hsum_ref, at_o, bt_o, kt_o, rt_o, v_o, g_o, bonus_o, pc_o):
    W = RWKV_WIDTH
    C = CHUNK
    p = p_ref[...]
    last = jnp.where(pl.program_id(0) == 0, 0.0, pp_ref[7:8, :])
    prev = pltpu.roll(p, 1, axis=0)
    row = lax.broadcasted_iota(jnp.int32, p.shape, 0)
    prev = jnp.where(row == 0, last, prev)
    ps = p + (prev - p) * mu_ref[...]
    r = ps[:, 0:W]
    k = ps[:, W:2 * W]
    v = ps[:, 2 * W:3 * W]
    o = 3 * W
    wd = ps[:, o:o + DECAY_LORA]
    ad = ps[:, o + DECAY_LORA:o + DECAY_LORA + AAA_LORA]
    gd = ps[:, o + DECAY_LORA + AAA_LORA:o + DECAY_LORA + AAA_LORA + GATE_LORA]
    z = w0_ref[...] + jnp.dot(jnp.tanh(wd).astype(BF16), w2_ref[...], preferred_element_type=F32)
    nz = -z
    softplus = jnp.maximum(nz, 0.0) + jnp.log(1.0 + jnp.exp(-jnp.abs(nz)))
    w_raw = -softplus - 0.5
    lw = -jnp.exp(w_raw)
    tm = p.shape[0]
    rr = lax.broadcasted_iota(jnp.int32, (tm, tm), 0)
    cc = lax.broadcasted_iota(jnp.int32, (tm, tm), 1)
    sh = C.bit_length() - 1
    tri = (((rr >> sh) == (cc >> sh)) & (rr >= cc)).astype(F32)
    cs = _dot(tri, lw, HIGHEST)
    a = _sigmoid(a0_ref[...] + jnp.dot(ad.astype(BF16), a2_ref[...], preferred_element_type=F32))
    g_o[...] = jnp.dot(_sigmoid(gd).astype(BF16), g2_ref[...], preferred_element_type=F32)

    def head_sum(x):
        hi = x.astype(BF16)
        lo = (x - hi.astype(F32)).astype(BF16)
        ones = hsum_ref[...]
        wb = ones.shape[0]
        cols = [slice(c, c + wb) for c in range(0, x.shape[1], wb)]
        return jnp.concatenate([_dot(hi[:, cs], ones) + _dot(lo[:, cs], ones) for cs in cols], axis=1)

    kk = k * kk_ref[...]
    kk = kk / jnp.maximum(jnp.sqrt(head_sum(kk * kk)), 1e-12)
    k2 = k * (1.0 + (a - 1.0) * ka_ref[...])
    e_neg = jnp.exp(-cs)
    at_o[...] = (-kk * jnp.exp(cs - lw)).astype(at_o.dtype)
    bt_o[...] = (kk * a) * e_neg
    kt_o[...] = k2 * e_neg
    rt_o[...] = r * jnp.exp(cs)
    v_o[...] = v
    bonus_o[...] = head_sum(r * k2 * rk_ref[...]) * v
    for q in range(tm // C):
        pc_o[q] = jnp.exp(cs[q * C + C - 1:q * C + C, :])


def rwkv_prep(p, mu, w0, w2, a0, a2, g2, k_k, k_a, r_k, *, tm=256):
    S = p.shape[0]
    tm = min(tm, S)
    assert tm % CHUNK == 0
    W = RWKV_WIDTH
    row = lambda x: x.reshape(1, -1).astype(F32)
    full = lambda a: pl.BlockSpec(a.shape, lambda i: (0, 0))
    head = jnp.arange(2 * LANES, dtype=jnp.int32) // HEAD_N
    hsum = (head[:, None] == head[None, :]).astype(BF16)
    args = [p, p, row(mu), row(w0), w2.astype(BF16), row(a0), a2.astype(BF16), g2.astype(BF16),
            row(k_k), row(k_a), row(r_k), hsum]
    in_specs = [pl.BlockSpec((tm, RWKV_COLS), lambda i: (i, 0)),
                pl.BlockSpec((8, RWKV_COLS), lambda i: (jnp.maximum(i * (tm // 8) - 1, 0), 0))]
    in_specs += [full(a) for a in args[2:]]
    f32 = jax.ShapeDtypeStruct((S, W), F32)
    blk = pl.BlockSpec((tm, W), lambda i: (i, 0))
    return pl.pallas_call(
        _rwkv_prep_kernel,
        out_shape=[jax.ShapeDtypeStruct((S, W), BF16)] + [f32] * 6 + [jax.ShapeDtypeStruct((S // CHUNK, 1, W), F32)],
        grid=(S // tm,),
        in_specs=in_specs,
        out_specs=[blk] * 7 + [pl.BlockSpec((tm // CHUNK, 1, W), lambda i: (i, 0, 0))],
        compiler_params=_params("parallel"),
        name="rwkv_prep",
    )(*args)


def _dot_nt(a, b, precision=None):
    return lax.dot_general(a, b, (((1,), (1,)), ((), ())), preferred_element_type=F32, precision=precision)


def _dot_tn(a, b, precision=None):
    return lax.dot_general(a, b, (((0,), (0,)), ((), ())), preferred_element_type=F32, precision=precision)


def _dot(a, b, precision=None):
    return jnp.dot(a, b, preferred_element_type=F32, precision=precision)


def _rwkv_scan_kernel(at_ref, bt_ref, kt_ref, rt_ref, v_ref, g_ref, bonus_ref, pc_ref, lnw_ref, lnb_ref,
                      y_ref, state_ref, yf_ref):
    C = CHUNK
    N = HEAD_N

    @pl.when(pl.program_id(1) == 0)
    def _():
        state_ref[...] = jnp.zeros_like(state_ref)

    row = lax.broadcasted_iota(jnp.int32, (C, C), 0)
    col = lax.broadcasted_iota(jnp.int32, (C, C), 1)
    eye = (row == col).astype(F32)
    strict = row > col
    incl = row >= col

    n_chunks = rt_ref.shape[0] // C
    heads = rt_ref.shape[1] // N
    items = [(j, q) for j in range(heads) for q in range(n_chunks)]
    G = range(len(items))

    def tile(ref, j, q):
        return ref[q * C:(q + 1) * C, j * N:(j + 1) * N]

    def par(ref, j):
        return ref[:, j * N:(j + 1) * N]

    at = [tile(at_ref, j, q) for j, q in items]
    bt = [tile(bt_ref, j, q) for j, q in items]
    kt = [tile(kt_ref, j, q) for j, q in items]
    rt = [tile(rt_ref, j, q) for j, q in items]
    v = [tile(v_ref, j, q) for j, q in items]
    pc = [pc_ref[q, :, j * N:(j + 1) * N] for j, q in items]
    vb = [x.astype(BF16) for x in v]
    btb = [x.astype(BF16) for x in bt]
    ktb = [x.astype(BF16) for x in kt]
    rtb = [x.astype(BF16) for x in rt]
    n_ab = [jnp.where(strict, _dot_nt(at[g], btb[g]), 0.0) for g in G]
    a_ak = [jnp.where(strict, _dot_nt(at[g], ktb[g]), 0.0).astype(BF16) for g in G]
    a_rb = [jnp.where(incl, _dot_nt(rtb[g], btb[g]), 0.0).astype(BF16) for g in G]
    a_rk = [jnp.where(incl, _dot_nt(rtb[g], ktb[g]), 0.0).astype(BF16) for g in G]
    akv = [_dot(a_ak[g], vb[g]) for g in G]
    t = None
    b = 1
    while b < C:
        sh = (2 * b).bit_length() - 1
        low_left = ((row >> sh) == (col >> sh)) & ((row & b) != 0) & ((col & b) == 0)
        nb = [jnp.where(low_left, n_ab[g], 0.0) for g in G]
        if b == 1:
            t = [eye + nb[g] for g in G]
        else:
            tb = [t[g].astype(BF16) for g in G]
            z = [_dot(nb[g].astype(BF16), tb[g]).astype(BF16) for g in G]
            t = [t[g] + _dot(tb[g], z[g]) for g in G]
        b *= 2
    tb = [t[g].astype(BF16) for g in G]
    wm = [_dot(tb[g], at[g]).astype(BF16) for g in G]
    u0 = [_dot(tb[g], akv[g].astype(BF16)).astype(BF16) for g in G]
    rm = [(rt[g] + _dot(a_rb[g], wm[g])).astype(BF16) for g in G]
    y0 = [_dot(a_rb[g], u0[g]) + _dot(a_rk[g], vb[g]) for g in G]
    bp = [(bt[g] * pc[g]).astype(BF16) for g in G]
    kp = [(kt[g] * pc[g]).astype(BF16) for g in G]
    mp = [_dot_tn(wm[g], bp[g]).astype(BF16) for g in G]
    s_add = [_dot_tn(u0[g], bp[g]) + _dot_tn(vb[g], kp[g]) for g in G]

    for j in range(heads):
        s = state_ref[j]
        for q in range(n_chunks):
            g = j * n_chunks + q
            sb = s.astype(BF16)
            y = _dot_nt(rm[g], sb) + y0[g]
            s = s * pc[g] + _dot(sb, mp[g]) + s_add[g]
            mean = jnp.mean(y, axis=-1, keepdims=True)
            yc = y - mean
            var = jnp.mean(yc * yc, axis=-1, keepdims=True)
            yn = yc * lax.rsqrt(var + LNX_EPS) * par(lnw_ref, j) + par(lnb_ref, j)
            yf_ref[q * C:(q + 1) * C, j * N:(j + 1) * N] = (yn + tile(bonus_ref, j, q)) * tile(g_ref, j, q)
        state_ref[j] = s
    y_ref[...] = yf_ref[...].astype(y_ref.dtype)


def rwkv_scan(at, bt, kt, rt, v, g, bonus, pc, lnx_w, lnx_b, *, rows=SCAN_ROWS, width=SCAN_WIDTH):
    S, W = rt.shape
    rows = min(rows, S)
    hp = width // HEAD_N
    row = lambda x: x.reshape(1, W).astype(F32)
    seq = pl.BlockSpec((rows, width), lambda h, c: (c, h))
    par = pl.BlockSpec((1, width), lambda h, c: (0, h))
    return pl.pallas_call(
        _rwkv_scan_kernel,
        out_shape=jax.ShapeDtypeStruct((S, W), BF16),
        grid=(W // width, S // rows),
        in_specs=[seq] * 7 + [pl.BlockSpec((rows // CHUNK, 1, width), lambda h, c: (c, 0, h))] + [par] * 2,
        out_specs=seq,
        scratch_shapes=[pltpu.VMEM((hp, HEAD_N, HEAD_N), F32), pltpu.VMEM((rows, width), F32)],
        compiler_params=_params("parallel", "arbitrary"),
        name="rwkv_scan",
    )(at, bt, kt, rt, v, g, bonus, pc, row(lnx_w), row(lnx_b))


def _diff_proj_kernel(x_ref, g_ref, w_ref, pos_ref, freq_ref, o_ref, xs_ref, cos_ref, sin_ref):
    j = pl.program_id(1)
    half = DIFF_QK // 2

    def first_half(shape):
        return (lax.broadcasted_iota(jnp.int32, shape, 1) & (DIFF_QK - 1)) < half

    @pl.when(j == 0)
    def _():
        x = x_ref[...]
        x = x * lax.rsqrt(jnp.mean(x * x, axis=-1, keepdims=True) + NORM_EPS) * g_ref[...]
        xs_ref[...] = x.astype(BF16)
        ang = pos_ref[...] * freq_ref[...]
        cos_ref[...] = jnp.cos(ang)
        sin_ref[...] = jnp.where(first_half(ang.shape), -1.0, 1.0) * jnp.sin(ang)

    @pl.when(j < 2)
    def _():
        sc = jnp.where(j == 0, DIFF_QK ** -0.5, 1.0)
        cos = cos_ref[...]
        sin = sin_ref[...]
        fh = first_half(cos.shape)
        for c0 in range(0, o_ref.shape[1], MXU_WIDTH):
            acc = jnp.dot(xs_ref[...], w_ref[:, c0:c0 + MXU_WIDTH], preferred_element_type=F32)
            for b0 in range(0, MXU_WIDTH, LANES):
                x = acc[:, b0:b0 + LANES]
                partner = jnp.where(fh, pltpu.roll(x, LANES - half, axis=1), pltpu.roll(x, half, axis=1))
                o_ref[:, c0 + b0:c0 + b0 + LANES] = ((x * cos + partner * sin) * sc).astype(o_ref.dtype)

    @pl.when(j == 2)
    def _():
        o_ref[...] = jnp.dot(xs_ref[...], w_ref[...], preferred_element_type=F32).astype(o_ref.dtype)


def diff_proj(x, gain, w, positions, *, tm=1024):
    S, K = x.shape
    W = DIFF_WIDTH
    tm = min(tm, S)
    assert w.shape == (K, 3 * W) and S % tm == 0
    inv_freq = ROPE_THETA ** (-(jnp.arange(0, DIFF_QK, 2, dtype=F32) / DIFF_QK))
    freq = jnp.tile(inv_freq, LANES // (DIFF_QK // 2)).reshape(1, LANES)
    pos = positions.reshape(S, 1).astype(F32)
    return pl.pallas_call(
        _diff_proj_kernel,
        out_shape=jax.ShapeDtypeStruct((S, 3 * W), BF16),
        grid=(S // tm, 3),
        in_specs=[pl.BlockSpec((tm, K), lambda i, j: (i, 0)),
                  pl.BlockSpec((1, K), lambda i, j: (0, 0)),
                  pl.BlockSpec((K, W), lambda i, j: (0, j)),
                  pl.BlockSpec((tm, 1), lambda i, j: (i, 0)),
                  pl.BlockSpec((1, LANES), lambda i, j: (0, 0))],
        out_specs=pl.BlockSpec((tm, W), lambda i, j: (i, j)),
        scratch_shapes=[pltpu.VMEM((tm, K), BF16), pltpu.VMEM((tm, LANES), F32), pltpu.VMEM((tm, LANES), F32)],
        compiler_params=_params("parallel", "arbitrary"),
        name="diff_proj",
    )(x, gain.reshape(1, K).astype(F32), w, pos, freq)


def _diff_flash_kernel(qi_tab, ki_tab, q_ref, k_ref, v_ref, lam_ref, sw_ref, o_ref,
                       m_ref, l_ref, acc_ref, s_ref, p_ref, alpha_ref, *, tile, row_block, diag_parts, lambda_init):
    p = pl.program_id(1)
    qi = qi_tab[p]
    ki = ki_tab[p]

    @pl.when(ki == 0)
    def _():
        m_ref[...] = jnp.full_like(m_ref, -jnp.inf)
        l_ref[...] = jnp.zeros_like(l_ref)
        acc_ref[...] = jnp.zeros_like(acc_ref)

    def process(row0, nrows, kmax, masked):
        rows = slice(row0, row0 + nrows)
        q = q_ref[rows, :]
        k = k_ref[0:kmax, :]
        v = v_ref[0:kmax, :]
        lane = lax.broadcasted_iota(jnp.int32, q.shape, 1)
        reps = kmax // LANES
        for c in range(2):
            qc = jnp.where((lane >> (DIFF_QK.bit_length() - 1)) == c, q, jnp.zeros_like(q))
            s_ref[0:nrows, 0:kmax] = _dot_nt(qc, k)
            for r0 in range(0, nrows, row_block):
                ar = slice(row0 + r0, row0 + r0 + row_block)
                lr = slice(r0, r0 + row_block)
                s = s_ref[lr, 0:kmax]
                if masked:
                    qpos = lax.broadcasted_iota(jnp.int32, s.shape, 0) + (row0 + r0)
                    kpos = lax.broadcasted_iota(jnp.int32, s.shape, 1)
                    s = jnp.where(kpos <= qpos, s, -jnp.inf)
                m_old = m_ref[c, ar, :]
                m_new = jnp.maximum(m_old, jnp.max(s, axis=-1, keepdims=True))
                alpha = jnp.exp(m_old - m_new)
                pr = jnp.exp(s - jnp.concatenate([m_new] * reps, axis=1))
                l_ref[c, ar, :] = alpha * l_ref[c, ar, :] + jnp.sum(pr, axis=-1, keepdims=True)
                m_ref[c, ar, :] = m_new
                alpha_ref[lr, :] = alpha
                p_ref[lr, 0:kmax] = pr.astype(BF16)
            acc_ref[c, rows, :] = (alpha_ref[0:nrows, :] * acc_ref[c, rows, :]
                                   + _dot(p_ref[0:nrows, 0:kmax], v))

    @pl.when(ki < qi)
    def _():
        process(0, tile, tile, False)

    @pl.when(ki == qi)
    def _():
        part = tile // diag_parts
        for a in range(diag_parts):
            process(a * part, part, (a + 1) * part, True)
        lp = lam_ref[...]
        lam = (jnp.exp(jnp.sum(lp[0:1] * lp[1:2], axis=-1, keepdims=True))
               - jnp.exp(jnp.sum(lp[2:3] * lp[3:4], axis=-1, keepdims=True)) + lambda_init)
        o = acc_ref[0] / l_ref[0] - lam * (acc_ref[1] / l_ref[1])
        o = o * lax.rsqrt(jnp.mean(o * o, axis=-1, keepdims=True) + SUBLN_EPS)
        o_ref[...] = (o * sw_ref[...] * (1.0 - lambda_init)).astype(o_ref.dtype)


def diff_flash(qkv, lambdas, subln_w, lambda_init, *, tile=1024, row_block=32, diag_parts=2):
    S = qkv.shape[0]
    tile = min(tile, S)
    assert S % tile == 0 and (tile // diag_parts) % LANES == 0 and (tile // diag_parts) % row_block == 0
    pairs = [(qi, ki) for qi in range(S // tile) for ki in range(qi + 1)]
    qi_tab = jnp.asarray([p[0] for p in pairs], jnp.int32)
    ki_tab = jnp.asarray([p[1] for p in pairs], jnp.int32)
    grid_spec = pltpu.PrefetchScalarGridSpec(
        num_scalar_prefetch=2,
        grid=(DIFF_HEADS, len(pairs)),
        in_specs=[pl.BlockSpec((tile, LANES), lambda h, p, qt, kt: (qt[p], h)),
                  pl.BlockSpec((tile, LANES), lambda h, p, qt, kt: (kt[p], DIFF_HEADS + h)),
                  pl.BlockSpec((tile, LANES), lambda h, p, qt, kt: (kt[p], 2 * DIFF_HEADS + h)),
                  pl.BlockSpec((4, DIFF_QK), lambda h, p, qt, kt: (0, 0)),
                  pl.BlockSpec((1, DIFF_V), lambda h, p, qt, kt: (0, 0))],
        out_specs=pl.BlockSpec((tile, LANES), lambda h, p, qt, kt: (qt[p], h)),
        scratch_shapes=[pltpu.VMEM((2, tile, LANES), F32),
                        pltpu.VMEM((2, tile, LANES), F32),
                        pltpu.VMEM((2, tile, DIFF_V), F32),
                        pltpu.VMEM((tile, tile), F32),
                        pltpu.VMEM((tile, tile), BF16),
                        pltpu.VMEM((tile, LANES), F32)],
    )
    return pl.pallas_call(
        functools.partial(_diff_flash_kernel, tile=tile, row_block=row_block, diag_parts=diag_parts,
                          lambda_init=lambda_init),
        out_shape=jax.ShapeDtypeStruct((S, DIFF_WIDTH), BF16),
        grid_spec=grid_spec,
        compiler_params=_params("parallel", "arbitrary"),
        name="diff_flash",
    )(qi_tab, ki_tab, qkv, qkv, qkv, lambdas.astype(F32), subln_w.reshape(1, DIFF_V).astype(F32))


def _mem_attn_kernel(x_ref, g_ref, w_ref, k_ref, v_ref, o_ref, xs_ref):
    @pl.when(pl.program_id(1) == 0)
    def _():
        x = x_ref[...]
        x = x * lax.rsqrt(jnp.mean(x * x, axis=-1, keepdims=True) + NORM_EPS) * g_ref[...]
        xs_ref[...] = x.astype(BF16)

    q = jnp.dot(xs_ref[...], w_ref[...], preferred_element_type=F32).astype(BF16)
    s = _dot_nt(q, k_ref[...]) * (q.shape[1] ** -0.5)
    s = s - jnp.max(s, axis=-1, keepdims=True)
    e = jnp.exp(s)
    pr = e / jnp.sum(e, axis=-1, keepdims=True)
    o_ref[...] = _dot(pr.astype(BF16), v_ref[...]).astype(o_ref.dtype)


def mem_attn(x, gain, wq, k, v, *, tm=1024):
    S, D = x.shape
    M = k.shape[0]
    hd = D // MEM_HEADS
    tm = min(tm, S)
    return pl.pallas_call(
        _mem_attn_kernel,
        out_shape=jax.ShapeDtypeStruct((S, D), BF16),
        grid=(S // tm, MEM_HEADS),
        in_specs=[pl.BlockSpec((tm, D), lambda i, j: (i, 0)),
                  pl.BlockSpec((1, D), lambda i, j: (0, 0)),
                  pl.BlockSpec((D, hd), lambda i, j: (0, j)),
                  pl.BlockSpec((M, hd), lambda i, j: (0, j)),
                  pl.BlockSpec((M, hd), lambda i, j: (0, j))],
        out_specs=pl.BlockSpec((tm, hd), lambda i, j: (i, j)),
        scratch_shapes=[pltpu.VMEM((tm, D), BF16)],
        compiler_params=_params("parallel", "arbitrary"),
        name="mem_attn",
    )(x, gain.reshape(1, D).astype(F32), wq, k, v)


def _first_argmax(x, lane, big):
    m = jnp.max(x, axis=-1, keepdims=True)
    idx = jnp.min(jnp.where(x == m, lane, big), axis=-1, keepdims=True)
    return m, idx


def _router_kernel(h_ref, gain_ref, wg_ref, bg_ref, we_ref, be_ref, hn_ref, eid_ref, ew_ref, cnt_ref):
    @pl.when(pl.program_id(0) == 0)
    def _():
        cnt_ref[...] = jnp.zeros_like(cnt_ref)

    x = h_ref[...]
    hn = x * lax.rsqrt(jnp.mean(x * x, axis=-1, keepdims=True) + NORM_EPS) * gain_ref[...]
    hn_ref[...] = hn
    hn_hi = hn.astype(BF16)
    hn_lo = (hn - hn_hi.astype(F32)).astype(BF16)

    def logits(w_ref, b_ref):
        w = w_ref[...]
        w_hi = w.astype(BF16)
        w_lo = (w - w_hi.astype(F32)).astype(BF16)
        return _dot(hn_hi, w_hi) + (_dot(hn_hi, w_lo) + _dot(hn_lo, w_hi)) + b_ref[...]
    g_logits = logits(wg_ref, bg_ref)
    e_logits = logits(we_ref, be_ref)
    tm = x.shape[0]
    lane_g = lax.broadcasted_iota(jnp.int32, (tm, N_GROUPS), 1)
    g_max, g_idx = _first_argmax(g_logits, lane_g, N_GROUPS)
    g_w = 1.0 / jnp.sum(jnp.exp(g_logits - g_max), axis=-1, keepdims=True)
    lane_e = lax.broadcasted_iota(jnp.int32, (tm, N_EXPERTS), 1)
    in_group = (lane_e // EXPERTS_PER_GROUP) == g_idx
    el = jnp.where(in_group, e_logits, -jnp.inf)
    e_max = jnp.max(el, axis=-1, keepdims=True)
    ex = jnp.exp(el - e_max)
    prob = ex / jnp.sum(ex, axis=-1, keepdims=True)
    prob = jnp.where(in_group, prob, -1.0)
    p1, i1 = _first_argmax(prob, lane_e, N_EXPERTS)
    p2, i2 = _first_argmax(jnp.where(lane_e == i1, -1.0, prob), lane_e, N_EXPERTS)
    tot = p1 + p2
    lane_o = lax.broadcasted_iota(jnp.int32, (tm, LANES), 1)
    eid_ref[...] = jnp.where(lane_o == 0, i1, jnp.where(lane_o == 1, i2, 0))
    ew_ref[...] = jnp.where(lane_o == 0, g_w * (p1 / tot), jnp.where(lane_o == 1, g_w * (p2 / tot), 0.0))
    chosen = ((lane_o == i1) | (lane_o == i2)).astype(F32)
    cnt_ref[...] = cnt_ref[...] + jnp.sum(chosen, axis=0, keepdims=True).astype(jnp.int32)


def router(h, gain, wg, bg, we, be, *, tm=512):
    S, D = h.shape
    tm = min(tm, S)
    full = lambda a: pl.BlockSpec(a.shape, lambda i: (0, 0))
    args = [h, gain.reshape(1, D).astype(F32), wg.astype(F32), bg.reshape(1, -1).astype(F32),
            we.astype(F32), be.reshape(1, -1).astype(F32)]
    return pl.pallas_call(
        _router_kernel,
        out_shape=[jax.ShapeDtypeStruct((S, D), F32),
                   jax.ShapeDtypeStruct((S, LANES), jnp.int32),
                   jax.ShapeDtypeStruct((S, LANES), F32),
                   jax.ShapeDtypeStruct((1, LANES), jnp.int32)],
        grid=(S // tm,),
        in_specs=[pl.BlockSpec((tm, D), lambda i: (i, 0))] + [full(a) for a in args[1:]],
        out_specs=[pl.BlockSpec((tm, D), lambda i: (i, 0)),
                   pl.BlockSpec((tm, LANES), lambda i: (i, 0)),
                   pl.BlockSpec((tm, LANES), lambda i: (i, 0)),
                   pl.BlockSpec((1, LANES), lambda i: (0, 0))],
        compiler_params=_params("arbitrary"),
        name="router",
    )(*args)


def _plan_kernel(cnt_ref, blk_e_ref, blk_n_ref, start_ref):
    nb = blk_e_ref.shape[0]
    sh = MOE_ROWS.bit_length() - 1

    def per_expert(e, b):
        n = cnt_ref[e]
        start_ref[e] = b

        def per_block(t, _):
            blk_e_ref[b + t] = e
            blk_n_ref[b + t] = jnp.minimum(n - t * MOE_ROWS, MOE_ROWS)
            return 0
        k = (n + (MOE_ROWS - 1)) >> sh
        lax.fori_loop(0, k, per_block, 0)
        return b + k
    total = lax.fori_loop(0, N_EXPERTS, per_expert, 0)

    def rest(e, _):
        start_ref[e] = total
        return 0
    lax.fori_loop(N_EXPERTS, start_ref.shape[0], rest, 0)
    last_e = blk_e_ref[jnp.maximum(total - 1, 0)]

    def tail(b, _):
        blk_e_ref[b] = last_e
        blk_n_ref[b] = 0
        return 0
    lax.fori_loop(total, nb, tail, 0)


def block_plan(counts, n_assign):
    nb = (n_assign + N_EXPERTS * (MOE_ROWS - 1)) // MOE_ROWS
    smem = pl.BlockSpec(memory_space=pltpu.SMEM)
    return pl.pallas_call(
        _plan_kernel,
        out_shape=[jax.ShapeDtypeStruct((nb,), jnp.int32), jax.ShapeDtypeStruct((nb,), jnp.int32),
                   jax.ShapeDtypeStruct((LANES,), jnp.int32)],
        in_specs=[smem],
        out_specs=[smem, smem, smem],
        name="block_plan",
    )(counts.reshape(LANES))


def _dest_kernel(eid_ref, start_ref, dest_ref, carry_ref):
    @pl.when(pl.program_id(0) == 0)
    def _():
        carry_ref[...] = jnp.zeros_like(carry_ref)

    eid = eid_ref[...]
    tm = eid.shape[0]
    lane = lax.broadcasted_iota(jnp.int32, (tm, LANES), 1)
    oh0 = lane == eid[:, 0:1]
    oh1 = lane == eid[:, 1:2]
    both = jnp.where(oh0 | oh1, 1.0, 0.0).astype(BF16)
    r = lax.broadcasted_iota(jnp.int32, (tm, tm), 0)
    c = lax.broadcasted_iota(jnp.int32, (tm, tm), 1)
    before = _dot(jnp.where(r > c, 1.0, 0.0).astype(BF16), both) + carry_ref[...]
    pos = before + (start_ref[...] * MOE_ROWS).astype(F32)
    d0 = jnp.sum(jnp.where(oh0, pos, 0.0), axis=-1, keepdims=True).astype(jnp.int32)
    d1 = jnp.sum(jnp.where(oh1, pos, 0.0), axis=-1, keepdims=True).astype(jnp.int32)
    dest_ref[...] = jnp.where(lane == 0, d0, jnp.where(lane == 1, d1, 0))
    carry_ref[...] = carry_ref[...] + jnp.sum(both.astype(F32), axis=0, keepdims=True)


def assignment_dest(eid, blk_start, *, tm=512):
    S = eid.shape[0]
    tm = min(tm, S)
    return pl.pallas_call(
        _dest_kernel,
        out_shape=jax.ShapeDtypeStruct((S, LANES), jnp.int32),
        grid=(S // tm,),
        in_specs=[pl.BlockSpec((tm, LANES), lambda i: (i, 0)), pl.BlockSpec((1, LANES), lambda i: (0, 0))],
        out_specs=pl.BlockSpec((tm, LANES), lambda i: (i, 0)),
        scratch_shapes=[pltpu.VMEM((1, LANES), F32)],
        compiler_params=_params("arbitrary"),
        name="assignment_dest",
    )(eid, blk_start.reshape(1, LANES))


def _for_rows(n, fn):
    sh = MOE_DMA_UNROLL.bit_length() - 1

    def group(t, _):
        for u in range(MOE_DMA_UNROLL):
            fn(t * MOE_DMA_UNROLL + u)
        return 0
    lax.fori_loop(0, n >> sh, group, 0)

    def one(r, _):
        fn(r)
        return 0
    lax.fori_loop((n >> sh) << sh, n, one, 0)


def _experts_kernel(blk_e, blk_n, dest, hn_hbm, wg_ref, wu_ref, wd_ref, out_hbm,
                    xg_ref, xb_ref, acc_ref, wgb_ref, wub_ref, wdb_ref, inv_ref, gsem, ssem):
    i = pl.program_id(0)
    j = pl.program_id(1)
    nb = pl.num_programs(0)
    nsplit = pl.num_programs(1)
    n_tok = hn_hbm.shape[0]
    k_shift = TOP_K.bit_length() - 1

    def gather(b, r):
        tok = inv_ref[b * MOE_ROWS + r] >> k_shift
        return pltpu.make_async_copy(hn_hbm.at[pl.ds(tok, 1)], xg_ref.at[pl.ds(r, 1)], gsem)

    def scatter(b, r):
        a = inv_ref[b * MOE_ROWS + r]
        row = (a & (TOP_K - 1)) * n_tok + (a >> k_shift)
        return pltpu.make_async_copy(acc_ref.at[b & 1, pl.ds(r, 1)], out_hbm.at[pl.ds(row, 1)], ssem)

    def n_gather(b):
        return pl.multiple_of(((blk_n[b] + 7) >> 3) << 3, 8)

    def start_gather(b):
        def pad(r, _):
            inv_ref[b * MOE_ROWS + r] = 0
            return 0
        lax.fori_loop(blk_n[b], n_gather(b), pad, 0)
        _for_rows(n_gather(b), lambda r: gather(b, r).start(priority=1))

    def wait_gather(b):
        n = n_gather(b)

        @pl.when(n > 0)
        def _():
            pltpu.make_async_copy(hn_hbm.at[pl.ds(0, n)], xg_ref.at[pl.ds(0, n)], gsem).wait()

    def start_scatter(b):
        _for_rows(blk_n[b], lambda r: scatter(b, r).start(priority=1))

    def wait_scatter(b):
        n8 = pl.multiple_of((blk_n[b] >> 3) << 3, 8)

        @pl.when(n8 > 0)
        def _():
            pltpu.make_async_copy(acc_ref.at[b & 1, pl.ds(0, n8)], out_hbm.at[pl.ds(0, n8)], ssem).wait()

        def one(r, _):
            scatter(b, r).wait()
            return 0
        lax.fori_loop(n8, blk_n[b], one, 0)

    n_units = (blk_n[i] + (MOE_UNIT - 1)) >> (MOE_UNIT.bit_length() - 1)
    slot = i & 1

    @pl.when(j == 0)
    def _():
        @pl.when(i == 0)
        def _():
            def invert(a):
                inv_ref[dest[a]] = a
            _for_rows(dest.shape[0], invert)
            xg_ref[...] = jnp.zeros_like(xg_ref)
            start_gather(0)
        wait_gather(i)

        def cast(sb, _):
            rs = pl.ds(pl.multiple_of(sb * MOE_UNIT, MOE_UNIT), MOE_UNIT)
            xb_ref[rs, :] = xg_ref[rs, :].astype(BF16)
            return 0
        lax.fori_loop(0, n_units, cast, 0)

        @pl.when(i + 1 < nb)
        def _():
            start_gather(i + 1)

    @pl.when(n_units > 0)
    def _():
        wgb_ref[...] = wg_ref[0].astype(BF16)
        wub_ref[...] = wu_ref[0].astype(BF16)
        wdb_ref[...] = wd_ref[0].astype(BF16)

    def compute(row0, nrows):
        rs = pl.ds(row0, nrows)
        xb = xb_ref[rs, :]
        gate = _dot(xb, wgb_ref[...])
        hid = (gate * _sigmoid(gate)) * _dot(xb, wub_ref[...])
        part = _dot(hid.astype(BF16), wdb_ref[...])

        @pl.when(j == 0)
        def _():
            acc_ref[slot, rs, :] = part

        @pl.when(j > 0)
        def _():
            acc_ref[slot, rs, :] = acc_ref[slot, rs, :] + part

    for units in range(1, MOE_ROWS // MOE_UNIT + 1):
        @pl.when(n_units == units)
        def _():
            compute(0, units * MOE_UNIT)

    @pl.when(j == nsplit - 1)
    def _():
        @pl.when(i > 0)
        def _():
            wait_scatter(i - 1)
        start_scatter(i)

        @pl.when(i == nb - 1)
        def _():
            wait_scatter(i)


def experts(hn, blk_e, blk_n, dest, w_gate, w_up, w_down):
    S, D = hn.shape
    E, _, DE = w_gate.shape
    nb = blk_e.shape[0]
    de = DE // MOE_SPLIT
    split = lambda i, j, bn: jnp.where(bn[i] > 0, j, MOE_SPLIT - 1)
    grid_spec = pltpu.PrefetchScalarGridSpec(
        num_scalar_prefetch=3,
        grid=(nb, MOE_SPLIT),
        in_specs=[pl.BlockSpec(memory_space=pl.ANY),
                  pl.BlockSpec((1, D, de), lambda i, j, be, bn, ds: (be[i], 0, split(i, j, bn))),
                  pl.BlockSpec((1, D, de), lambda i, j, be, bn, ds: (be[i], 0, split(i, j, bn))),
                  pl.BlockSpec((1, de, D), lambda i, j, be, bn, ds: (be[i], split(i, j, bn), 0))],
        out_specs=pl.BlockSpec(memory_space=pl.ANY),
        scratch_shapes=[pltpu.VMEM((MOE_ROWS, D), F32),
                        pltpu.VMEM((MOE_ROWS, D), BF16),
                        pltpu.VMEM((2, MOE_ROWS, D), F32),
                        pltpu.VMEM((D, de), BF16),
                        pltpu.VMEM((D, de), BF16),
                        pltpu.VMEM((de, D), BF16),
                        pltpu.SMEM((nb * MOE_ROWS,), jnp.int32),
                        pltpu.SemaphoreType.DMA(()),
                        pltpu.SemaphoreType.DMA(())],
    )
    return pl.pallas_call(
        _experts_kernel,
        out_shape=jax.ShapeDtypeStruct((TOP_K * S, D), F32),
        grid_spec=grid_spec,
        compiler_params=_params("arbitrary", "arbitrary"),
        name="experts",
    )(blk_e, blk_n, dest, hn, w_gate, w_up, w_down)


def _combine_kernel(*refs, has_norm):
    h_ref, y0_ref, y1_ref, w_ref = refs[:4]
    o_ref = refs[-1]
    w = w_ref[...]
    x = h_ref[...] + (y0_ref[0] * w[:, 0:1] + y1_ref[0] * w[:, 1:2])
    if has_norm:
        x = x * lax.rsqrt(jnp.mean(x * x, axis=-1, keepdims=True) + NORM_EPS) * refs[4][...]
    o_ref[...] = x


def combine(h, y_slots, slot_w, gain=None, *, tm=512):
    S, D = h.shape
    tm = min(tm, S)
    y3 = y_slots.reshape(TOP_K, S, D)
    in_specs = [pl.BlockSpec((tm, D), lambda i: (i, 0)),
                pl.BlockSpec((1, tm, D), lambda i: (0, i, 0)),
                pl.BlockSpec((1, tm, D), lambda i: (1, i, 0)),
                pl.BlockSpec((tm, LANES), lambda i: (i, 0))]
    args = [h, y3, y3, slot_w]
    if gain is not None:
        in_specs.append(pl.BlockSpec((1, D), lambda i: (0, 0)))
        args.append(gain.reshape(1, D).astype(F32))
    return pl.pallas_call(
        functools.partial(_combine_kernel, has_norm=gain is not None),
        out_shape=jax.ShapeDtypeStruct((S, D), F32),
        grid=(S // tm,),
        in_specs=in_specs,
        out_specs=pl.BlockSpec((tm, D), lambda i: (i, 0)),
        compiler_params=_params("parallel"),
        name="combine",
    )(*args)


def moe(h, gain, wg, bg, we, be, w_gate, w_up, w_down, final_gain=None):
    S = h.shape[0]
    hn, eid, slot_w, counts = router(h, gain, wg, bg, we, be)
    blk_e, blk_n, blk_start = block_plan(counts, S * TOP_K)
    dest = assignment_dest(eid, blk_start)[:, :TOP_K].reshape(S * TOP_K)
    y_slots = experts(hn, blk_e, blk_n, dest, w_gate, w_up, w_down)
    return combine(h, y_slots, slot_w, final_gain)


def kernel(x, mem, positions, mix_norm, w_in, shift_mu, decay_w0, decay_w2, aaa_a0, aaa_a2, gate_g2, k_k, k_a,
           r_k, lnx_w, lnx_b, diff_lambda, subln_w, w_out, mem_q_norm, mem_kv_norm, wq_mem, wk_mem, wv_mem,
           wo_mem, moe_norm, router_group_w, router_group_b, router_expert_w, router_expert_b, expert_gate,
           expert_up, expert_down, final_norm):
    B, S, D = x.shape
    depth = w_in.shape[0]
    outs = []
    for b in range(B):
        h = x[b]
        memb = mem[b]
        for l in range(depth):
            lambda_init = 0.8 - 0.6 * math.exp(-0.3 * l)
            w_in_b = w_in[l].astype(BF16)
            proj_r = matmul(h, w_in_b[:, :RWKV_COLS], gain=mix_norm[l], tm=1024, tn=RWKV_COLS // 2)
            qkv = diff_proj(h, mix_norm[l], w_in_b[:, RWKV_COLS:], positions[b])
            pre = rwkv_prep(proj_r, shift_mu[l], decay_w0[l], decay_w2[l], aaa_a0[l], aaa_a2[l], gate_g2[l],
                            k_k[l], k_a[l], r_k[l])
            y_rwkv = rwkv_scan(*pre, lnx_w[l], lnx_b[l])
            y_diff = diff_flash(qkv, diff_lambda[l], subln_w[l], lambda_init)
            h = matmul([y_rwkv, y_diff], w_out[l].astype(BF16), residual=h, tm=1024, tn=1024)
            km = matmul(memb, wk_mem[l].astype(BF16), gain=mem_kv_norm[l], out_dtype=BF16)
            vm = matmul(memb, wv_mem[l].astype(BF16), gain=mem_kv_norm[l], out_dtype=BF16)
            o = mem_attn(h, mem_q_norm[l], wq_mem[l].astype(BF16), km, vm)
            h = matmul(o, wo_mem[l].astype(BF16), residual=h, tm=1024, tn=1024)
            h = moe(h, moe_norm[l], router_group_w[l], router_group_b[l], router_expert_w[l], router_expert_b[l],
                    expert_gate[l], expert_up[l], expert_down[l], final_norm if l == depth - 1 else None)
        outs.append(h)
    return jnp.stack(outs, axis=0)
```

```python
import functools
import math

import numpy as np
import jax
import jax.numpy as jnp
from jax import lax
from jax.experimental import pallas as pl
from jax.experimental.pallas import tpu as pltpu

F32 = jnp.float32
BF16 = jnp.bfloat16
HIGHEST = lax.Precision.HIGHEST

RWKV_HEADS = 16
HEAD_N = 64
RWKV_WIDTH = RWKV_HEADS * HEAD_N
DECAY_LORA = 64
AAA_LORA = 64
GATE_LORA = 128
RWKV_COLS = 3 * RWKV_WIDTH + DECAY_LORA + AAA_LORA + GATE_LORA
DIFF_HEADS = 8
DIFF_QK = 64
DIFF_V = 128
DIFF_WIDTH = DIFF_HEADS * DIFF_V
ROPE_THETA = 10000.0
MEM_HEADS = 4
N_GROUPS = 8
EXPERTS_PER_GROUP = 8
N_EXPERTS = N_GROUPS * EXPERTS_PER_GROUP
TOP_K = 2
NORM_EPS = 1e-6
LNX_EPS = 64e-5
SUBLN_EPS = 1e-5

LANES = 128
MXU_WIDTH = 256
VMEM_LIMIT = 56 * 1024 * 1024

CHUNK = 64
SCAN_ROWS = 256
SCAN_WIDTH = 512
MOE_ROWS = 512
MOE_UNIT = 128
MOE_SPLIT = 2
MOE_DMA_UNROLL = 8
MOE_WEIGHT_SLOTS = 2


def _params(*sem):
    return pltpu.CompilerParams(dimension_semantics=sem, vmem_limit_bytes=VMEM_LIMIT)


def _mm_kernel(*refs, n_x, has_norm, has_res):
    it = iter(refs)
    x_refs = [next(it) for _ in range(n_x)]
    g_ref = next(it) if has_norm else None
    w_ref = next(it)
    r_ref = next(it) if has_res else None
    o_ref = next(it)
    xs_ref = next(it)

    @pl.when(pl.program_id(1) == 0)
    def _():
        if has_norm:
            x = x_refs[0][...].astype(F32)
            x = x * lax.rsqrt(jnp.mean(x * x, axis=-1, keepdims=True) + NORM_EPS) * g_ref[...]
            xs_ref[...] = x.astype(BF16)
        else:
            k0 = 0
            for x_ref in x_refs:
                xs_ref[:, k0:k0 + x_ref.shape[1]] = x_ref[...].astype(BF16)
                k0 += x_ref.shape[1]

    acc = jnp.dot(xs_ref[...], w_ref[...], preferred_element_type=F32)
    if has_res:
        acc = acc + r_ref[...]
    o_ref[...] = acc.astype(o_ref.dtype)


def matmul(x, w, *, gain=None, residual=None, out_dtype=F32, tm=512, tn=512):
    xs = list(x) if isinstance(x, (list, tuple)) else [x]
    assert gain is None or len(xs) == 1
    M = xs[0].shape[0]
    K, N = w.shape
    assert sum(p.shape[1] for p in xs) == K
    tm = min(tm, M)
    assert M % tm == 0 and N % tn == 0, (M, N, tm, tn)
    in_specs = [pl.BlockSpec((tm, p.shape[1]), lambda i, j: (i, 0)) for p in xs]
    args = list(xs)
    if gain is not None:
        in_specs.append(pl.BlockSpec((1, K), lambda i, j: (0, 0)))
        args.append(gain.reshape(1, K).astype(F32))
    in_specs.append(pl.BlockSpec((K, tn), lambda i, j: (0, j)))
    args.append(w)
    if residual is not None:
        in_specs.append(pl.BlockSpec((tm, tn), lambda i, j: (i, j)))
        args.append(residual)
    return pl.pallas_call(
        functools.partial(_mm_kernel, n_x=len(xs), has_norm=gain is not None, has_res=residual is not None),
        out_shape=jax.ShapeDtypeStruct((M, N), out_dtype),
        grid=(M // tm, N // tn),
        in_specs=in_specs,
        out_specs=pl.BlockSpec((tm, tn), lambda i, j: (i, j)),
        scratch_shapes=[pltpu.VMEM((tm, K), BF16)],
        compiler_params=_params("parallel", "arbitrary"),
        name="matmul",
    )(*args)


def _sigmoid(x):
    return 1.0 / (1.0 + jnp.exp(-x))


def _rwkv_prep_kernel(p_ref, pp_ref, mu_ref, w0_ref, w2_ref, a0_ref, a2_ref, g2_ref, kk_ref, ka_ref, rk_ref,
                      hsum_ref, at_o, bt_o, kt_o, rt_o, v_o, g_o, bonus_o, pc_o):
    W = RWKV_WIDTH
    C = CHUNK
    p = p_ref[...]
    last = jnp.where(pl.program_id(0) == 0, 0.0, pp_ref[7:8, :])
    prev = pltpu.roll(p, 1, axis=0)
    row = lax.broadcasted_iota(jnp.int32, p.shape, 0)
    prev = jnp.where(row == 0, last, prev)
    ps = p + (prev - p) * mu_ref[...]
    r = ps[:, 0:W]
    k = ps[:, W:2 * W]
    v = ps[:, 2 * W:3 * W]
    o = 3 * W
    wd = ps[:, o:o + DECAY_LORA]
    ad = ps[:, o + DECAY_LORA:o + DECAY_LORA + AAA_LORA]
    gd = ps[:, o + DECAY_LORA + AAA_LORA:o + DECAY_LORA + AAA_LORA + GATE_LORA]
    z = w0_ref[...] + jnp.dot(jnp.tanh(wd).astype(BF16), w2_ref[...], preferred_element_type=F32)
    nz = -z
    softplus = jnp.maximum(nz, 0.0) + jnp.log(1.0 + jnp.exp(-jnp.abs(nz)))
    w_raw = -softplus - 0.5
    lw = -jnp.exp(w_raw)
    tm = p.shape[0]
    rr = lax.broadcasted_iota(jnp.int32, (tm, tm), 0)
    cc = lax.broadcasted_iota(jnp.int32, (tm, tm), 1)
    sh = C.bit_length() - 1
    tri = (((rr >> sh) == (cc >> sh)) & (rr >= cc)).astype(F32)
    cs = _dot(tri, lw, HIGHEST)
    a = _sigmoid(a0_ref[...] + jnp.dot(ad.astype(BF16), a2_ref[...], preferred_element_type=F32))
    g_o[...] = jnp.dot(_sigmoid(gd).astype(BF16), g2_ref[...], preferred_element_type=F32)

    def head_sum(x):
        hi = x.astype(BF16)
        lo = (x - hi.astype(F32)).astype(BF16)
        ones = hsum_ref[...]
        wb = ones.shape[0]
        cols = [slice(c, c + wb) for c in range(0, x.shape[1], wb)]
        return jnp.concatenate([_dot(hi[:, cs], ones) + _dot(lo[:, cs], ones) for cs in cols], axis=1)

    kk = k * kk_ref[...]
    kk = kk / jnp.maximum(jnp.sqrt(head_sum(kk * kk)), 1e-12)
    k2 = k * (1.0 + (a - 1.0) * ka_ref[...])
    e_neg = jnp.exp(-cs)
    at_o[...] = (-kk * jnp.exp(cs - lw)).astype(at_o.dtype)
    bt_o[...] = (kk * a) * e_neg
    kt_o[...] = k2 * e_neg
    rt_o[...] = r * jnp.exp(cs)
    v_o[...] = v
    bonus_o[...] = head_sum(r * k2 * rk_ref[...]) * v
    for q in range(tm // C):
        pc_o[q] = jnp.exp(cs[q * C + C - 1:q * C + C, :])


def rwkv_prep(p, mu, w0, w2, a0, a2, g2, k_k, k_a, r_k, *, tm=256):
    S = p.shape[0]
    tm = min(tm, S)
    assert tm % CHUNK == 0
    W = RWKV_WIDTH
    row = lambda x: x.reshape(1, -1).astype(F32)
    full = lambda a: pl.BlockSpec(a.shape, lambda i: (0, 0))
    head = jnp.arange(2 * LANES, dtype=jnp.int32) // HEAD_N
    hsum = (head[:, None] == head[None, :]).astype(BF16)
    args = [p, p, row(mu), row(w0), w2.astype(BF16), row(a0), a2.astype(BF16), g2.astype(BF16),
            row(k_k), row(k_a), row(r_k), hsum]
    in_specs = [pl.BlockSpec((tm, RWKV_COLS), lambda i: (i, 0)),
                pl.BlockSpec((8, RWKV_COLS), lambda i: (jnp.maximum(i * (tm // 8) - 1, 0), 0))]
    in_specs += [full(a) for a in args[2:]]
    f32 = jax.ShapeDtypeStruct((S, W), F32)
    blk = pl.BlockSpec((tm, W), lambda i: (i, 0))
    return pl.pallas_call(
        _rwkv_prep_kernel,
        out_shape=[jax.ShapeDtypeStruct((S, W), BF16)] + [f32] * 6 + [jax.ShapeDtypeStruct((S // CHUNK, 1, W), F32)],
        grid=(S // tm,),
        in_specs=in_specs,
        out_specs=[blk] * 7 + [pl.BlockSpec((tm // CHUNK, 1, W), lambda i: (i, 0, 0))],
        compiler_params=_params("parallel"),
        name="rwkv_prep",
    )(*args)


def _dot_nt(a, b, precision=None):
    return lax.dot_general(a, b, (((1,), (1,)), ((), ())), preferred_element_type=F32, precision=precision)


def _dot_tn(a, b, precision=None):
    return lax.dot_general(a, b, (((0,), (0,)), ((), ())), preferred_element_type=F32, precision=precision)


def _dot(a, b, precision=None):
    return jnp.dot(a, b, preferred_element_type=F32, precision=precision)


def _rwkv_scan_kernel(at_ref, bt_ref, kt_ref, rt_ref, v_ref, g_ref, bonus_ref, pc_ref, lnw_ref, lnb_ref,
                      y_ref, state_ref, yf_ref):
    C = CHUNK
    N = HEAD_N

    @pl.when(pl.program_id(1) == 0)
    def _():
        state_ref[...] = jnp.zeros_like(state_ref)

    row = lax.broadcasted_iota(jnp.int32, (C, C), 0)
    col = lax.broadcasted_iota(jnp.int32, (C, C), 1)
    eye = (row == col).astype(F32)
    strict = row > col
    incl = row >= col

    n_chunks = rt_ref.shape[0] // C
    heads = rt_ref.shape[1] // N
    items = [(j, q) for j in range(heads) for q in range(n_chunks)]
    G = range(len(items))

    def tile(ref, j, q):
        return ref[q * C:(q + 1) * C, j * N:(j + 1) * N]

    def par(ref, j):
        return ref[:, j * N:(j + 1) * N]

    at = [tile(at_ref, j, q) for j, q in items]
    bt = [tile(bt_ref, j, q) for j, q in items]
    kt = [tile(kt_ref, j, q) for j, q in items]
    rt = [tile(rt_ref, j, q) for j, q in items]
    v = [tile(v_ref, j, q) for j, q in items]
    pc = [pc_ref[q, :, j * N:(j + 1) * N] for j, q in items]
    vb = [x.astype(BF16) for x in v]
    btb = [x.astype(BF16) for x in bt]
    ktb = [x.astype(BF16) for x in kt]
    rtb = [x.astype(BF16) for x in rt]
    n_ab = [jnp.where(strict, _dot_nt(at[g], btb[g]), 0.0) for g in G]
    a_ak = [jnp.where(strict, _dot_nt(at[g], ktb[g]), 0.0).astype(BF16) for g in G]
    a_rb = [jnp.where(incl, _dot_nt(rtb[g], btb[g]), 0.0).astype(BF16) for g in G]
    a_rk = [jnp.where(incl, _dot_nt(rtb[g], ktb[g]), 0.0).astype(BF16) for g in G]
    akv = [_dot(a_ak[g], vb[g]) for g in G]
    t = None
    b = 1
    while b < C:
        sh = (2 * b).bit_length() - 1
        low_left = ((row >> sh) == (col >> sh)) & ((row & b) != 0) & ((col & b) == 0)
        nb = [jnp.where(low_left, n_ab[g], 0.0) for g in G]
        if b == 1:
            t = [eye + nb[g] for g in G]
        else:
            tb = [t[g].astype(BF16) for g in G]
            z = [_dot(nb[g].astype(BF16), tb[g]).astype(BF16) for g in G]
            t = [t[g] + _dot(tb[g], z[g]) for g in G]
        b *= 2
    tb = [t[g].astype(BF16) for g in G]
    wm = [_dot(tb[g], at[g]).astype(BF16) for g in G]
    u0 = [_dot(tb[g], akv[g].astype(BF16)).astype(BF16) for g in G]
    rm = [(rt[g] + _dot(a_rb[g], wm[g])).astype(BF16) for g in G]
    y0 = [_dot(a_rb[g], u0[g]) + _dot(a_rk[g], vb[g]) for g in G]
    bp = [(bt[g] * pc[g]).astype(BF16) for g in G]
    kp = [(kt[g] * pc[g]).astype(BF16) for g in G]
    mp = [_dot_tn(wm[g], bp[g]).astype(BF16) for g in G]
    s_add = [_dot_tn(u0[g], bp[g]) + _dot_tn(vb[g], kp[g]) for g in G]

    for j in range(heads):
        s = state_ref[j]
        for q in range(n_chunks):
            g = j * n_chunks + q
            sb = s.astype(BF16)
            y = _dot_nt(rm[g], sb) + y0[g]
            s = s * pc[g] + _dot(sb, mp[g]) + s_add[g]
            mean = jnp.mean(y, axis=-1, keepdims=True)
            yc = y - mean
            var = jnp.mean(yc * yc, axis=-1, keepdims=True)
            yn = yc * lax.rsqrt(var + LNX_EPS) * par(lnw_ref, j) + par(lnb_ref, j)
            yf_ref[q * C:(q + 1) * C, j * N:(j + 1) * N] = (yn + tile(bonus_ref, j, q)) * tile(g_ref, j, q)
        state_ref[j] = s
    y_ref[...] = yf_ref[...].astype(y_ref.dtype)


def rwkv_scan(at, bt, kt, rt, v, g, bonus, pc, lnx_w, lnx_b, *, rows=SCAN_ROWS, width=SCAN_WIDTH):
    S, W = rt.shape
    rows = min(rows, S)
    hp = width // HEAD_N
    row = lambda x: x.reshape(1, W).astype(F32)
    seq = pl.BlockSpec((rows, width), lambda h, c: (c, h))
    par = pl.BlockSpec((1, width), lambda h, c: (0, h))
    return pl.pallas_call(
        _rwkv_scan_kernel,
        out_shape=jax.ShapeDtypeStruct((S, W), BF16),
        grid=(W // width, S // rows),
        in_specs=[seq] * 7 + [pl.BlockSpec((rows // CHUNK, 1, width), lambda h, c: (c, 0, h))] + [par] * 2,
        out_specs=seq,
        scratch_shapes=[pltpu.VMEM((hp, HEAD_N, HEAD_N), F32), pltpu.VMEM((rows, width), F32)],
        compiler_params=_params("parallel", "arbitrary"),
        name="rwkv_scan",
    )(at, bt, kt, rt, v, g, bonus, pc, row(lnx_w), row(lnx_b))


def _diff_proj_kernel(x_ref, g_ref, w_ref, pos_ref, freq_ref, o_ref, xs_ref, cos_ref, sin_ref):
    j = pl.program_id(1)
    half = DIFF_QK // 2

    def first_half(shape):
        return (lax.broadcasted_iota(jnp.int32, shape, 1) & (DIFF_QK - 1)) < half

    @pl.when(j == 0)
    def _():
        x = x_ref[...]
        x = x * lax.rsqrt(jnp.mean(x * x, axis=-1, keepdims=True) + NORM_EPS) * g_ref[...]
        xs_ref[...] = x.astype(BF16)
        ang = pos_ref[...] * freq_ref[...]
        cos_ref[...] = jnp.cos(ang)
        sin_ref[...] = jnp.where(first_half(ang.shape), -1.0, 1.0) * jnp.sin(ang)

    @pl.when(j < 2)
    def _():
        sc = jnp.where(j == 0, DIFF_QK ** -0.5, 1.0)
        cos = cos_ref[...]
        sin = sin_ref[...]
        fh = first_half(cos.shape)
        for c0 in range(0, o_ref.shape[1], MXU_WIDTH):
            acc = jnp.dot(xs_ref[...], w_ref[:, c0:c0 + MXU_WIDTH], preferred_element_type=F32)
            for b0 in range(0, MXU_WIDTH, LANES):
                x = acc[:, b0:b0 + LANES]
                partner = jnp.where(fh, pltpu.roll(x, LANES - half, axis=1), pltpu.roll(x, half, axis=1))
                o_ref[:, c0 + b0:c0 + b0 + LANES] = ((x * cos + partner * sin) * sc).astype(o_ref.dtype)

    @pl.when(j == 2)
    def _():
        o_ref[...] = jnp.dot(xs_ref[...], w_ref[...], preferred_element_type=F32).astype(o_ref.dtype)


def diff_proj(x, gain, w, positions, *, tm=1024):
    S, K = x.shape
    W = DIFF_WIDTH
    tm = min(tm, S)
    assert w.shape == (K, 3 * W) and S % tm == 0
    inv_freq = ROPE_THETA ** (-(jnp.arange(0, DIFF_QK, 2, dtype=F32) / DIFF_QK))
    freq = jnp.tile(inv_freq, LANES // (DIFF_QK // 2)).reshape(1, LANES)
    pos = positions.reshape(S, 1).astype(F32)
    return pl.pallas_call(
        _diff_proj_kernel,
        out_shape=jax.ShapeDtypeStruct((S, 3 * W), BF16),
        grid=(S // tm, 3),
        in_specs=[pl.BlockSpec((tm, K), lambda i, j: (i, 0)),
                  pl.BlockSpec((1, K), lambda i, j: (0, 0)),
                  pl.BlockSpec((K, W), lambda i, j: (0, j)),
                  pl.BlockSpec((tm, 1), lambda i, j: (i, 0)),
                  pl.BlockSpec((1, LANES), lambda i, j: (0, 0))],
        out_specs=pl.BlockSpec((tm, W), lambda i, j: (i, j)),
        scratch_shapes=[pltpu.VMEM((tm, K), BF16), pltpu.VMEM((tm, LANES), F32), pltpu.VMEM((tm, LANES), F32)],
        compiler_params=_params("parallel", "arbitrary"),
        name="diff_proj",
    )(x, gain.reshape(1, K).astype(F32), w, pos, freq)


def _diff_flash_kernel(qi_tab, ki_tab, q_ref, k_ref, v_ref, lam_ref, sw_ref, o_ref,
                       m_ref, l_ref, acc_ref, s_ref, p_ref, alpha_ref, *, tile, row_block, diag_parts, lambda_init):
    p = pl.program_id(1)
    qi = qi_tab[p]
    ki = ki_tab[p]

    @pl.when(ki == 0)
    def _():
        m_ref[...] = jnp.full_like(m_ref, -jnp.inf)
        l_ref[...] = jnp.zeros_like(l_ref)
        acc_ref[...] = jnp.zeros_like(acc_ref)

    def process(row0, nrows, kmax, masked):
        rows = slice(row0, row0 + nrows)
        q = q_ref[rows, :]
        k = k_ref[0:kmax, :]
        v = v_ref[0:kmax, :]
        lane = lax.broadcasted_iota(jnp.int32, q.shape, 1)
        reps = kmax // LANES
        for c in range(2):
            qc = jnp.where((lane >> (DIFF_QK.bit_length() - 1)) == c, q, jnp.zeros_like(q))
            s_ref[0:nrows, 0:kmax] = _dot_nt(qc, k)
            for r0 in range(0, nrows, row_block):
                ar = slice(row0 + r0, row0 + r0 + row_block)
                lr = slice(r0, r0 + row_block)
                s = s_ref[lr, 0:kmax]
                if masked:
                    qpos = lax.broadcasted_iota(jnp.int32, s.shape, 0) + (row0 + r0)
                    kpos = lax.broadcasted_iota(jnp.int32, s.shape, 1)
                    s = jnp.where(kpos <= qpos, s, -jnp.inf)
                m_old = m_ref[c, ar, :]
                m_new = jnp.maximum(m_old, jnp.max(s, axis=-1, keepdims=True))
                alpha = jnp.exp(m_old - m_new)
                pr = jnp.exp(s - jnp.concatenate([m_new] * reps, axis=1))
                l_ref[c, ar, :] = alpha * l_ref[c, ar, :] + jnp.sum(pr, axis=-1, keepdims=True)
                m_ref[c, ar, :] = m_new
                alpha_ref[lr, :] = alpha
                p_ref[lr, 0:kmax] = pr.astype(BF16)
            acc_ref[c, rows, :] = (alpha_ref[0:nrows, :] * acc_ref[c, rows, :]
                                   + _dot(p_ref[0:nrows, 0:kmax], v))

    @pl.when(ki < qi)
    def _():
        process(0, tile, tile, False)

    @pl.when(ki == qi)
    def _():
        part = tile // diag_parts
        for a in range(diag_parts):
            process(a * part, part, (a + 1) * part, True)
        lp = lam_ref[...]
        lam = (jnp.exp(jnp.sum(lp[0:1] * lp[1:2], axis=-1, keepdims=True))
               - jnp.exp(jnp.sum(lp[2:3] * lp[3:4], axis=-1, keepdims=True)) + lambda_init)
        o = acc_ref[0] / l_ref[0] - lam * (acc_ref[1] / l_ref[1])
        o = o * lax.rsqrt(jnp.mean(o * o, axis=-1, keepdims=True) + SUBLN_EPS)
        o_ref[...] = (o * sw_ref[...] * (1.0 - lambda_init)).astype(o_ref.dtype)


def diff_flash(qkv, lambdas, subln_w, lambda_init, *, tile=1024, row_block=32, diag_parts=2):
    S = qkv.shape[0]
    tile = min(tile, S)
    assert S % tile == 0 and (tile // diag_parts) % LANES == 0 and (tile // diag_parts) % row_block == 0
    pairs = [(qi, ki) for qi in range(S // tile) for ki in range(qi + 1)]
    qi_tab = jnp.asarray([p[0] for p in pairs], jnp.int32)
    ki_tab = jnp.asarray([p[1] for p in pairs], jnp.int32)
    grid_spec = pltpu.PrefetchScalarGridSpec(
        num_scalar_prefetch=2,
        grid=(DIFF_HEADS, len(pairs)),
        in_specs=[pl.BlockSpec((tile, LANES), lambda h, p, qt, kt: (qt[p], h)),
                  pl.BlockSpec((tile, LANES), lambda h, p, qt, kt: (kt[p], DIFF_HEADS + h)),
                  pl.BlockSpec((tile, LANES), lambda h, p, qt, kt: (kt[p], 2 * DIFF_HEADS + h)),
                  pl.BlockSpec((4, DIFF_QK), lambda h, p, qt, kt: (0, 0)),
                  pl.BlockSpec((1, DIFF_V), lambda h, p, qt, kt: (0, 0))],
        out_specs=pl.BlockSpec((tile, LANES), lambda h, p, qt, kt: (qt[p], h)),
        scratch_shapes=[pltpu.VMEM((2, tile, LANES), F32),
                        pltpu.VMEM((2, tile, LANES), F32),
                        pltpu.VMEM((2, tile, DIFF_V), F32),
                        pltpu.VMEM((tile, tile), F32),
                        pltpu.VMEM((tile, tile), BF16),
                        pltpu.VMEM((tile, LANES), F32)],
    )
    return pl.pallas_call(
        functools.partial(_diff_flash_kernel, tile=tile, row_block=row_block, diag_parts=diag_parts,
                          lambda_init=lambda_init),
        out_shape=jax.ShapeDtypeStruct((S, DIFF_WIDTH), BF16),
        grid_spec=grid_spec,
        compiler_params=_params("parallel", "arbitrary"),
        name="diff_flash",
    )(qi_tab, ki_tab, qkv, qkv, qkv, lambdas.astype(F32), subln_w.reshape(1, DIFF_V).astype(F32))


def _mem_attn_kernel(x_ref, g_ref, w_ref, k_ref, v_ref, o_ref, xs_ref):
    @pl.when(pl.program_id(1) == 0)
    def _():
        x = x_ref[...]
        x = x * lax.rsqrt(jnp.mean(x * x, axis=-1, keepdims=True) + NORM_EPS) * g_ref[...]
        xs_ref[...] = x.astype(BF16)

    q = jnp.dot(xs_ref[...], w_ref[...], preferred_element_type=F32).astype(BF16)
    s = _dot_nt(q, k_ref[...]) * (q.shape[1] ** -0.5)
    s = s - jnp.max(s, axis=-1, keepdims=True)
    e = jnp.exp(s)
    pr = e / jnp.sum(e, axis=-1, keepdims=True)
    o_ref[...] = _dot(pr.astype(BF16), v_ref[...]).astype(o_ref.dtype)


def mem_attn(x, gain, wq, k, v, *, tm=1024):
    S, D = x.shape
    M = k.shape[0]
    hd = D // MEM_HEADS
    tm = min(tm, S)
    return pl.pallas_call(
        _mem_attn_kernel,
        out_shape=jax.ShapeDtypeStruct((S, D), BF16),
        grid=(S // tm, MEM_HEADS),
        in_specs=[pl.BlockSpec((tm, D), lambda i, j: (i, 0)),
                  pl.BlockSpec((1, D), lambda i, j: (0, 0)),
                  pl.BlockSpec((D, hd), lambda i, j: (0, j)),
                  pl.BlockSpec((M, hd), lambda i, j: (0, j)),
                  pl.BlockSpec((M, hd), lambda i, j: (0, j))],
        out_specs=pl.BlockSpec((tm, hd), lambda i, j: (i, j)),
        scratch_shapes=[pltpu.VMEM((tm, D), BF16)],
        compiler_params=_params("parallel", "arbitrary"),
        name="mem_attn",
    )(x, gain.reshape(1, D).astype(F32), wq, k, v)


def _first_argmax(x, lane, big):
    m = jnp.max(x, axis=-1, keepdims=True)
    idx = jnp.min(jnp.where(x == m, lane, big), axis=-1, keepdims=True)
    return m, idx


def _router_kernel(h_ref, gain_ref, wg_ref, bg_ref, we_ref, be_ref, hn_ref, eid_ref, ew_ref, cnt_ref):
    @pl.when(pl.program_id(0) == 0)
    def _():
        cnt_ref[...] = jnp.zeros_like(cnt_ref)

    x = h_ref[...]
    hn = x * lax.rsqrt(jnp.mean(x * x, axis=-1, keepdims=True) + NORM_EPS) * gain_ref[...]
    hn_ref[...] = hn
    hn_hi = hn.astype(BF16)
    hn_lo = (hn - hn_hi.astype(F32)).astype(BF16)

    def logits(w_ref, b_ref):
        w = w_ref[...]
        w_hi = w.astype(BF16)
        w_lo = (w - w_hi.astype(F32)).astype(BF16)
        return _dot(hn_hi, w_hi) + (_dot(hn_hi, w_lo) + _dot(hn_lo, w_hi)) + b_ref[...]
    g_logits = logits(wg_ref, bg_ref)
    e_logits = logits(we_ref, be_ref)
    tm = x.shape[0]
    lane_g = lax.broadcasted_iota(jnp.int32, (tm, N_GROUPS), 1)
    g_max, g_idx = _first_argmax(g_logits, lane_g, N_GROUPS)
    g_w = 1.0 / jnp.sum(jnp.exp(g_logits - g_max), axis=-1, keepdims=True)
    lane_e = lax.broadcasted_iota(jnp.int32, (tm, N_EXPERTS), 1)
    in_group = (lane_e // EXPERTS_PER_GROUP) == g_idx
    el = jnp.where(in_group, e_logits, -jnp.inf)
    e_max = jnp.max(el, axis=-1, keepdims=True)
    ex = jnp.exp(el - e_max)
    prob = ex / jnp.sum(ex, axis=-1, keepdims=True)
    prob = jnp.where(in_group, prob, -1.0)
    p1, i1 = _first_argmax(prob, lane_e, N_EXPERTS)
    p2, i2 = _first_argmax(jnp.where(lane_e == i1, -1.0, prob), lane_e, N_EXPERTS)
    tot = p1 + p2
    lane_o = lax.broadcasted_iota(jnp.int32, (tm, LANES), 1)
    eid_ref[...] = jnp.where(lane_o == 0, i1, jnp.where(lane_o == 1, i2, 0))
    ew_ref[...] = jnp.where(lane_o == 0, g_w * (p1 / tot), jnp.where(lane_o == 1, g_w * (p2 / tot), 0.0))
    chosen = ((lane_o == i1) | (lane_o == i2)).astype(F32)
    cnt_ref[...] = cnt_ref[...] + jnp.sum(chosen, axis=0, keepdims=True).astype(jnp.int32)


def router(h, gain, wg, bg, we, be, *, tm=512):
    S, D = h.shape
    tm = min(tm, S)
    full = lambda a: pl.BlockSpec(a.shape, lambda i: (0, 0))
    args = [h, gain.reshape(1, D).astype(F32), wg.astype(F32), bg.reshape(1, -1).astype(F32),
            we.astype(F32), be.reshape(1, -1).astype(F32)]
    return pl.pallas_call(
        _router_kernel,
        out_shape=[jax.ShapeDtypeStruct((S, D), F32),
                   jax.ShapeDtypeStruct((S, LANES), jnp.int32),
                   jax.ShapeDtypeStruct((S, LANES), F32),
                   jax.ShapeDtypeStruct((1, LANES), jnp.int32)],
        grid=(S // tm,),
        in_specs=[pl.BlockSpec((tm, D), lambda i: (i, 0))] + [full(a) for a in args[1:]],
        out_specs=[pl.BlockSpec((tm, D), lambda i: (i, 0)),
                   pl.BlockSpec((tm, LANES), lambda i: (i, 0)),
                   pl.BlockSpec((tm, LANES), lambda i: (i, 0)),
                   pl.BlockSpec((1, LANES), lambda i: (0, 0))],
        compiler_params=_params("arbitrary"),
        name="router",
    )(*args)


def _plan_kernel(cnt_ref, blk_e_ref, blk_n_ref, start_ref):
    nb = blk_e_ref.shape[0]
    sh = MOE_ROWS.bit_length() - 1

    def per_expert(e, b):
        n = cnt_ref[e]
        start_ref[e] = b

        def per_block(t, _):
            blk_e_ref[b + t] = e
            blk_n_ref[b + t] = jnp.minimum(n - t * MOE_ROWS, MOE_ROWS)
            return 0
        k = (n + (MOE_ROWS - 1)) >> sh
        lax.fori_loop(0, k, per_block, 0)
        return b + k
    total = lax.fori_loop(0, N_EXPERTS, per_expert, 0)

    def rest(e, _):
        start_ref[e] = total
        return 0
    lax.fori_loop(N_EXPERTS, start_ref.shape[0], rest, 0)
    last_e = blk_e_ref[jnp.maximum(total - 1, 0)]

    def tail(b, _):
        blk_e_ref[b] = last_e
        blk_n_ref[b] = 0
        return 0
    lax.fori_loop(total, nb, tail, 0)


def block_plan(counts, n_assign):
    nb = (n_assign + N_EXPERTS * (MOE_ROWS - 1)) // MOE_ROWS
    smem = pl.BlockSpec(memory_space=pltpu.SMEM)
    return pl.pallas_call(
        _plan_kernel,
        out_shape=[jax.ShapeDtypeStruct((nb,), jnp.int32), jax.ShapeDtypeStruct((nb,), jnp.int32),
                   jax.ShapeDtypeStruct((LANES,), jnp.int32)],
        in_specs=[smem],
        out_specs=[smem, smem, smem],
        name="block_plan",
    )(counts.reshape(LANES))


def _dest_kernel(eid_ref, start_ref, dest_ref, carry_ref):
    @pl.when(pl.program_id(0) == 0)
    def _():
        carry_ref[...] = jnp.zeros_like(carry_ref)

    eid = eid_ref[...]
    tm = eid.shape[0]
    lane = lax.broadcasted_iota(jnp.int32, (tm, LANES), 1)
    oh0 = lane == eid[:, 0:1]
    oh1 = lane == eid[:, 1:2]
    both = jnp.where(oh0 | oh1, 1.0, 0.0).astype(BF16)
    r = lax.broadcasted_iota(jnp.int32, (tm, tm), 0)
    c = lax.broadcasted_iota(jnp.int32, (tm, tm), 1)
    before = _dot(jnp.where(r > c, 1.0, 0.0).astype(BF16), both) + carry_ref[...]
    pos = before + (start_ref[...] * MOE_ROWS).astype(F32)
    d0 = jnp.sum(jnp.where(oh0, pos, 0.0), axis=-1, keepdims=True).astype(jnp.int32)
    d1 = jnp.sum(jnp.where(oh1, pos, 0.0), axis=-1, keepdims=True).astype(jnp.int32)
    dest_ref[...] = jnp.where(lane == 0, d0, jnp.where(lane == 1, d1, 0))
    carry_ref[...] = carry_ref[...] + jnp.sum(both.astype(F32), axis=0, keepdims=True)


def assignment_dest(eid, blk_start, *, tm=512):
    S = eid.shape[0]
    tm = min(tm, S)
    return pl.pallas_call(
        _dest_kernel,
        out_shape=jax.ShapeDtypeStruct((S, LANES), jnp.int32),
        grid=(S // tm,),
        in_specs=[pl.BlockSpec((tm, LANES), lambda i: (i, 0)), pl.BlockSpec((1, LANES), lambda i: (0, 0))],
        out_specs=pl.BlockSpec((tm, LANES), lambda i: (i, 0)),
        scratch_shapes=[pltpu.VMEM((1, LANES), F32)],
        compiler_params=_params("arbitrary"),
        name="assignment_dest",
    )(eid, blk_start.reshape(1, LANES))


def _for_rows(n, fn):
    sh = MOE_DMA_UNROLL.bit_length() - 1

    def group(t, _):
        for u in range(MOE_DMA_UNROLL):
            fn(t * MOE_DMA_UNROLL + u)
        return 0
    lax.fori_loop(0, n >> sh, group, 0)

    def one(r, _):
        fn(r)
        return 0
    lax.fori_loop((n >> sh) << sh, n, one, 0)


def _experts_kernel(blk_e, blk_n, dest, hn_hbm, wg_hbm, wu_hbm, wd_hbm, out_hbm,
                    xg_ref, xb_ref, acc_ref, wg_ring, wu_ring, wd_ring, wgb_ref, wub_ref, wdb_ref, inv_ref,
                    gsem, ssem, wsem):
    i = pl.program_id(0)
    j = pl.program_id(1)
    nb = pl.num_programs(0)
    nsplit = pl.num_programs(1)
    n_tok = hn_hbm.shape[0]
    k_shift = TOP_K.bit_length() - 1
    de = wgb_ref.shape[1]
    n_slots = wg_ring.shape[0]

    def weight_copies(b, jj):
        slot = lax.rem(b * MOE_SPLIT + jj, n_slots)
        e = blk_e[b]
        c0 = pl.multiple_of(jj * de, de)
        return (pltpu.make_async_copy(wg_hbm.at[e, :, pl.ds(c0, de)], wg_ring.at[slot], wsem.at[slot]),
                pltpu.make_async_copy(wu_hbm.at[e, :, pl.ds(c0, de)], wu_ring.at[slot], wsem.at[slot]),
                pltpu.make_async_copy(wd_hbm.at[e, pl.ds(c0, de), :], wd_ring.at[slot], wsem.at[slot]))

    def start_weights(t):
        b = lax.div(t, MOE_SPLIT)
        jj = lax.rem(t, MOE_SPLIT)
        bb = jnp.minimum(b, nb - 1)

        @pl.when((b < nb) & (blk_n[bb] > 0))
        def _():
            for cp in weight_copies(bb, jj):
                cp.start()

    def gather(b, r):
        tok = inv_ref[b * MOE_ROWS + r] >> k_shift
        return pltpu.make_async_copy(hn_hbm.at[pl.ds(tok, 1)], xg_ref.at[pl.ds(r, 1)], gsem)

    def scatter(b, r):
        a = inv_ref[b * MOE_ROWS + r]
        row = (a & (TOP_K - 1)) * n_tok + (a >> k_shift)
        return pltpu.make_async_copy(acc_ref.at[pl.ds(r, 1)], out_hbm.at[pl.ds(row, 1)], ssem)

    def n_gather(b):
        return pl.multiple_of(((blk_n[b] + 7) >> 3) << 3, 8)

    def start_gather(b):
        def pad(r, _):
            inv_ref[b * MOE_ROWS + r] = 0
            return 0
        lax.fori_loop(blk_n[b], n_gather(b), pad, 0)
        _for_rows(n_gather(b), lambda r: gather(b, r).start(priority=1))

    def wait_gather(b):
        n = n_gather(b)

        @pl.when(n > 0)
        def _():
            pltpu.make_async_copy(hn_hbm.at[pl.ds(0, n)], xg_ref.at[pl.ds(0, n)], gsem).wait()

    def start_scatter(b):
        _for_rows(blk_n[b], lambda r: scatter(b, r).start(priority=1))

    def wait_scatter(b):
        n8 = pl.multiple_of((blk_n[b] >> 3) << 3, 8)

        @pl.when(n8 > 0)
        def _():
            pltpu.make_async_copy(acc_ref.at[pl.ds(0, n8)], out_hbm.at[pl.ds(0, n8)], ssem).wait()

        def one(r, _):
            scatter(b, r).wait()
            return 0
        lax.fori_loop(n8, blk_n[b], one, 0)

    n_units = (blk_n[i] + (MOE_UNIT - 1)) >> (MOE_UNIT.bit_length() - 1)

    @pl.when(j == 0)
    def _():
        @pl.when(i == 0)
        def _():
            def invert(a):
                inv_ref[dest[a]] = a
            _for_rows(dest.shape[0], invert)
            xg_ref[...] = jnp.zeros_like(xg_ref)
            start_gather(0)
        wait_gather(i)

        def cast(sb, _):
            rs = pl.ds(pl.multiple_of(sb * MOE_UNIT, MOE_UNIT), MOE_UNIT)
            xb_ref[rs, :] = xg_ref[rs, :].astype(BF16)
            return 0
        lax.fori_loop(0, n_units, cast, 0)

        @pl.when(i + 1 < nb)
        def _():
            start_gather(i + 1)

        @pl.when(i > 0)
        def _():
            wait_scatter(i - 1)

    t = i * MOE_SPLIT + j

    @pl.when(t == 0)
    def _():
        for t0 in range(n_slots):
            start_weights(jnp.int32(t0))

    @pl.when(n_units > 0)
    def _():
        for cp in weight_copies(i, j):
            cp.wait()
        slot = lax.rem(t, n_slots)
        wgb_ref[...] = wg_ring[slot].astype(BF16)
        wub_ref[...] = wu_ring[slot].astype(BF16)
        wdb_ref[...] = wd_ring[slot].astype(BF16)

    start_weights(t + n_slots)

    def compute(row0, nrows):
        rs = pl.ds(row0, nrows)
        xb = xb_ref[rs, :]
        gate = _dot(xb, wgb_ref[...])
        hid = (gate * _sigmoid(gate)) * _dot(xb, wub_ref[...])
        part = _dot(hid.astype(BF16), wdb_ref[...])

        @pl.when(j == 0)
        def _():
            acc_ref[rs, :] = part

        @pl.when(j > 0)
        def _():
            acc_ref[rs, :] = acc_ref[rs, :] + part

    for units in range(1, MOE_ROWS // MOE_UNIT + 1):
        @pl.when(n_units == units)
        def _():
            compute(0, units * MOE_UNIT)

    @pl.when(j == nsplit - 1)
    def _():
        start_scatter(i)

        @pl.when(i == nb - 1)
        def _():
            wait_scatter(i)


def experts(hn, blk_e, blk_n, dest, w_gate, w_up, w_down):
    S, D = hn.shape
    E, _, DE = w_gate.shape
    nb = blk_e.shape[0]
    de = DE // MOE_SPLIT
    any_space = pl.BlockSpec(memory_space=pl.ANY)
    grid_spec = pltpu.PrefetchScalarGridSpec(
        num_scalar_prefetch=3,
        grid=(nb, MOE_SPLIT),
        in_specs=[any_space] * 4,
        out_specs=any_space,
        scratch_shapes=[pltpu.VMEM((MOE_ROWS, D), F32),
                        pltpu.VMEM((MOE_ROWS, D), BF16),
                        pltpu.VMEM((MOE_ROWS, D), F32),
                        pltpu.VMEM((MOE_WEIGHT_SLOTS, D, de), F32),
                        pltpu.VMEM((MOE_WEIGHT_SLOTS, D, de), F32),
                        pltpu.VMEM((MOE_WEIGHT_SLOTS, de, D), F32),
                        pltpu.VMEM((D, de), BF16),
                        pltpu.VMEM((D, de), BF16),
                        pltpu.VMEM((de, D), BF16),
                        pltpu.SMEM((nb * MOE_ROWS,), jnp.int32),
                        pltpu.SemaphoreType.DMA(()),
                        pltpu.SemaphoreType.DMA(()),
                        pltpu.SemaphoreType.DMA((MOE_WEIGHT_SLOTS,))],
    )
    return pl.pallas_call(
        _experts_kernel,
        out_shape=jax.ShapeDtypeStruct((TOP_K * S, D), F32),
        grid_spec=grid_spec,
        compiler_params=_params("arbitrary", "arbitrary"),
        name="experts",
    )(blk_e, blk_n, dest, hn, w_gate, w_up, w_down)


def _combine_kernel(*refs, has_norm):
    h_ref, y0_ref, y1_ref, w_ref = refs[:4]
    o_ref = refs[-1]
    w = w_ref[...]
    x = h_ref[...] + (y0_ref[0] * w[:, 0:1] + y1_ref[0] * w[:, 1:2])
    if has_norm:
        x = x * lax.rsqrt(jnp.mean(x * x, axis=-1, keepdims=True) + NORM_EPS) * refs[4][...]
    o_ref[...] = x


def combine(h, y_slots, slot_w, gain=None, *, tm=512):
    S, D = h.shape
    tm = min(tm, S)
    y3 = y_slots.reshape(TOP_K, S, D)
    in_specs = [pl.BlockSpec((tm, D), lambda i: (i, 0)),
                pl.BlockSpec((1, tm, D), lambda i: (0, i, 0)),
                pl.BlockSpec((1, tm, D), lambda i: (1, i, 0)),
                pl.BlockSpec((tm, LANES), lambda i: (i, 0))]
    args = [h, y3, y3, slot_w]
    if gain is not None:
        in_specs.append(pl.BlockSpec((1, D), lambda i: (0, 0)))
        args.append(gain.reshape(1, D).astype(F32))
    return pl.pallas_call(
        functools.partial(_combine_kernel, has_norm=gain is not None),
        out_shape=jax.ShapeDtypeStruct((S, D), F32),
        grid=(S // tm,),
        in_specs=in_specs,
        out_specs=pl.BlockSpec((tm, D), lambda i: (i, 0)),
        compiler_params=_params("parallel"),
        name="combine",
    )(*args)


def moe(h, gain, wg, bg, we, be, w_gate, w_up, w_down, final_gain=None):
    S = h.shape[0]
    hn, eid, slot_w, counts = router(h, gain, wg, bg, we, be)
    blk_e, blk_n, blk_start = block_plan(counts, S * TOP_K)
    dest = assignment_dest(eid, blk_start)[:, :TOP_K].reshape(S * TOP_K)
    y_slots = experts(hn, blk_e, blk_n, dest, w_gate, w_up, w_down)
    return combine(h, y_slots, slot_w, final_gain)


def kernel(x, mem, positions, mix_norm, w_in, shift_mu, decay_w0, decay_w2, aaa_a0, aaa_a2, gate_g2, k_k, k_a,
           r_k, lnx_w, lnx_b, diff_lambda, subln_w, w_out, mem_q_norm, mem_kv_norm, wq_mem, wk_mem, wv_mem,
           wo_mem, moe_norm, router_group_w, router_group_b, router_expert_w, router_expert_b, expert_gate,
           expert_up, expert_down, final_norm):
    B, S, D = x.shape
    depth = w_in.shape[0]
    outs = []
    for b in range(B):
        h = x[b]
        memb = mem[b]
        for l in range(depth):
            lambda_init = 0.8 - 0.6 * math.exp(-0.3 * l)
            w_in_b = w_in[l].astype(BF16)
            proj_r = matmul(h, w_in_b[:, :RWKV_COLS], gain=mix_norm[l], tm=1024, tn=RWKV_COLS // 2)
            qkv = diff_proj(h, mix_norm[l], w_in_b[:, RWKV_COLS:], positions[b])
            pre = rwkv_prep(proj_r, shift_mu[l], decay_w0[l], decay_w2[l], aaa_a0[l], aaa_a2[l], gate_g2[l],
                            k_k[l], k_a[l], r_k[l])
            y_rwkv = rwkv_scan(*pre, lnx_w[l], lnx_b[l])
            y_diff = diff_flash(qkv, diff_lambda[l], subln_w[l], lambda_init)
            h = matmul([y_rwkv, y_diff], w_out[l].astype(BF16), residual=h, tm=1024, tn=1024)
            km = matmul(memb, wk_mem[l].astype(BF16), gain=mem_kv_norm[l], out_dtype=BF16)
            vm = matmul(memb, wv_mem[l].astype(BF16), gain=mem_kv_norm[l], out_dtype=BF16)
            o = mem_attn(h, mem_q_norm[l], wq_mem[l].astype(BF16), km, vm)
            h = matmul(o, wo_mem[l].astype(BF16), residual=h, tm=1024, tn=1024)
            h = moe(h, moe_norm[l], router_group_w[l], router_group_b[l], router_expert_w[l], router_expert_b[l],
                    expert_gate[l], expert_up[l], expert_down[l], final_norm if l == depth - 1 else None)
        outs.append(h)
    return jnp.stack(outs, axis=0)
```

```python
import functools
import math

import numpy as np
import jax
import jax.numpy as jnp
from jax import lax
from jax.experimental import pallas as pl
from jax.experimental.pallas import tpu as pltpu

F32 = jnp.float32
BF16 = jnp.bfloat16
HIGHEST = lax.Precision.HIGHEST

RWKV_HEADS = 16
HEAD_N = 64
RWKV_WIDTH = RWKV_HEADS * HEAD_N
DECAY_LORA = 64
AAA_LORA = 64
GATE_LORA = 128
RWKV_COLS = 3 * RWKV_WIDTH + DECAY_LORA + AAA_LORA + GATE_LORA
DIFF_HEADS = 8
DIFF_QK = 64
DIFF_V = 128
DIFF_WIDTH = DIFF_HEADS * DIFF_V
ROPE_THETA = 10000.0
MEM_HEADS = 4
N_GROUPS = 8
EXPERTS_PER_GROUP = 8
N_EXPERTS = N_GROUPS * EXPERTS_PER_GROUP
TOP_K = 2
NORM_EPS = 1e-6
LNX_EPS = 64e-5
SUBLN_EPS = 1e-5

LANES = 128
MXU_WIDTH = 256
VMEM_LIMIT = 56 * 1024 * 1024

CHUNK = 64
SCAN_ROWS = 256
SCAN_WIDTH = 512
MOE_ROWS = 512
MOE_UNIT = 128
MOE_SPLIT = 2
MOE_DMA_UNROLL = 8
MOE_WEIGHT_SLOTS = 3


def _params(*sem):
    return pltpu.CompilerParams(dimension_semantics=sem, vmem_limit_bytes=VMEM_LIMIT)


def _mm_kernel(*refs, n_x, has_norm, has_res):
    it = iter(refs)
    x_refs = [next(it) for _ in range(n_x)]
    g_ref = next(it) if has_norm else None
    w_ref = next(it)
    r_ref = next(it) if has_res else None
    o_ref = next(it)
    xs_ref = next(it)

    @pl.when(pl.program_id(1) == 0)
    def _():
        if has_norm:
            x = x_refs[0][...].astype(F32)
            x = x * lax.rsqrt(jnp.mean(x * x, axis=-1, keepdims=True) + NORM_EPS) * g_ref[...]
            xs_ref[...] = x.astype(BF16)
        else:
            k0 = 0
            for x_ref in x_refs:
                xs_ref[:, k0:k0 + x_ref.shape[1]] = x_ref[...].astype(BF16)
                k0 += x_ref.shape[1]

    acc = jnp.dot(xs_ref[...], w_ref[...], preferred_element_type=F32)
    if has_res:
        acc = acc + r_ref[...]
    o_ref[...] = acc.astype(o_ref.dtype)


def matmul(x, w, *, gain=None, residual=None, out_dtype=F32, tm=512, tn=512):
    xs = list(x) if isinstance(x, (list, tuple)) else [x]
    assert gain is None or len(xs) == 1
    M = xs[0].shape[0]
    K, N = w.shape
    assert sum(p.shape[1] for p in xs) == K
    tm = min(tm, M)
    assert M % tm == 0 and N % tn == 0, (M, N, tm, tn)
    in_specs = [pl.BlockSpec((tm, p.shape[1]), lambda i, j: (i, 0)) for p in xs]
    args = list(xs)
    if gain is not None:
        in_specs.append(pl.BlockSpec((1, K), lambda i, j: (0, 0)))
        args.append(gain.reshape(1, K).astype(F32))
    in_specs.append(pl.BlockSpec((K, tn), lambda i, j: (0, j)))
    args.append(w)
    if residual is not None:
        in_specs.append(pl.BlockSpec((tm, tn), lambda i, j: (i, j)))
        args.append(residual)
    return pl.pallas_call(
        functools.partial(_mm_kernel, n_x=len(xs), has_norm=gain is not None, has_res=residual is not None),
        out_shape=jax.ShapeDtypeStruct((M, N), out_dtype),
        grid=(M // tm, N // tn),
        in_specs=in_specs,
        out_specs=pl.BlockSpec((tm, tn), lambda i, j: (i, j)),
        scratch_shapes=[pltpu.VMEM((tm, K), BF16)],
        compiler_params=_params("parallel", "arbitrary"),
        name="matmul",
    )(*args)


def _sigmoid(x):
    return 1.0 / (1.0 + jnp.exp(-x))


def _rwkv_prep_kernel(p_ref, pp_ref, mu_ref, w0_ref, w2_ref, a0_ref, a2_ref, g2_ref, kk_ref, ka_ref, rk_ref,
                      hsum_ref, at_o, bt_o, kt_o, rt_o, v_o, g_o, bonus_o, pc_o):
    W = RWKV_WIDTH
    C = CHUNK
    p = p_ref[...]
    last = jnp.where(pl.program_id(0) == 0, 0.0, pp_ref[7:8, :])
    prev = pltpu.roll(p, 1, axis=0)
    row = lax.broadcasted_iota(jnp.int32, p.shape, 0)
    prev = jnp.where(row == 0, last, prev)
    ps = p + (prev - p) * mu_ref[...]
    r = ps[:, 0:W]
    k = ps[:, W:2 * W]
    v = ps[:, 2 * W:3 * W]
    o = 3 * W
    wd = ps[:, o:o + DECAY_LORA]
    ad = ps[:, o + DECAY_LORA:o + DECAY_LORA + AAA_LORA]
    gd = ps[:, o + DECAY_LORA + AAA_LORA:o + DECAY_LORA + AAA_LORA + GATE_LORA]
    z = w0_ref[...] + jnp.dot(jnp.tanh(wd).astype(BF16), w2_ref[...], preferred_element_type=F32)
    nz = -z
    softplus = jnp.maximum(nz, 0.0) + jnp.log(1.0 + jnp.exp(-jnp.abs(nz)))
    w_raw = -softplus - 0.5
    lw = -jnp.exp(w_raw)
    tm = p.shape[0]
    rr = lax.broadcasted_iota(jnp.int32, (tm, tm), 0)
    cc = lax.broadcasted_iota(jnp.int32, (tm, tm), 1)
    sh = C.bit_length() - 1
    tri = (((rr >> sh) == (cc >> sh)) & (rr >= cc)).astype(F32)
    cs = _dot(tri, lw, HIGHEST)
    a = _sigmoid(a0_ref[...] + jnp.dot(ad.astype(BF16), a2_ref[...], preferred_element_type=F32))
    g_o[...] = jnp.dot(_sigmoid(gd).astype(BF16), g2_ref[...], preferred_element_type=F32)

    def head_sum(x):
        hi = x.astype(BF16)
        lo = (x - hi.astype(F32)).astype(BF16)
        ones = hsum_ref[...]
        wb = ones.shape[0]
        cols = [slice(c, c + wb) for c in range(0, x.shape[1], wb)]
        return jnp.concatenate([_dot(hi[:, cs], ones) + _dot(lo[:, cs], ones) for cs in cols], axis=1)

    kk = k * kk_ref[...]
    kk = kk / jnp.maximum(jnp.sqrt(head_sum(kk * kk)), 1e-12)
    k2 = k * (1.0 + (a - 1.0) * ka_ref[...])
    e_neg = jnp.exp(-cs)
    at_o[...] = (-kk * jnp.exp(cs - lw)).astype(at_o.dtype)
    bt_o[...] = (kk * a) * e_neg
    kt_o[...] = k2 * e_neg
    rt_o[...] = r * jnp.exp(cs)
    v_o[...] = v
    bonus_o[...] = head_sum(r * k2 * rk_ref[...]) * v
    for q in range(tm // C):
        pc_o[q] = jnp.exp(cs[q * C + C - 1:q * C + C, :])


def rwkv_prep(p, mu, w0, w2, a0, a2, g2, k_k, k_a, r_k, *, tm=256):
    S = p.shape[0]
    tm = min(tm, S)
    assert tm % CHUNK == 0
    W = RWKV_WIDTH
    row = lambda x: x.reshape(1, -1).astype(F32)
    full = lambda a: pl.BlockSpec(a.shape, lambda i: (0, 0))
    head = jnp.arange(2 * LANES, dtype=jnp.int32) // HEAD_N
    hsum = (head[:, None] == head[None, :]).astype(BF16)
    args = [p, p, row(mu), row(w0), w2.astype(BF16), row(a0), a2.astype(BF16), g2.astype(BF16),
            row(k_k), row(k_a), row(r_k), hsum]
    in_specs = [pl.BlockSpec((tm, RWKV_COLS), lambda i: (i, 0)),
                pl.BlockSpec((8, RWKV_COLS), lambda i: (jnp.maximum(i * (tm // 8) - 1, 0), 0))]
    in_specs += [full(a) for a in args[2:]]
    f32 = jax.ShapeDtypeStruct((S, W), F32)
    blk = pl.BlockSpec((tm, W), lambda i: (i, 0))
    return pl.pallas_call(
        _rwkv_prep_kernel,
        out_shape=[jax.ShapeDtypeStruct((S, W), BF16)] + [f32] * 6 + [jax.ShapeDtypeStruct((S // CHUNK, 1, W), F32)],
        grid=(S // tm,),
        in_specs=in_specs,
        out_specs=[blk] * 7 + [pl.BlockSpec((tm // CHUNK, 1, W), lambda i: (i, 0, 0))],
        compiler_params=_params("parallel"),
        name="rwkv_prep",
    )(*args)


def _dot_nt(a, b, precision=None):
    return lax.dot_general(a, b, (((1,), (1,)), ((), ())), preferred_element_type=F32, precision=precision)


def _dot_tn(a, b, precision=None):
    return lax.dot_general(a, b, (((0,), (0,)), ((), ())), preferred_element_type=F32, precision=precision)


def _dot(a, b, precision=None):
    return jnp.dot(a, b, preferred_element_type=F32, precision=precision)


def _rwkv_scan_kernel(at_ref, bt_ref, kt_ref, rt_ref, v_ref, g_ref, bonus_ref, pc_ref, lnw_ref, lnb_ref,
                      y_ref, state_ref, yf_ref):
    C = CHUNK
    N = HEAD_N

    @pl.when(pl.program_id(1) == 0)
    def _():
        state_ref[...] = jnp.zeros_like(state_ref)

    row = lax.broadcasted_iota(jnp.int32, (C, C), 0)
    col = lax.broadcasted_iota(jnp.int32, (C, C), 1)
    eye = (row == col).astype(F32)
    strict = row > col
    incl = row >= col

    n_chunks = rt_ref.shape[0] // C
    heads = rt_ref.shape[1] // N
    items = [(j, q) for j in range(heads) for q in range(n_chunks)]
    G = range(len(items))

    def tile(ref, j, q):
        return ref[q * C:(q + 1) * C, j * N:(j + 1) * N]

    def par(ref, j):
        return ref[:, j * N:(j + 1) * N]

    at = [tile(at_ref, j, q) for j, q in items]
    bt = [tile(bt_ref, j, q) for j, q in items]
    kt = [tile(kt_ref, j, q) for j, q in items]
    rt = [tile(rt_ref, j, q) for j, q in items]
    v = [tile(v_ref, j, q) for j, q in items]
    pc = [pc_ref[q, :, j * N:(j + 1) * N] for j, q in items]
    vb = [x.astype(BF16) for x in v]
    btb = [x.astype(BF16) for x in bt]
    ktb = [x.astype(BF16) for x in kt]
    rtb = [x.astype(BF16) for x in rt]
    n_ab = [jnp.where(strict, _dot_nt(at[g], btb[g]), 0.0) for g in G]
    a_ak = [jnp.where(strict, _dot_nt(at[g], ktb[g]), 0.0).astype(BF16) for g in G]
    a_rb = [jnp.where(incl, _dot_nt(rtb[g], btb[g]), 0.0).astype(BF16) for g in G]
    a_rk = [jnp.where(incl, _dot_nt(rtb[g], ktb[g]), 0.0).astype(BF16) for g in G]
    akv = [_dot(a_ak[g], vb[g]) for g in G]
    t = None
    b = 1
    while b < C:
        sh = (2 * b).bit_length() - 1
        low_left = ((row >> sh) == (col >> sh)) & ((row & b) != 0) & ((col & b) == 0)
        nb = [jnp.where(low_left, n_ab[g], 0.0) for g in G]
        if b == 1:
            t = [eye + nb[g] for g in G]
        else:
            tb = [t[g].astype(BF16) for g in G]
            z = [_dot(nb[g].astype(BF16), tb[g]).astype(BF16) for g in G]
            t = [t[g] + _dot(tb[g], z[g]) for g in G]
        b *= 2
    tb = [t[g].astype(BF16) for g in G]
    wm = [_dot(tb[g], at[g]).astype(BF16) for g in G]
    u0 = [_dot(tb[g], akv[g].astype(BF16)).astype(BF16) for g in G]
    rm = [(rt[g] + _dot(a_rb[g], wm[g])).astype(BF16) for g in G]
    y0 = [_dot(a_rb[g], u0[g]) + _dot(a_rk[g], vb[g]) for g in G]
    bp = [(bt[g] * pc[g]).astype(BF16) for g in G]
    kp = [(kt[g] * pc[g]).astype(BF16) for g in G]
    mp = [_dot_tn(wm[g], bp[g]).astype(BF16) for g in G]
    s_add = [_dot_tn(u0[g], bp[g]) + _dot_tn(vb[g], kp[g]) for g in G]

    for j in range(heads):
        s = state_ref[j]
        for q in range(n_chunks):
            g = j * n_chunks + q
            sb = s.astype(BF16)
            y = _dot_nt(rm[g], sb) + y0[g]
            s = s * pc[g] + _dot(sb, mp[g]) + s_add[g]
            mean = jnp.mean(y, axis=-1, keepdims=True)
            yc = y - mean
            var = jnp.mean(yc * yc, axis=-1, keepdims=True)
            yn = yc * lax.rsqrt(var + LNX_EPS) * par(lnw_ref, j) + par(lnb_ref, j)
            yf_ref[q * C:(q + 1) * C, j * N:(j + 1) * N] = (yn + tile(bonus_ref, j, q)) * tile(g_ref, j, q)
        state_ref[j] = s
    y_ref[...] = yf_ref[...].astype(y_ref.dtype)


def rwkv_scan(at, bt, kt, rt, v, g, bonus, pc, lnx_w, lnx_b, *, rows=SCAN_ROWS, width=SCAN_WIDTH):
    S, W = rt.shape
    rows = min(rows, S)
    hp = width // HEAD_N
    row = lambda x: x.reshape(1, W).astype(F32)
    seq = pl.BlockSpec((rows, width), lambda h, c: (c, h))
    par = pl.BlockSpec((1, width), lambda h, c: (0, h))
    return pl.pallas_call(
        _rwkv_scan_kernel,
        out_shape=jax.ShapeDtypeStruct((S, W), BF16),
        grid=(W // width, S // rows),
        in_specs=[seq] * 7 + [pl.BlockSpec((rows // CHUNK, 1, width), lambda h, c: (c, 0, h))] + [par] * 2,
        out_specs=seq,
        scratch_shapes=[pltpu.VMEM((hp, HEAD_N, HEAD_N), F32), pltpu.VMEM((rows, width), F32)],
        compiler_params=_params("parallel", "arbitrary"),
        name="rwkv_scan",
    )(at, bt, kt, rt, v, g, bonus, pc, row(lnx_w), row(lnx_b))


def _diff_proj_kernel(x_ref, g_ref, w_ref, pos_ref, freq_ref, o_ref, xs_ref, cos_ref, sin_ref):
    j = pl.program_id(1)
    half = DIFF_QK // 2

    def first_half(shape):
        return (lax.broadcasted_iota(jnp.int32, shape, 1) & (DIFF_QK - 1)) < half

    @pl.when(j == 0)
    def _():
        x = x_ref[...]
        x = x * lax.rsqrt(jnp.mean(x * x, axis=-1, keepdims=True) + NORM_EPS) * g_ref[...]
        xs_ref[...] = x.astype(BF16)
        ang = pos_ref[...] * freq_ref[...]
        cos_ref[...] = jnp.cos(ang)
        sin_ref[...] = jnp.where(first_half(ang.shape), -1.0, 1.0) * jnp.sin(ang)

    @pl.when(j < 2)
    def _():
        sc = jnp.where(j == 0, DIFF_QK ** -0.5, 1.0)
        cos = cos_ref[...]
        sin = sin_ref[...]
        fh = first_half(cos.shape)
        for c0 in range(0, o_ref.shape[1], MXU_WIDTH):
            acc = jnp.dot(xs_ref[...], w_ref[:, c0:c0 + MXU_WIDTH], preferred_element_type=F32)
            for b0 in range(0, MXU_WIDTH, LANES):
                x = acc[:, b0:b0 + LANES]
                partner = jnp.where(fh, pltpu.roll(x, LANES - half, axis=1), pltpu.roll(x, half, axis=1))
                o_ref[:, c0 + b0:c0 + b0 + LANES] = ((x * cos + partner * sin) * sc).astype(o_ref.dtype)

    @pl.when(j == 2)
    def _():
        o_ref[...] = jnp.dot(xs_ref[...], w_ref[...], preferred_element_type=F32).astype(o_ref.dtype)


def diff_proj(x, gain, w, positions, *, tm=1024):
    S, K = x.shape
    W = DIFF_WIDTH
    tm = min(tm, S)
    assert w.shape == (K, 3 * W) and S % tm == 0
    inv_freq = ROPE_THETA ** (-(jnp.arange(0, DIFF_QK, 2, dtype=F32) / DIFF_QK))
    freq = jnp.tile(inv_freq, LANES // (DIFF_QK // 2)).reshape(1, LANES)
    pos = positions.reshape(S, 1).astype(F32)
    return pl.pallas_call(
        _diff_proj_kernel,
        out_shape=jax.ShapeDtypeStruct((S, 3 * W), BF16),
        grid=(S // tm, 3),
        in_specs=[pl.BlockSpec((tm, K), lambda i, j: (i, 0)),
                  pl.BlockSpec((1, K), lambda i, j: (0, 0)),
                  pl.BlockSpec((K, W), lambda i, j: (0, j)),
                  pl.BlockSpec((tm, 1), lambda i, j: (i, 0)),
                  pl.BlockSpec((1, LANES), lambda i, j: (0, 0))],
        out_specs=pl.BlockSpec((tm, W), lambda i, j: (i, j)),
        scratch_shapes=[pltpu.VMEM((tm, K), BF16), pltpu.VMEM((tm, LANES), F32), pltpu.VMEM((tm, LANES), F32)],
        compiler_params=_params("parallel", "arbitrary"),
        name="diff_proj",
    )(x, gain.reshape(1, K).astype(F32), w, pos, freq)


def _diff_flash_kernel(qi_tab, ki_tab, q_ref, k_ref, v_ref, lam_ref, sw_ref, o_ref,
                       m_ref, l_ref, acc_ref, s_ref, p_ref, alpha_ref, *, tile, row_block, diag_parts, lambda_init):
    p = pl.program_id(1)
    qi = qi_tab[p]
    ki = ki_tab[p]

    @pl.when(ki == 0)
    def _():
        m_ref[...] = jnp.full_like(m_ref, -jnp.inf)
        l_ref[...] = jnp.zeros_like(l_ref)
        acc_ref[...] = jnp.zeros_like(acc_ref)

    def process(row0, nrows, kmax, masked):
        rows = slice(row0, row0 + nrows)
        q = q_ref[rows, :]
        k = k_ref[0:kmax, :]
        v = v_ref[0:kmax, :]
        lane = lax.broadcasted_iota(jnp.int32, q.shape, 1)
        reps = kmax // LANES
        for c in range(2):
            qc = jnp.where((lane >> (DIFF_QK.bit_length() - 1)) == c, q, jnp.zeros_like(q))
            s_ref[0:nrows, 0:kmax] = _dot_nt(qc, k)
            for r0 in range(0, nrows, row_block):
                ar = slice(row0 + r0, row0 + r0 + row_block)
                lr = slice(r0, r0 + row_block)
                s = s_ref[lr, 0:kmax]
                if masked:
                    qpos = lax.broadcasted_iota(jnp.int32, s.shape, 0) + (row0 + r0)
                    kpos = lax.broadcasted_iota(jnp.int32, s.shape, 1)
                    s = jnp.where(kpos <= qpos, s, -jnp.inf)
                m_old = m_ref[c, ar, :]
                m_new = jnp.maximum(m_old, jnp.max(s, axis=-1, keepdims=True))
                alpha = jnp.exp(m_old - m_new)
                pr = jnp.exp(s - jnp.concatenate([m_new] * reps, axis=1))
                l_ref[c, ar, :] = alpha * l_ref[c, ar, :] + jnp.sum(pr, axis=-1, keepdims=True)
                m_ref[c, ar, :] = m_new
                alpha_ref[lr, :] = alpha
                p_ref[lr, 0:kmax] = pr.astype(BF16)
            acc_ref[c, rows, :] = (alpha_ref[0:nrows, :] * acc_ref[c, rows, :]
                                   + _dot(p_ref[0:nrows, 0:kmax], v))

    @pl.when(ki < qi)
    def _():
        process(0, tile, tile, False)

    @pl.when(ki == qi)
    def _():
        part = tile // diag_parts
        for a in range(diag_parts):
            process(a * part, part, (a + 1) * part, True)
        lp = lam_ref[...]
        lam = (jnp.exp(jnp.sum(lp[0:1] * lp[1:2], axis=-1, keepdims=True))
               - jnp.exp(jnp.sum(lp[2:3] * lp[3:4], axis=-1, keepdims=True)) + lambda_init)
        o = acc_ref[0] / l_ref[0] - lam * (acc_ref[1] / l_ref[1])
        o = o * lax.rsqrt(jnp.mean(o * o, axis=-1, keepdims=True) + SUBLN_EPS)
        o_ref[...] = (o * sw_ref[...] * (1.0 - lambda_init)).astype(o_ref.dtype)


def diff_flash(qkv, lambdas, subln_w, lambda_init, *, tile=1024, row_block=32, diag_parts=2):
    S = qkv.shape[0]
    tile = min(tile, S)
    assert S % tile == 0 and (tile // diag_parts) % LANES == 0 and (tile // diag_parts) % row_block == 0
    pairs = [(qi, ki) for qi in range(S // tile) for ki in range(qi + 1)]
    qi_tab = jnp.asarray([p[0] for p in pairs], jnp.int32)
    ki_tab = jnp.asarray([p[1] for p in pairs], jnp.int32)
    grid_spec = pltpu.PrefetchScalarGridSpec(
        num_scalar_prefetch=2,
        grid=(DIFF_HEADS, len(pairs)),
        in_specs=[pl.BlockSpec((tile, LANES), lambda h, p, qt, kt: (qt[p], h)),
                  pl.BlockSpec((tile, LANES), lambda h, p, qt, kt: (kt[p], DIFF_HEADS + h)),
                  pl.BlockSpec((tile, LANES), lambda h, p, qt, kt: (kt[p], 2 * DIFF_HEADS + h)),
                  pl.BlockSpec((4, DIFF_QK), lambda h, p, qt, kt: (0, 0)),
                  pl.BlockSpec((1, DIFF_V), lambda h, p, qt, kt: (0, 0))],
        out_specs=pl.BlockSpec((tile, LANES), lambda h, p, qt, kt: (qt[p], h)),
        scratch_shapes=[pltpu.VMEM((2, tile, LANES), F32),
                        pltpu.VMEM((2, tile, LANES), F32),
                        pltpu.VMEM((2, tile, DIFF_V), F32),
                        pltpu.VMEM((tile, tile), F32),
                        pltpu.VMEM((tile, tile), BF16),
                        pltpu.VMEM((tile, LANES), F32)],
    )
    return pl.pallas_call(
        functools.partial(_diff_flash_kernel, tile=tile, row_block=row_block, diag_parts=diag_parts,
                          lambda_init=lambda_init),
        out_shape=jax.ShapeDtypeStruct((S, DIFF_WIDTH), BF16),
        grid_spec=grid_spec,
        compiler_params=_params("parallel", "arbitrary"),
        name="diff_flash",
    )(qi_tab, ki_tab, qkv, qkv, qkv, lambdas.astype(F32), subln_w.reshape(1, DIFF_V).astype(F32))


def _mem_attn_kernel(x_ref, g_ref, w_ref, k_ref, v_ref, o_ref, xs_ref):
    @pl.when(pl.program_id(1) == 0)
    def _():
        x = x_ref[...]
        x = x * lax.rsqrt(jnp.mean(x * x, axis=-1, keepdims=True) + NORM_EPS) * g_ref[...]
        xs_ref[...] = x.astype(BF16)

    q = jnp.dot(xs_ref[...], w_ref[...], preferred_element_type=F32).astype(BF16)
    s = _dot_nt(q, k_ref[...]) * (q.shape[1] ** -0.5)
    s = s - jnp.max(s, axis=-1, keepdims=True)
    e = jnp.exp(s)
    pr = e / jnp.sum(e, axis=-1, keepdims=True)
    o_ref[...] = _dot(pr.astype(BF16), v_ref[...]).astype(o_ref.dtype)


def mem_attn(x, gain, wq, k, v, *, tm=1024):
    S, D = x.shape
    M = k.shape[0]
    hd = D // MEM_HEADS
    tm = min(tm, S)
    return pl.pallas_call(
        _mem_attn_kernel,
        out_shape=jax.ShapeDtypeStruct((S, D), BF16),
        grid=(S // tm, MEM_HEADS),
        in_specs=[pl.BlockSpec((tm, D), lambda i, j: (i, 0)),
                  pl.BlockSpec((1, D), lambda i, j: (0, 0)),
                  pl.BlockSpec((D, hd), lambda i, j: (0, j)),
                  pl.BlockSpec((M, hd), lambda i, j: (0, j)),
                  pl.BlockSpec((M, hd), lambda i, j: (0, j))],
        out_specs=pl.BlockSpec((tm, hd), lambda i, j: (i, j)),
        scratch_shapes=[pltpu.VMEM((tm, D), BF16)],
        compiler_params=_params("parallel", "arbitrary"),
        name="mem_attn",
    )(x, gain.reshape(1, D).astype(F32), wq, k, v)


def _first_argmax(x, lane, big):
    m = jnp.max(x, axis=-1, keepdims=True)
    idx = jnp.min(jnp.where(x == m, lane, big), axis=-1, keepdims=True)
    return m, idx


def _router_kernel(h_ref, gain_ref, wg_ref, bg_ref, we_ref, be_ref, hn_ref, eid_ref, ew_ref, cnt_ref):
    @pl.when(pl.program_id(0) == 0)
    def _():
        cnt_ref[...] = jnp.zeros_like(cnt_ref)

    x = h_ref[...]
    hn = x * lax.rsqrt(jnp.mean(x * x, axis=-1, keepdims=True) + NORM_EPS) * gain_ref[...]
    hn_ref[...] = hn
    hn_hi = hn.astype(BF16)
    hn_lo = (hn - hn_hi.astype(F32)).astype(BF16)

    def logits(w_ref, b_ref):
        w = w_ref[...]
        w_hi = w.astype(BF16)
        w_lo = (w - w_hi.astype(F32)).astype(BF16)
        return _dot(hn_hi, w_hi) + (_dot(hn_hi, w_lo) + _dot(hn_lo, w_hi)) + b_ref[...]
    g_logits = logits(wg_ref, bg_ref)
    e_logits = logits(we_ref, be_ref)
    tm = x.shape[0]
    lane_g = lax.broadcasted_iota(jnp.int32, (tm, N_GROUPS), 1)
    g_max, g_idx = _first_argmax(g_logits, lane_g, N_GROUPS)
    g_w = 1.0 / jnp.sum(jnp.exp(g_logits - g_max), axis=-1, keepdims=True)
    lane_e = lax.broadcasted_iota(jnp.int32, (tm, N_EXPERTS), 1)
    in_group = (lane_e // EXPERTS_PER_GROUP) == g_idx
    el = jnp.where(in_group, e_logits, -jnp.inf)
    e_max = jnp.max(el, axis=-1, keepdims=True)
    ex = jnp.exp(el - e_max)
    prob = ex / jnp.sum(ex, axis=-1, keepdims=True)
    prob = jnp.where(in_group, prob, -1.0)
    p1, i1 = _first_argmax(prob, lane_e, N_EXPERTS)
    p2, i2 = _first_argmax(jnp.where(lane_e == i1, -1.0, prob), lane_e, N_EXPERTS)
    tot = p1 + p2
    lane_o = lax.broadcasted_iota(jnp.int32, (tm, LANES), 1)
    eid_ref[...] = jnp.where(lane_o == 0, i1, jnp.where(lane_o == 1, i2, 0))
    ew_ref[...] = jnp.where(lane_o == 0, g_w * (p1 / tot), jnp.where(lane_o == 1, g_w * (p2 / tot), 0.0))
    chosen = ((lane_o == i1) | (lane_o == i2)).astype(F32)
    cnt_ref[...] = cnt_ref[...] + jnp.sum(chosen, axis=0, keepdims=True).astype(jnp.int32)


def router(h, gain, wg, bg, we, be, *, tm=512):
    S, D = h.shape
    tm = min(tm, S)
    full = lambda a: pl.BlockSpec(a.shape, lambda i: (0, 0))
    args = [h, gain.reshape(1, D).astype(F32), wg.astype(F32), bg.reshape(1, -1).astype(F32),
            we.astype(F32), be.reshape(1, -1).astype(F32)]
    return pl.pallas_call(
        _router_kernel,
        out_shape=[jax.ShapeDtypeStruct((S, D), F32),
                   jax.ShapeDtypeStruct((S, LANES), jnp.int32),
                   jax.ShapeDtypeStruct((S, LANES), F32),
                   jax.ShapeDtypeStruct((1, LANES), jnp.int32)],
        grid=(S // tm,),
        in_specs=[pl.BlockSpec((tm, D), lambda i: (i, 0))] + [full(a) for a in args[1:]],
        out_specs=[pl.BlockSpec((tm, D), lambda i: (i, 0)),
                   pl.BlockSpec((tm, LANES), lambda i: (i, 0)),
                   pl.BlockSpec((tm, LANES), lambda i: (i, 0)),
                   pl.BlockSpec((1, LANES), lambda i: (0, 0))],
        compiler_params=_params("arbitrary"),
        name="router",
    )(*args)


def _plan_kernel(cnt_ref, blk_e_ref, blk_n_ref, start_ref):
    nb = blk_e_ref.shape[0]
    sh = MOE_ROWS.bit_length() - 1

    def per_expert(e, b):
        n = cnt_ref[e]
        start_ref[e] = b

        def per_block(t, _):
            blk_e_ref[b + t] = e
            blk_n_ref[b + t] = jnp.minimum(n - t * MOE_ROWS, MOE_ROWS)
            return 0
        k = (n + (MOE_ROWS - 1)) >> sh
        lax.fori_loop(0, k, per_block, 0)
        return b + k
    total = lax.fori_loop(0, N_EXPERTS, per_expert, 0)

    def rest(e, _):
        start_ref[e] = total
        return 0
    lax.fori_loop(N_EXPERTS, start_ref.shape[0], rest, 0)
    last_e = blk_e_ref[jnp.maximum(total - 1, 0)]

    def tail(b, _):
        blk_e_ref[b] = last_e
        blk_n_ref[b] = 0
        return 0
    lax.fori_loop(total, nb, tail, 0)


def block_plan(counts, n_assign):
    nb = (n_assign + N_EXPERTS * (MOE_ROWS - 1)) // MOE_ROWS
    smem = pl.BlockSpec(memory_space=pltpu.SMEM)
    return pl.pallas_call(
        _plan_kernel,
        out_shape=[jax.ShapeDtypeStruct((nb,), jnp.int32), jax.ShapeDtypeStruct((nb,), jnp.int32),
                   jax.ShapeDtypeStruct((LANES,), jnp.int32)],
        in_specs=[smem],
        out_specs=[smem, smem, smem],
        name="block_plan",
    )(counts.reshape(LANES))


def _dest_kernel(eid_ref, start_ref, dest_ref, carry_ref):
    @pl.when(pl.program_id(0) == 0)
    def _():
        carry_ref[...] = jnp.zeros_like(carry_ref)

    eid = eid_ref[...]
    tm = eid.shape[0]
    lane = lax.broadcasted_iota(jnp.int32, (tm, LANES), 1)
    oh0 = lane == eid[:, 0:1]
    oh1 = lane == eid[:, 1:2]
    both = jnp.where(oh0 | oh1, 1.0, 0.0).astype(BF16)
    r = lax.broadcasted_iota(jnp.int32, (tm, tm), 0)
    c = lax.broadcasted_iota(jnp.int32, (tm, tm), 1)
    before = _dot(jnp.where(r > c, 1.0, 0.0).astype(BF16), both) + carry_ref[...]
    pos = before + (start_ref[...] * MOE_ROWS).astype(F32)
    d0 = jnp.sum(jnp.where(oh0, pos, 0.0), axis=-1, keepdims=True).astype(jnp.int32)
    d1 = jnp.sum(jnp.where(oh1, pos, 0.0), axis=-1, keepdims=True).astype(jnp.int32)
    dest_ref[...] = jnp.where(lane == 0, d0, jnp.where(lane == 1, d1, 0))
    carry_ref[...] = carry_ref[...] + jnp.sum(both.astype(F32), axis=0, keepdims=True)


def assignment_dest(eid, blk_start, *, tm=512):
    S = eid.shape[0]
    tm = min(tm, S)
    return pl.pallas_call(
        _dest_kernel,
        out_shape=jax.ShapeDtypeStruct((S, LANES), jnp.int32),
        grid=(S // tm,),
        in_specs=[pl.BlockSpec((tm, LANES), lambda i: (i, 0)), pl.BlockSpec((1, LANES), lambda i: (0, 0))],
        out_specs=pl.BlockSpec((tm, LANES), lambda i: (i, 0)),
        scratch_shapes=[pltpu.VMEM((1, LANES), F32)],
        compiler_params=_params("arbitrary"),
        name="assignment_dest",
    )(eid, blk_start.reshape(1, LANES))


def _for_rows(n, fn):
    sh = MOE_DMA_UNROLL.bit_length() - 1

    def group(t, _):
        for u in range(MOE_DMA_UNROLL):
            fn(t * MOE_DMA_UNROLL + u)
        return 0
    lax.fori_loop(0, n >> sh, group, 0)

    def one(r, _):
        fn(r)
        return 0
    lax.fori_loop((n >> sh) << sh, n, one, 0)


def _experts_kernel(blk_e, blk_n, dest, hn_hbm, wg_hbm, wu_hbm, wd_hbm, out_hbm,
                    xg_ref, xb_ref, acc_ref, wg_ring, wu_ring, wd_ring, wgb_ref, wub_ref, wdb_ref, inv_ref,
                    gsem, ssem, wsem):
    i = pl.program_id(0)
    j = pl.program_id(1)
    nb = pl.num_programs(0)
    nsplit = pl.num_programs(1)
    n_tok = hn_hbm.shape[0]
    k_shift = TOP_K.bit_length() - 1
    de = wgb_ref.shape[1]
    n_slots = wg_ring.shape[0]

    def weight_copies(b, jj):
        slot = lax.rem(b * MOE_SPLIT + jj, n_slots)
        e = blk_e[b]
        c0 = pl.multiple_of(jj * de, de)
        return (pltpu.make_async_copy(wg_hbm.at[e, :, pl.ds(c0, de)], wg_ring.at[slot], wsem.at[slot]),
                pltpu.make_async_copy(wu_hbm.at[e, :, pl.ds(c0, de)], wu_ring.at[slot], wsem.at[slot]),
                pltpu.make_async_copy(wd_hbm.at[e, pl.ds(c0, de), :], wd_ring.at[slot], wsem.at[slot]))

    def start_weights(t):
        b = lax.div(t, MOE_SPLIT)
        jj = lax.rem(t, MOE_SPLIT)
        bb = jnp.minimum(b, nb - 1)

        @pl.when((b < nb) & (blk_n[bb] > 0))
        def _():
            for cp in weight_copies(bb, jj):
                cp.start()

    def gather(b, r):
        tok = inv_ref[b * MOE_ROWS + r] >> k_shift
        return pltpu.make_async_copy(hn_hbm.at[pl.ds(tok, 1)], xg_ref.at[pl.ds(r, 1)], gsem)

    def scatter(b, r):
        a = inv_ref[b * MOE_ROWS + r]
        row = (a & (TOP_K - 1)) * n_tok + (a >> k_shift)
        return pltpu.make_async_copy(acc_ref.at[pl.ds(r, 1)], out_hbm.at[pl.ds(row, 1)], ssem)

    def n_gather(b):
        return pl.multiple_of(((blk_n[b] + 7) >> 3) << 3, 8)

    def start_gather(b):
        def pad(r, _):
            inv_ref[b * MOE_ROWS + r] = 0
            return 0
        lax.fori_loop(blk_n[b], n_gather(b), pad, 0)
        _for_rows(n_gather(b), lambda r: gather(b, r).start(priority=1))

    def wait_gather(b):
        n = n_gather(b)

        @pl.when(n > 0)
        def _():
            pltpu.make_async_copy(hn_hbm.at[pl.ds(0, n)], xg_ref.at[pl.ds(0, n)], gsem).wait()

    def start_scatter(b):
        _for_rows(blk_n[b], lambda r: scatter(b, r).start(priority=1))

    def wait_scatter(b):
        n8 = pl.multiple_of((blk_n[b] >> 3) << 3, 8)

        @pl.when(n8 > 0)
        def _():
            pltpu.make_async_copy(acc_ref.at[pl.ds(0, n8)], out_hbm.at[pl.ds(0, n8)], ssem).wait()

        def one(r, _):
            scatter(b, r).wait()
            return 0
        lax.fori_loop(n8, blk_n[b], one, 0)

    n_units = (blk_n[i] + (MOE_UNIT - 1)) >> (MOE_UNIT.bit_length() - 1)

    @pl.when(j == 0)
    def _():
        @pl.when(i == 0)
        def _():
            def invert(a):
                inv_ref[dest[a]] = a
            _for_rows(dest.shape[0], invert)
            xg_ref[...] = jnp.zeros_like(xg_ref)
            start_gather(0)
        wait_gather(i)

        def cast(sb, _):
            rs = pl.ds(pl.multiple_of(sb * MOE_UNIT, MOE_UNIT), MOE_UNIT)
            xb_ref[rs, :] = xg_ref[rs, :].astype(BF16)
            return 0
        lax.fori_loop(0, n_units, cast, 0)

        @pl.when(i + 1 < nb)
        def _():
            start_gather(i + 1)

        @pl.when(i > 0)
        def _():
            wait_scatter(i - 1)

    t = i * MOE_SPLIT + j

    @pl.when(t == 0)
    def _():
        for t0 in range(n_slots):
            start_weights(jnp.int32(t0))

    @pl.when(n_units > 0)
    def _():
        for cp in weight_copies(i, j):
            cp.wait()
        slot = lax.rem(t, n_slots)
        wgb_ref[...] = wg_ring[slot].astype(BF16)
        wub_ref[...] = wu_ring[slot].astype(BF16)
        wdb_ref[...] = wd_ring[slot].astype(BF16)

    start_weights(t + n_slots)

    def compute(row0, nrows):
        rs = pl.ds(row0, nrows)
        xb = xb_ref[rs, :]
        gate = _dot(xb, wgb_ref[...])
        hid = (gate * _sigmoid(gate)) * _dot(xb, wub_ref[...])
        part = _dot(hid.astype(BF16), wdb_ref[...])

        @pl.when(j == 0)
        def _():
            acc_ref[rs, :] = part

        @pl.when(j > 0)
        def _():
            acc_ref[rs, :] = acc_ref[rs, :] + part

    for units in range(1, MOE_ROWS // MOE_UNIT + 1):
        @pl.when(n_units == units)
        def _():
            compute(0, units * MOE_UNIT)

    @pl.when(j == nsplit - 1)
    def _():
        start_scatter(i)

        @pl.when(i == nb - 1)
        def _():
            wait_scatter(i)


def experts(hn, blk_e, blk_n, dest, w_gate, w_up, w_down):
    S, D = hn.shape
    E, _, DE = w_gate.shape
    nb = blk_e.shape[0]
    de = DE // MOE_SPLIT
    any_space = pl.BlockSpec(memory_space=pl.ANY)
    grid_spec = pltpu.PrefetchScalarGridSpec(
        num_scalar_prefetch=3,
        grid=(nb, MOE_SPLIT),
        in_specs=[any_space] * 4,
        out_specs=any_space,
        scratch_shapes=[pltpu.VMEM((MOE_ROWS, D), F32),
                        pltpu.VMEM((MOE_ROWS, D), BF16),
                        pltpu.VMEM((MOE_ROWS, D), F32),
                        pltpu.VMEM((MOE_WEIGHT_SLOTS, D, de), F32),
                        pltpu.VMEM((MOE_WEIGHT_SLOTS, D, de), F32),
                        pltpu.VMEM((MOE_WEIGHT_SLOTS, de, D), F32),
                        pltpu.VMEM((D, de), BF16),
                        pltpu.VMEM((D, de), BF16),
                        pltpu.VMEM((de, D), BF16),
                        pltpu.SMEM((nb * MOE_ROWS,), jnp.int32),
                        pltpu.SemaphoreType.DMA(()),
                        pltpu.SemaphoreType.DMA(()),
                        pltpu.SemaphoreType.DMA((MOE_WEIGHT_SLOTS,))],
    )
    return pl.pallas_call(
        _experts_kernel,
        out_shape=jax.ShapeDtypeStruct((TOP_K * S, D), F32),
        grid_spec=grid_spec,
        compiler_params=_params("arbitrary", "arbitrary"),
        name="experts",
    )(blk_e, blk_n, dest, hn, w_gate, w_up, w_down)


def _combine_kernel(*refs, has_norm):
    h_ref, y0_ref, y1_ref, w_ref = refs[:4]
    o_ref = refs[-1]
    w = w_ref[...]
    x = h_ref[...] + (y0_ref[0] * w[:, 0:1] + y1_ref[0] * w[:, 1:2])
    if has_norm:
        x = x * lax.rsqrt(jnp.mean(x * x, axis=-1, keepdims=True) + NORM_EPS) * refs[4][...]
    o_ref[...] = x


def combine(h, y_slots, slot_w, gain=None, *, tm=512):
    S, D = h.shape
    tm = min(tm, S)
    y3 = y_slots.reshape(TOP_K, S, D)
    in_specs = [pl.BlockSpec((tm, D), lambda i: (i, 0)),
                pl.BlockSpec((1, tm, D), lambda i: (0, i, 0)),
                pl.BlockSpec((1, tm, D), lambda i: (1, i, 0)),
                pl.BlockSpec((tm, LANES), lambda i: (i, 0))]
    args = [h, y3, y3, slot_w]
    if gain is not None:
        in_specs.append(pl.BlockSpec((1, D), lambda i: (0, 0)))
        args.append(gain.reshape(1, D).astype(F32))
    return pl.pallas_call(
        functools.partial(_combine_kernel, has_norm=gain is not None),
        out_shape=jax.ShapeDtypeStruct((S, D), F32),
        grid=(S // tm,),
        in_specs=in_specs,
        out_specs=pl.BlockSpec((tm, D), lambda i: (i, 0)),
        compiler_params=_params("parallel"),
        name="combine",
    )(*args)


def moe(h, gain, wg, bg, we, be, w_gate, w_up, w_down, final_gain=None):
    S = h.shape[0]
    hn, eid, slot_w, counts = router(h, gain, wg, bg, we, be)
    blk_e, blk_n, blk_start = block_plan(counts, S * TOP_K)
    dest = assignment_dest(eid, blk_start)[:, :TOP_K].reshape(S * TOP_K)
    y_slots = experts(hn, blk_e, blk_n, dest, w_gate, w_up, w_down)
    return combine(h, y_slots, slot_w, final_gain)


def kernel(x, mem, positions, mix_norm, w_in, shift_mu, decay_w0, decay_w2, aaa_a0, aaa_a2, gate_g2, k_k, k_a,
           r_k, lnx_w, lnx_b, diff_lambda, subln_w, w_out, mem_q_norm, mem_kv_norm, wq_mem, wk_mem, wv_mem,
           wo_mem, moe_norm, router_group_w, router_group_b, router_expert_w, router_expert_b, expert_gate,
           expert_up, expert_down, final_norm):
    B, S, D = x.shape
    depth = w_in.shape[0]
    outs = []
    for b in range(B):
        h = x[b]
        memb = mem[b]
        for l in range(depth):
            lambda_init = 0.8 - 0.6 * math.exp(-0.3 * l)
            w_in_b = w_in[l].astype(BF16)
            proj_r = matmul(h, w_in_b[:, :RWKV_COLS], gain=mix_norm[l], tm=1024, tn=RWKV_COLS // 2)
            qkv = diff_proj(h, mix_norm[l], w_in_b[:, RWKV_COLS:], positions[b])
            pre = rwkv_prep(proj_r, shift_mu[l], decay_w0[l], decay_w2[l], aaa_a0[l], aaa_a2[l], gate_g2[l],
                            k_k[l], k_a[l], r_k[l])
            y_rwkv = rwkv_scan(*pre, lnx_w[l], lnx_b[l])
            y_diff = diff_flash(qkv, diff_lambda[l], subln_w[l], lambda_init)
            h = matmul([y_rwkv, y_diff], w_out[l].astype(BF16), residual=h, tm=1024, tn=1024)
            km = matmul(memb, wk_mem[l].astype(BF16), gain=mem_kv_norm[l], out_dtype=BF16)
            vm = matmul(memb, wv_mem[l].astype(BF16), gain=mem_kv_norm[l], out_dtype=BF16)
            o = mem_attn(h, mem_q_norm[l], wq_mem[l].astype(BF16), km, vm)
            h = matmul(o, wo_mem[l].astype(BF16), residual=h, tm=1024, tn=1024)
            h = moe(h, moe_norm[l], router_group_w[l], router_group_b[l], router_expert_w[l], router_expert_b[l],
                    expert_gate[l], expert_up[l], expert_down[l], final_norm if l == depth - 1 else None)
        outs.append(h)
    return jnp.stack(outs, axis=0)
```

```python
import functools
import math

import numpy as np
import jax
import jax.numpy as jnp
from jax import lax
from jax.experimental import pallas as pl
from jax.experimental.pallas import tpu as pltpu

F32 = jnp.float32
BF16 = jnp.bfloat16
HIGHEST = lax.Precision.HIGHEST

RWKV_HEADS = 16
HEAD_N = 64
RWKV_WIDTH = RWKV_HEADS * HEAD_N
DECAY_LORA = 64
AAA_LORA = 64
GATE_LORA = 128
RWKV_COLS = 3 * RWKV_WIDTH + DECAY_LORA + AAA_LORA + GATE_LORA
DIFF_HEADS = 8
DIFF_QK = 64
DIFF_V = 128
DIFF_WIDTH = DIFF_HEADS * DIFF_V
ROPE_THETA = 10000.0
MEM_HEADS = 4
N_GROUPS = 8
EXPERTS_PER_GROUP = 8
N_EXPERTS = N_GROUPS * EXPERTS_PER_GROUP
TOP_K = 2
NORM_EPS = 1e-6
LNX_EPS = 64e-5
SUBLN_EPS = 1e-5

LANES = 128
MXU_WIDTH = 256
VMEM_LIMIT = 56 * 1024 * 1024

CHUNK = 64
SCAN_ROWS = 256
SCAN_WIDTH = 512
MOE_ROWS = 512
MOE_UNIT = 128
MOE_SPLIT = 2
MOE_DMA_UNROLL = 8
MOE_WEIGHT_SLOTS = 2


def _params(*sem):
    return pltpu.CompilerParams(dimension_semantics=sem, vmem_limit_bytes=VMEM_LIMIT)


def _mm_kernel(*refs, n_x, has_norm, has_res):
    it = iter(refs)
    x_refs = [next(it) for _ in range(n_x)]
    g_ref = next(it) if has_norm else None
    w_ref = next(it)
    r_ref = next(it) if has_res else None
    o_ref = next(it)
    xs_ref = next(it)

    @pl.when(pl.program_id(1) == 0)
    def _():
        if has_norm:
            x = x_refs[0][...].astype(F32)
            x = x * lax.rsqrt(jnp.mean(x * x, axis=-1, keepdims=True) + NORM_EPS) * g_ref[...]
            xs_ref[...] = x.astype(BF16)
        else:
            k0 = 0
            for x_ref in x_refs:
                xs_ref[:, k0:k0 + x_ref.shape[1]] = x_ref[...].astype(BF16)
                k0 += x_ref.shape[1]

    acc = jnp.dot(xs_ref[...], w_ref[...], preferred_element_type=F32)
    if has_res:
        acc = acc + r_ref[...]
    o_ref[...] = acc.astype(o_ref.dtype)


def matmul(x, w, *, gain=None, residual=None, out_dtype=F32, tm=512, tn=512):
    xs = list(x) if isinstance(x, (list, tuple)) else [x]
    assert gain is None or len(xs) == 1
    M = xs[0].shape[0]
    K, N = w.shape
    assert sum(p.shape[1] for p in xs) == K
    tm = min(tm, M)
    assert M % tm == 0 and N % tn == 0, (M, N, tm, tn)
    in_specs = [pl.BlockSpec((tm, p.shape[1]), lambda i, j: (i, 0)) for p in xs]
    args = list(xs)
    if gain is not None:
        in_specs.append(pl.BlockSpec((1, K), lambda i, j: (0, 0)))
        args.append(gain.reshape(1, K).astype(F32))
    in_specs.append(pl.BlockSpec((K, tn), lambda i, j: (0, j)))
    args.append(w)
    if residual is not None:
        in_specs.append(pl.BlockSpec((tm, tn), lambda i, j: (i, j)))
        args.append(residual)
    return pl.pallas_call(
        functools.partial(_mm_kernel, n_x=len(xs), has_norm=gain is not None, has_res=residual is not None),
        out_shape=jax.ShapeDtypeStruct((M, N), out_dtype),
        grid=(M // tm, N // tn),
        in_specs=in_specs,
        out_specs=pl.BlockSpec((tm, tn), lambda i, j: (i, j)),
        scratch_shapes=[pltpu.VMEM((tm, K), BF16)],
        compiler_params=_params("parallel", "arbitrary"),
        name="matmul",
    )(*args)


def _sigmoid(x):
    return 1.0 / (1.0 + jnp.exp(-x))


def _rwkv_prep_kernel(p_ref, pp_ref, mu_ref, w0_ref, w2_ref, a0_ref, a2_ref, g2_ref, kk_ref, ka_ref, rk_ref,
                      hsum_ref, at_o, bt_o, kt_o, rt_o, v_o, g_o, bonus_o, pc_o):
    W = RWKV_WIDTH
    C = CHUNK
    p = p_ref[...]
    last = jnp.where(pl.program_id(0) == 0, 0.0, pp_ref[7:8, :])
    prev = pltpu.roll(p, 1, axis=0)
    row = lax.broadcasted_iota(jnp.int32, p.shape, 0)
    prev = jnp.where(row == 0, last, prev)
    ps = p + (prev - p) * mu_ref[...]
    r = ps[:, 0:W]
    k = ps[:, W:2 * W]
    v = ps[:, 2 * W:3 * W]
    o = 3 * W
    wd = ps[:, o:o + DECAY_LORA]
    ad = ps[:, o + DECAY_LORA:o + DECAY_LORA + AAA_LORA]
    gd = ps[:, o + DECAY_LORA + AAA_LORA:o + DECAY_LORA + AAA_LORA + GATE_LORA]
    z = w0_ref[...] + jnp.dot(jnp.tanh(wd).astype(BF16), w2_ref[...], preferred_element_type=F32)
    nz = -z
    softplus = jnp.maximum(nz, 0.0) + jnp.log(1.0 + jnp.exp(-jnp.abs(nz)))
    w_raw = -softplus - 0.5
    lw = -jnp.exp(w_raw)
    tm = p.shape[0]
    rr = lax.broadcasted_iota(jnp.int32, (tm, tm), 0)
    cc = lax.broadcasted_iota(jnp.int32, (tm, tm), 1)
    sh = C.bit_length() - 1
    tri = (((rr >> sh) == (cc >> sh)) & (rr >= cc)).astype(F32)
    cs = _dot(tri, lw, HIGHEST)
    a = _sigmoid(a0_ref[...] + jnp.dot(ad.astype(BF16), a2_ref[...], preferred_element_type=F32))
    g_o[...] = jnp.dot(_sigmoid(gd).astype(BF16), g2_ref[...], preferred_element_type=F32)

    def head_sum(x):
        hi = x.astype(BF16)
        lo = (x - hi.astype(F32)).astype(BF16)
        ones = hsum_ref[...]
        wb = ones.shape[0]
        cols = [slice(c, c + wb) for c in range(0, x.shape[1], wb)]
        return jnp.concatenate([_dot(hi[:, cs], ones) + _dot(lo[:, cs], ones) for cs in cols], axis=1)

    kk = k * kk_ref[...]
    kk = kk / jnp.maximum(jnp.sqrt(head_sum(kk * kk)), 1e-12)
    k2 = k * (1.0 + (a - 1.0) * ka_ref[...])
    e_neg = jnp.exp(-cs)
    at_o[...] = (-kk * jnp.exp(cs - lw)).astype(at_o.dtype)
    bt_o[...] = (kk * a) * e_neg
    kt_o[...] = k2 * e_neg
    rt_o[...] = r * jnp.exp(cs)
    v_o[...] = v
    bonus_o[...] = head_sum(r * k2 * rk_ref[...]) * v
    for q in range(tm // C):
        pc_o[q] = jnp.exp(cs[q * C + C - 1:q * C + C, :])


def rwkv_prep(p, mu, w0, w2, a0, a2, g2, k_k, k_a, r_k, *, tm=256):
    S = p.shape[0]
    tm = min(tm, S)
    assert tm % CHUNK == 0
    W = RWKV_WIDTH
    row = lambda x: x.reshape(1, -1).astype(F32)
    full = lambda a: pl.BlockSpec(a.shape, lambda i: (0, 0))
    head = jnp.arange(2 * LANES, dtype=jnp.int32) // HEAD_N
    hsum = (head[:, None] == head[None, :]).astype(BF16)
    args = [p, p, row(mu), row(w0), w2.astype(BF16), row(a0), a2.astype(BF16), g2.astype(BF16),
            row(k_k), row(k_a), row(r_k), hsum]
    in_specs = [pl.BlockSpec((tm, RWKV_COLS), lambda i: (i, 0)),
                pl.BlockSpec((8, RWKV_COLS), lambda i: (jnp.maximum(i * (tm // 8) - 1, 0), 0))]
    in_specs += [full(a) for a in args[2:]]
    f32 = jax.ShapeDtypeStruct((S, W), F32)
    blk = pl.BlockSpec((tm, W), lambda i: (i, 0))
    return pl.pallas_call(
        _rwkv_prep_kernel,
        out_shape=[jax.ShapeDtypeStruct((S, W), BF16)] + [f32] * 6 + [jax.ShapeDtypeStruct((S // CHUNK, 1, W), F32)],
        grid=(S // tm,),
        in_specs=in_specs,
        out_specs=[blk] * 7 + [pl.BlockSpec((tm // CHUNK, 1, W), lambda i: (i, 0, 0))],
        compiler_params=_params("parallel"),
        name="rwkv_prep",
    )(*args)


def _dot_nt(a, b, precision=None):
    return lax.dot_general(a, b, (((1,), (1,)), ((), ())), preferred_element_type=F32, precision=precision)


def _dot_tn(a, b, precision=None):
    return lax.dot_general(a, b, (((0,), (0,)), ((), ())), preferred_element_type=F32, precision=precision)


def _dot(a, b, precision=None):
    return jnp.dot(a, b, preferred_element_type=F32, precision=precision)


def _rwkv_scan_kernel(at_ref, bt_ref, kt_ref, rt_ref, v_ref, g_ref, bonus_ref, pc_ref, lnw_ref, lnb_ref,
                      y_ref, state_ref, yf_ref):
    C = CHUNK
    N = HEAD_N

    @pl.when(pl.program_id(1) == 0)
    def _():
        state_ref[...] = jnp.zeros_like(state_ref)

    row = lax.broadcasted_iota(jnp.int32, (C, C), 0)
    col = lax.broadcasted_iota(jnp.int32, (C, C), 1)
    eye = (row == col).astype(F32)
    strict = row > col
    incl = row >= col

    n_chunks = rt_ref.shape[0] // C
    heads = rt_ref.shape[1] // N
    items = [(j, q) for j in range(heads) for q in range(n_chunks)]
    G = range(len(items))

    def tile(ref, j, q):
        return ref[q * C:(q + 1) * C, j * N:(j + 1) * N]

    def par(ref, j):
        return ref[:, j * N:(j + 1) * N]

    at = [tile(at_ref, j, q) for j, q in items]
    bt = [tile(bt_ref, j, q) for j, q in items]
    kt = [tile(kt_ref, j, q) for j, q in items]
    rt = [tile(rt_ref, j, q) for j, q in items]
    v = [tile(v_ref, j, q) for j, q in items]
    pc = [pc_ref[q, :, j * N:(j + 1) * N] for j, q in items]
    vb = [x.astype(BF16) for x in v]
    btb = [x.astype(BF16) for x in bt]
    ktb = [x.astype(BF16) for x in kt]
    rtb = [x.astype(BF16) for x in rt]
    ar = [jnp.concatenate([at[g], rtb[g]], axis=0) for g in G]
    xb_ = [_dot_nt(ar[g], btb[g]) for g in G]
    xk_ = [_dot_nt(ar[g], ktb[g]) for g in G]
    n_ab = [jnp.where(strict, xb_[g][:C], 0.0) for g in G]
    a_ak = [jnp.where(strict, xk_[g][:C], 0.0).astype(BF16) for g in G]
    a_rb = [jnp.where(incl, xb_[g][C:], 0.0).astype(BF16) for g in G]
    a_rk = [jnp.where(incl, xk_[g][C:], 0.0).astype(BF16) for g in G]
    akv = [_dot(a_ak[g], vb[g]) for g in G]
    t = None
    b = 1
    while b < C:
        sh = (2 * b).bit_length() - 1
        low_left = ((row >> sh) == (col >> sh)) & ((row & b) != 0) & ((col & b) == 0)
        nb = [jnp.where(low_left, n_ab[g], 0.0) for g in G]
        if b == 1:
            t = [eye + nb[g] for g in G]
        else:
            tb = [t[g].astype(BF16) for g in G]
            z = [_dot(nb[g].astype(BF16), tb[g]).astype(BF16) for g in G]
            t = [t[g] + _dot(tb[g], z[g]) for g in G]
        b *= 2
    tb = [t[g].astype(BF16) for g in G]
    wm = [_dot(tb[g], at[g]).astype(BF16) for g in G]
    u0 = [_dot(tb[g], akv[g].astype(BF16)).astype(BF16) for g in G]
    rm = [(rt[g] + _dot(a_rb[g], wm[g])).astype(BF16) for g in G]
    y0 = [_dot(a_rb[g], u0[g]) + _dot(a_rk[g], vb[g]) for g in G]
    bp = [(bt[g] * pc[g]).astype(BF16) for g in G]
    kp = [(kt[g] * pc[g]).astype(BF16) for g in G]
    mp = [_dot_tn(wm[g], bp[g]).astype(BF16) for g in G]
    s_add = [_dot_tn(u0[g], bp[g]) + _dot_tn(vb[g], kp[g]) for g in G]

    for j in range(heads):
        s = state_ref[j]
        for q in range(n_chunks):
            g = j * n_chunks + q
            sb = s.astype(BF16)
            y = _dot_nt(rm[g], sb) + y0[g]
            s = s * pc[g] + _dot(sb, mp[g]) + s_add[g]
            mean = jnp.mean(y, axis=-1, keepdims=True)
            yc = y - mean
            var = jnp.mean(yc * yc, axis=-1, keepdims=True)
            yn = yc * lax.rsqrt(var + LNX_EPS) * par(lnw_ref, j) + par(lnb_ref, j)
            yf_ref[q * C:(q + 1) * C, j * N:(j + 1) * N] = (yn + tile(bonus_ref, j, q)) * tile(g_ref, j, q)
        state_ref[j] = s
    y_ref[...] = yf_ref[...].astype(y_ref.dtype)


def rwkv_scan(at, bt, kt, rt, v, g, bonus, pc, lnx_w, lnx_b, *, rows=SCAN_ROWS, width=SCAN_WIDTH):
    S, W = rt.shape
    rows = min(rows, S)
    hp = width // HEAD_N
    row = lambda x: x.reshape(1, W).astype(F32)
    seq = pl.BlockSpec((rows, width), lambda h, c: (c, h))
    par = pl.BlockSpec((1, width), lambda h, c: (0, h))
    return pl.pallas_call(
        _rwkv_scan_kernel,
        out_shape=jax.ShapeDtypeStruct((S, W), BF16),
        grid=(W // width, S // rows),
        in_specs=[seq] * 7 + [pl.BlockSpec((rows // CHUNK, 1, width), lambda h, c: (c, 0, h))] + [par] * 2,
        out_specs=seq,
        scratch_shapes=[pltpu.VMEM((hp, HEAD_N, HEAD_N), F32), pltpu.VMEM((rows, width), F32)],
        compiler_params=_params("parallel", "arbitrary"),
        name="rwkv_scan",
    )(at, bt, kt, rt, v, g, bonus, pc, row(lnx_w), row(lnx_b))


def _diff_proj_kernel(x_ref, g_ref, w_ref, pos_ref, freq_ref, o_ref, xs_ref, cos_ref, sin_ref):
    j = pl.program_id(1)
    half = DIFF_QK // 2

    def first_half(shape):
        return (lax.broadcasted_iota(jnp.int32, shape, 1) & (DIFF_QK - 1)) < half

    @pl.when(j == 0)
    def _():
        x = x_ref[...]
        x = x * lax.rsqrt(jnp.mean(x * x, axis=-1, keepdims=True) + NORM_EPS) * g_ref[...]
        xs_ref[...] = x.astype(BF16)
        ang = pos_ref[...] * freq_ref[...]
        cos_ref[...] = jnp.cos(ang)
        sin_ref[...] = jnp.where(first_half(ang.shape), -1.0, 1.0) * jnp.sin(ang)

    @pl.when(j < 2)
    def _():
        sc = jnp.where(j == 0, DIFF_QK ** -0.5, 1.0)
        cos = cos_ref[...]
        sin = sin_ref[...]
        fh = first_half(cos.shape)
        for c0 in range(0, o_ref.shape[1], MXU_WIDTH):
            acc = jnp.dot(xs_ref[...], w_ref[:, c0:c0 + MXU_WIDTH], preferred_element_type=F32)
            for b0 in range(0, MXU_WIDTH, LANES):
                x = acc[:, b0:b0 + LANES]
                partner = jnp.where(fh, pltpu.roll(x, LANES - half, axis=1), pltpu.roll(x, half, axis=1))
                o_ref[:, c0 + b0:c0 + b0 + LANES] = ((x * cos + partner * sin) * sc).astype(o_ref.dtype)

    @pl.when(j == 2)
    def _():
        o_ref[...] = jnp.dot(xs_ref[...], w_ref[...], preferred_element_type=F32).astype(o_ref.dtype)


def diff_proj(x, gain, w, positions, *, tm=1024):
    S, K = x.shape
    W = DIFF_WIDTH
    tm = min(tm, S)
    assert w.shape == (K, 3 * W) and S % tm == 0
    inv_freq = ROPE_THETA ** (-(jnp.arange(0, DIFF_QK, 2, dtype=F32) / DIFF_QK))
    freq = jnp.tile(inv_freq, LANES // (DIFF_QK // 2)).reshape(1, LANES)
    pos = positions.reshape(S, 1).astype(F32)
    return pl.pallas_call(
        _diff_proj_kernel,
        out_shape=jax.ShapeDtypeStruct((S, 3 * W), BF16),
        grid=(S // tm, 3),
        in_specs=[pl.BlockSpec((tm, K), lambda i, j: (i, 0)),
                  pl.BlockSpec((1, K), lambda i, j: (0, 0)),
                  pl.BlockSpec((K, W), lambda i, j: (0, j)),
                  pl.BlockSpec((tm, 1), lambda i, j: (i, 0)),
                  pl.BlockSpec((1, LANES), lambda i, j: (0, 0))],
        out_specs=pl.BlockSpec((tm, W), lambda i, j: (i, j)),
        scratch_shapes=[pltpu.VMEM((tm, K), BF16), pltpu.VMEM((tm, LANES), F32), pltpu.VMEM((tm, LANES), F32)],
        compiler_params=_params("parallel", "arbitrary"),
        name="diff_proj",
    )(x, gain.reshape(1, K).astype(F32), w, pos, freq)


def _diff_flash_kernel(qi_tab, ki_tab, q_ref, k_ref, v_ref, lam_ref, sw_ref, o_ref,
                       m_ref, l_ref, acc_ref, s_ref, p_ref, alpha_ref, *, tile, row_block, diag_parts, lambda_init):
    p = pl.program_id(1)
    qi = qi_tab[p]
    ki = ki_tab[p]

    @pl.when(ki == 0)
    def _():
        m_ref[...] = jnp.full_like(m_ref, -jnp.inf)
        l_ref[...] = jnp.zeros_like(l_ref)
        acc_ref[...] = jnp.zeros_like(acc_ref)

    def process(row0, nrows, kmax, masked):
        rows = slice(row0, row0 + nrows)
        q = q_ref[rows, :]
        k = k_ref[0:kmax, :]
        v = v_ref[0:kmax, :]
        lane = lax.broadcasted_iota(jnp.int32, q.shape, 1)
        reps = kmax // LANES
        for c in range(2):
            qc = jnp.where((lane >> (DIFF_QK.bit_length() - 1)) == c, q, jnp.zeros_like(q))
            s_ref[0:nrows, 0:kmax] = _dot_nt(qc, k)
            for r0 in range(0, nrows, row_block):
                ar = slice(row0 + r0, row0 + r0 + row_block)
                lr = slice(r0, r0 + row_block)
                s = s_ref[lr, 0:kmax]
                if masked:
                    qpos = lax.broadcasted_iota(jnp.int32, s.shape, 0) + (row0 + r0)
                    kpos = lax.broadcasted_iota(jnp.int32, s.shape, 1)
                    s = jnp.where(kpos <= qpos, s, -jnp.inf)
                m_old = m_ref[c, ar, :]
                m_new = jnp.maximum(m_old, jnp.max(s, axis=-1, keepdims=True))
                alpha = jnp.exp(m_old - m_new)
                pr = jnp.exp(s - jnp.concatenate([m_new] * reps, axis=1))
                l_ref[c, ar, :] = alpha * l_ref[c, ar, :] + jnp.sum(pr, axis=-1, keepdims=True)
                m_ref[c, ar, :] = m_new
                alpha_ref[lr, :] = alpha
                p_ref[lr, 0:kmax] = pr.astype(BF16)
            acc_ref[c, rows, :] = (alpha_ref[0:nrows, :] * acc_ref[c, rows, :]
                                   + _dot(p_ref[0:nrows, 0:kmax], v))

    @pl.when(ki < qi)
    def _():
        process(0, tile, tile, False)

    @pl.when(ki == qi)
    def _():
        part = tile // diag_parts
        for a in range(diag_parts):
            process(a * part, part, (a + 1) * part, True)
        lp = lam_ref[...]
        lam = (jnp.exp(jnp.sum(lp[0:1] * lp[1:2], axis=-1, keepdims=True))
               - jnp.exp(jnp.sum(lp[2:3] * lp[3:4], axis=-1, keepdims=True)) + lambda_init)
        o = acc_ref[0] / l_ref[0] - lam * (acc_ref[1] / l_ref[1])
        o = o * lax.rsqrt(jnp.mean(o * o, axis=-1, keepdims=True) + SUBLN_EPS)
        o_ref[...] = (o * sw_ref[...] * (1.0 - lambda_init)).astype(o_ref.dtype)


def diff_flash(qkv, lambdas, subln_w, lambda_init, *, tile=1024, row_block=32, diag_parts=2):
    S = qkv.shape[0]
    tile = min(tile, S)
    assert S % tile == 0 and (tile // diag_parts) % LANES == 0 and (tile // diag_parts) % row_block == 0
    pairs = [(qi, ki) for qi in range(S // tile) for ki in range(qi + 1)]
    qi_tab = jnp.asarray([p[0] for p in pairs], jnp.int32)
    ki_tab = jnp.asarray([p[1] for p in pairs], jnp.int32)
    grid_spec = pltpu.PrefetchScalarGridSpec(
        num_scalar_prefetch=2,
        grid=(DIFF_HEADS, len(pairs)),
        in_specs=[pl.BlockSpec((tile, LANES), lambda h, p, qt, kt: (qt[p], h)),
                  pl.BlockSpec((tile, LANES), lambda h, p, qt, kt: (kt[p], DIFF_HEADS + h)),
                  pl.BlockSpec((tile, LANES), lambda h, p, qt, kt: (kt[p], 2 * DIFF_HEADS + h)),
                  pl.BlockSpec((4, DIFF_QK), lambda h, p, qt, kt: (0, 0)),
                  pl.BlockSpec((1, DIFF_V), lambda h, p, qt, kt: (0, 0))],
        out_specs=pl.BlockSpec((tile, LANES), lambda h, p, qt, kt: (qt[p], h)),
        scratch_shapes=[pltpu.VMEM((2, tile, LANES), F32),
                        pltpu.VMEM((2, tile, LANES), F32),
                        pltpu.VMEM((2, tile, DIFF_V), F32),
                        pltpu.VMEM((tile, tile), F32),
                        pltpu.VMEM((tile, tile), BF16),
                        pltpu.VMEM((tile, LANES), F32)],
    )
    return pl.pallas_call(
        functools.partial(_diff_flash_kernel, tile=tile, row_block=row_block, diag_parts=diag_parts,
                          lambda_init=lambda_init),
        out_shape=jax.ShapeDtypeStruct((S, DIFF_WIDTH), BF16),
        grid_spec=grid_spec,
        compiler_params=_params("parallel", "arbitrary"),
        name="diff_flash",
    )(qi_tab, ki_tab, qkv, qkv, qkv, lambdas.astype(F32), subln_w.reshape(1, DIFF_V).astype(F32))


def _mem_attn_kernel(x_ref, g_ref, w_ref, k_ref, v_ref, o_ref, xs_ref):
    @pl.when(pl.program_id(1) == 0)
    def _():
        x = x_ref[...]
        x = x * lax.rsqrt(jnp.mean(x * x, axis=-1, keepdims=True) + NORM_EPS) * g_ref[...]
        xs_ref[...] = x.astype(BF16)

    q = jnp.dot(xs_ref[...], w_ref[...], preferred_element_type=F32).astype(BF16)
    s = _dot_nt(q, k_ref[...]) * (q.shape[1] ** -0.5)
    s = s - jnp.max(s, axis=-1, keepdims=True)
    e = jnp.exp(s)
    pr = e / jnp.sum(e, axis=-1, keepdims=True)
    o_ref[...] = _dot(pr.astype(BF16), v_ref[...]).astype(o_ref.dtype)


def mem_attn(x, gain, wq, k, v, *, tm=1024):
    S, D = x.shape
    M = k.shape[0]
    hd = D // MEM_HEADS
    tm = min(tm, S)
    return pl.pallas_call(
        _mem_attn_kernel,
        out_shape=jax.ShapeDtypeStruct((S, D), BF16),
        grid=(S // tm, MEM_HEADS),
        in_specs=[pl.BlockSpec((tm, D), lambda i, j: (i, 0)),
                  pl.BlockSpec((1, D), lambda i, j: (0, 0)),
                  pl.BlockSpec((D, hd), lambda i, j: (0, j)),
                  pl.BlockSpec((M, hd), lambda i, j: (0, j)),
                  pl.BlockSpec((M, hd), lambda i, j: (0, j))],
        out_specs=pl.BlockSpec((tm, hd), lambda i, j: (i, j)),
        scratch_shapes=[pltpu.VMEM((tm, D), BF16)],
        compiler_params=_params("parallel", "arbitrary"),
        name="mem_attn",
    )(x, gain.reshape(1, D).astype(F32), wq, k, v)


def _first_argmax(x, lane, big):
    m = jnp.max(x, axis=-1, keepdims=True)
    idx = jnp.min(jnp.where(x == m, lane, big), axis=-1, keepdims=True)
    return m, idx


def _router_kernel(h_ref, gain_ref, wg_ref, bg_ref, we_ref, be_ref, hn_ref, eid_ref, ew_ref, cnt_ref):
    @pl.when(pl.program_id(0) == 0)
    def _():
        cnt_ref[...] = jnp.zeros_like(cnt_ref)

    x = h_ref[...]
    hn = x * lax.rsqrt(jnp.mean(x * x, axis=-1, keepdims=True) + NORM_EPS) * gain_ref[...]
    hn_ref[...] = hn
    hn_hi = hn.astype(BF16)
    hn_lo = (hn - hn_hi.astype(F32)).astype(BF16)

    def logits(w_ref, b_ref):
        w = w_ref[...]
        w_hi = w.astype(BF16)
        w_lo = (w - w_hi.astype(F32)).astype(BF16)
        return _dot(hn_hi, w_hi) + (_dot(hn_hi, w_lo) + _dot(hn_lo, w_hi)) + b_ref[...]
    g_logits = logits(wg_ref, bg_ref)
    e_logits = logits(we_ref, be_ref)
    tm = x.shape[0]
    lane_g = lax.broadcasted_iota(jnp.int32, (tm, N_GROUPS), 1)
    g_max, g_idx = _first_argmax(g_logits, lane_g, N_GROUPS)
    g_w = 1.0 / jnp.sum(jnp.exp(g_logits - g_max), axis=-1, keepdims=True)
    lane_e = lax.broadcasted_iota(jnp.int32, (tm, N_EXPERTS), 1)
    in_group = (lane_e // EXPERTS_PER_GROUP) == g_idx
    el = jnp.where(in_group, e_logits, -jnp.inf)
    e_max = jnp.max(el, axis=-1, keepdims=True)
    ex = jnp.exp(el - e_max)
    prob = ex / jnp.sum(ex, axis=-1, keepdims=True)
    prob = jnp.where(in_group, prob, -1.0)
    p1, i1 = _first_argmax(prob, lane_e, N_EXPERTS)
    p2, i2 = _first_argmax(jnp.where(lane_e == i1, -1.0, prob), lane_e, N_EXPERTS)
    tot = p1 + p2
    lane_o = lax.broadcasted_iota(jnp.int32, (tm, LANES), 1)
    eid_ref[...] = jnp.where(lane_o == 0, i1, jnp.where(lane_o == 1, i2, 0))
    ew_ref[...] = jnp.where(lane_o == 0, g_w * (p1 / tot), jnp.where(lane_o == 1, g_w * (p2 / tot), 0.0))
    chosen = ((lane_o == i1) | (lane_o == i2)).astype(F32)
    cnt_ref[...] = cnt_ref[...] + jnp.sum(chosen, axis=0, keepdims=True).astype(jnp.int32)


def router(h, gain, wg, bg, we, be, *, tm=512):
    S, D = h.shape
    tm = min(tm, S)
    full = lambda a: pl.BlockSpec(a.shape, lambda i: (0, 0))
    args = [h, gain.reshape(1, D).astype(F32), wg.astype(F32), bg.reshape(1, -1).astype(F32),
            we.astype(F32), be.reshape(1, -1).astype(F32)]
    return pl.pallas_call(
        _router_kernel,
        out_shape=[jax.ShapeDtypeStruct((S, D), F32),
                   jax.ShapeDtypeStruct((S, LANES), jnp.int32),
                   jax.ShapeDtypeStruct((S, LANES), F32),
                   jax.ShapeDtypeStruct((1, LANES), jnp.int32)],
        grid=(S // tm,),
        in_specs=[pl.BlockSpec((tm, D), lambda i: (i, 0))] + [full(a) for a in args[1:]],
        out_specs=[pl.BlockSpec((tm, D), lambda i: (i, 0)),
                   pl.BlockSpec((tm, LANES), lambda i: (i, 0)),
                   pl.BlockSpec((tm, LANES), lambda i: (i, 0)),
                   pl.BlockSpec((1, LANES), lambda i: (0, 0))],
        compiler_params=_params("arbitrary"),
        name="router",
    )(*args)


def _plan_kernel(cnt_ref, blk_e_ref, blk_n_ref, start_ref):
    nb = blk_e_ref.shape[0]
    sh = MOE_ROWS.bit_length() - 1

    def per_expert(e, b):
        n = cnt_ref[e]
        start_ref[e] = b

        def per_block(t, _):
            blk_e_ref[b + t] = e
            blk_n_ref[b + t] = jnp.minimum(n - t * MOE_ROWS, MOE_ROWS)
            return 0
        k = (n + (MOE_ROWS - 1)) >> sh
        lax.fori_loop(0, k, per_block, 0)
        return b + k
    total = lax.fori_loop(0, N_EXPERTS, per_expert, 0)

    def rest(e, _):
        start_ref[e] = total
        return 0
    lax.fori_loop(N_EXPERTS, start_ref.shape[0], rest, 0)
    last_e = blk_e_ref[jnp.maximum(total - 1, 0)]

    def tail(b, _):
        blk_e_ref[b] = last_e
        blk_n_ref[b] = 0
        return 0
    lax.fori_loop(total, nb, tail, 0)


def block_plan(counts, n_assign):
    nb = (n_assign + N_EXPERTS * (MOE_ROWS - 1)) // MOE_ROWS
    smem = pl.BlockSpec(memory_space=pltpu.SMEM)
    return pl.pallas_call(
        _plan_kernel,
        out_shape=[jax.ShapeDtypeStruct((nb,), jnp.int32), jax.ShapeDtypeStruct((nb,), jnp.int32),
                   jax.ShapeDtypeStruct((LANES,), jnp.int32)],
        in_specs=[smem],
        out_specs=[smem, smem, smem],
        name="block_plan",
    )(counts.reshape(LANES))


def _dest_kernel(eid_ref, start_ref, dest_ref, carry_ref):
    @pl.when(pl.program_id(0) == 0)
    def _():
        carry_ref[...] = jnp.zeros_like(carry_ref)

    eid = eid_ref[...]
    tm = eid.shape[0]
    lane = lax.broadcasted_iota(jnp.int32, (tm, LANES), 1)
    oh0 = lane == eid[:, 0:1]
    oh1 = lane == eid[:, 1:2]
    both = jnp.where(oh0 | oh1, 1.0, 0.0).astype(BF16)
    r = lax.broadcasted_iota(jnp.int32, (tm, tm), 0)
    c = lax.broadcasted_iota(jnp.int32, (tm, tm), 1)
    before = _dot(jnp.where(r > c, 1.0, 0.0).astype(BF16), both) + carry_ref[...]
    pos = before + (start_ref[...] * MOE_ROWS).astype(F32)
    d0 = jnp.sum(jnp.where(oh0, pos, 0.0), axis=-1, keepdims=True).astype(jnp.int32)
    d1 = jnp.sum(jnp.where(oh1, pos, 0.0), axis=-1, keepdims=True).astype(jnp.int32)
    dest_ref[...] = jnp.where(lane == 0, d0, jnp.where(lane == 1, d1, 0))
    carry_ref[...] = carry_ref[...] + jnp.sum(both.astype(F32), axis=0, keepdims=True)


def assignment_dest(eid, blk_start, *, tm=512):
    S = eid.shape[0]
    tm = min(tm, S)
    return pl.pallas_call(
        _dest_kernel,
        out_shape=jax.ShapeDtypeStruct((S, LANES), jnp.int32),
        grid=(S // tm,),
        in_specs=[pl.BlockSpec((tm, LANES), lambda i: (i, 0)), pl.BlockSpec((1, LANES), lambda i: (0, 0))],
        out_specs=pl.BlockSpec((tm, LANES), lambda i: (i, 0)),
        scratch_shapes=[pltpu.VMEM((1, LANES), F32)],
        compiler_params=_params("arbitrary"),
        name="assignment_dest",
    )(eid, blk_start.reshape(1, LANES))


def _for_rows(n, fn):
    sh = MOE_DMA_UNROLL.bit_length() - 1

    def group(t, _):
        for u in range(MOE_DMA_UNROLL):
            fn(t * MOE_DMA_UNROLL + u)
        return 0
    lax.fori_loop(0, n >> sh, group, 0)

    def one(r, _):
        fn(r)
        return 0
    lax.fori_loop((n >> sh) << sh, n, one, 0)


def _experts_kernel(blk_e, blk_n, dest, hn_hbm, wg_hbm, wu_hbm, wd_hbm, out_hbm,
                    xg_ref, xb_ref, acc_ref, wg_ring, wu_ring, wd_ring, wgb_ref, wub_ref, wdb_ref, inv_ref,
                    gsem, ssem, wsem):
    i = pl.program_id(0)
    j = pl.program_id(1)
    nb = pl.num_programs(0)
    nsplit = pl.num_programs(1)
    n_tok = hn_hbm.shape[0]
    k_shift = TOP_K.bit_length() - 1
    de = wgb_ref.shape[1]
    n_slots = wg_ring.shape[0]

    def weight_copies(b, jj):
        slot = lax.rem(b * MOE_SPLIT + jj, n_slots)
        e = blk_e[b]
        c0 = pl.multiple_of(jj * de, de)
        return (pltpu.make_async_copy(wg_hbm.at[e, :, pl.ds(c0, de)], wg_ring.at[slot], wsem.at[slot]),
                pltpu.make_async_copy(wu_hbm.at[e, :, pl.ds(c0, de)], wu_ring.at[slot], wsem.at[slot]),
                pltpu.make_async_copy(wd_hbm.at[e, pl.ds(c0, de), :], wd_ring.at[slot], wsem.at[slot]))

    def start_weights(t):
        b = lax.div(t, MOE_SPLIT)
        jj = lax.rem(t, MOE_SPLIT)
        bb = jnp.minimum(b, nb - 1)

        @pl.when((b < nb) & (blk_n[bb] > 0))
        def _():
            for cp in weight_copies(bb, jj):
                cp.start()

    def gather(b, r):
        tok = inv_ref[b * MOE_ROWS + r] >> k_shift
        return pltpu.make_async_copy(hn_hbm.at[pl.ds(tok, 1)], xg_ref.at[pl.ds(r, 1)], gsem)

    def scatter(b, r):
        a = inv_ref[b * MOE_ROWS + r]
        row = (a & (TOP_K - 1)) * n_tok + (a >> k_shift)
        return pltpu.make_async_copy(acc_ref.at[pl.ds(r, 1)], out_hbm.at[pl.ds(row, 1)], ssem)

    def n_gather(b):
        return pl.multiple_of(((blk_n[b] + 7) >> 3) << 3, 8)

    def start_gather(b):
        def pad(r, _):
            inv_ref[b * MOE_ROWS + r] = 0
            return 0
        lax.fori_loop(blk_n[b], n_gather(b), pad, 0)
        _for_rows(n_gather(b), lambda r: gather(b, r).start(priority=1))

    def wait_gather(b):
        n = n_gather(b)

        @pl.when(n > 0)
        def _():
            pltpu.make_async_copy(hn_hbm.at[pl.ds(0, n)], xg_ref.at[pl.ds(0, n)], gsem).wait()

    def start_scatter(b):
        _for_rows(blk_n[b], lambda r: scatter(b, r).start(priority=1))

    def wait_scatter(b):
        n8 = pl.multiple_of((blk_n[b] >> 3) << 3, 8)

        @pl.when(n8 > 0)
        def _():
            pltpu.make_async_copy(acc_ref.at[pl.ds(0, n8)], out_hbm.at[pl.ds(0, n8)], ssem).wait()

        def one(r, _):
            scatter(b, r).wait()
            return 0
        lax.fori_loop(n8, blk_n[b], one, 0)

    n_units = (blk_n[i] + (MOE_UNIT - 1)) >> (MOE_UNIT.bit_length() - 1)

    @pl.when(j == 0)
    def _():
        @pl.when(i == 0)
        def _():
            def invert(a):
                inv_ref[dest[a]] = a
            _for_rows(dest.shape[0], invert)
            xg_ref[...] = jnp.zeros_like(xg_ref)
            start_gather(0)
        wait_gather(i)

        def cast(sb, _):
            rs = pl.ds(pl.multiple_of(sb * MOE_UNIT, MOE_UNIT), MOE_UNIT)
            xb_ref[rs, :] = xg_ref[rs, :].astype(BF16)
            return 0
        lax.fori_loop(0, n_units, cast, 0)

        @pl.when(i + 1 < nb)
        def _():
            start_gather(i + 1)

        @pl.when(i > 0)
        def _():
            wait_scatter(i - 1)

    t = i * MOE_SPLIT + j

    @pl.when(t == 0)
    def _():
        for t0 in range(n_slots):
            start_weights(jnp.int32(t0))

    @pl.when(n_units > 0)
    def _():
        for cp in weight_copies(i, j):
            cp.wait()
        slot = lax.rem(t, n_slots)
        wgb_ref[...] = wg_ring[slot].astype(BF16)
        wub_ref[...] = wu_ring[slot].astype(BF16)
        wdb_ref[...] = wd_ring[slot].astype(BF16)

    start_weights(t + n_slots)

    def compute(row0, nrows):
        rs = pl.ds(row0, nrows)
        xb = xb_ref[rs, :]
        gate = _dot(xb, wgb_ref[...])
        hid = (gate * _sigmoid(gate)) * _dot(xb, wub_ref[...])
        part = _dot(hid.astype(BF16), wdb_ref[...])

        @pl.when(j == 0)
        def _():
            acc_ref[rs, :] = part

        @pl.when(j > 0)
        def _():
            acc_ref[rs, :] = acc_ref[rs, :] + part

    for units in range(1, MOE_ROWS // MOE_UNIT + 1):
        @pl.when(n_units == units)
        def _():
            compute(0, units * MOE_UNIT)

    @pl.when(j == nsplit - 1)
    def _():
        start_scatter(i)

        @pl.when(i == nb - 1)
        def _():
            wait_scatter(i)


def experts(hn, blk_e, blk_n, dest, w_gate, w_up, w_down):
    S, D = hn.shape
    E, _, DE = w_gate.shape
    nb = blk_e.shape[0]
    de = DE // MOE_SPLIT
    any_space = pl.BlockSpec(memory_space=pl.ANY)
    grid_spec = pltpu.PrefetchScalarGridSpec(
        num_scalar_prefetch=3,
        grid=(nb, MOE_SPLIT),
        in_specs=[any_space] * 4,
        out_specs=any_space,
        scratch_shapes=[pltpu.VMEM((MOE_ROWS, D), F32),
                        pltpu.VMEM((MOE_ROWS, D), BF16),
                        pltpu.VMEM((MOE_ROWS, D), F32),
                        pltpu.VMEM((MOE_WEIGHT_SLOTS, D, de), F32),
                        pltpu.VMEM((MOE_WEIGHT_SLOTS, D, de), F32),
                        pltpu.VMEM((MOE_WEIGHT_SLOTS, de, D), F32),
                        pltpu.VMEM((D, de), BF16),
                        pltpu.VMEM((D, de), BF16),
                        pltpu.VMEM((de, D), BF16),
                        pltpu.SMEM((nb * MOE_ROWS,), jnp.int32),
                        pltpu.SemaphoreType.DMA(()),
                        pltpu.SemaphoreType.DMA(()),
                        pltpu.SemaphoreType.DMA((MOE_WEIGHT_SLOTS,))],
    )
    return pl.pallas_call(
        _experts_kernel,
        out_shape=jax.ShapeDtypeStruct((TOP_K * S, D), F32),
        grid_spec=grid_spec,
        compiler_params=_params("arbitrary", "arbitrary"),
        name="experts",
    )(blk_e, blk_n, dest, hn, w_gate, w_up, w_down)


def _combine_kernel(*refs, has_norm):
    h_ref, y0_ref, y1_ref, w_ref = refs[:4]
    o_ref = refs[-1]
    w = w_ref[...]
    x = h_ref[...] + (y0_ref[0] * w[:, 0:1] + y1_ref[0] * w[:, 1:2])
    if has_norm:
        x = x * lax.rsqrt(jnp.mean(x * x, axis=-1, keepdims=True) + NORM_EPS) * refs[4][...]
    o_ref[...] = x


def combine(h, y_slots, slot_w, gain=None, *, tm=512):
    S, D = h.shape
    tm = min(tm, S)
    y3 = y_slots.reshape(TOP_K, S, D)
    in_specs = [pl.BlockSpec((tm, D), lambda i: (i, 0)),
                pl.BlockSpec((1, tm, D), lambda i: (0, i, 0)),
                pl.BlockSpec((1, tm, D), lambda i: (1, i, 0)),
                pl.BlockSpec((tm, LANES), lambda i: (i, 0))]
    args = [h, y3, y3, slot_w]
    if gain is not None:
        in_specs.append(pl.BlockSpec((1, D), lambda i: (0, 0)))
        args.append(gain.reshape(1, D).astype(F32))
    return pl.pallas_call(
        functools.partial(_combine_kernel, has_norm=gain is not None),
        out_shape=jax.ShapeDtypeStruct((S, D), F32),
        grid=(S // tm,),
        in_specs=in_specs,
        out_specs=pl.BlockSpec((tm, D), lambda i: (i, 0)),
        compiler_params=_params("parallel"),
        name="combine",
    )(*args)


def moe(h, gain, wg, bg, we, be, w_gate, w_up, w_down, final_gain=None):
    S = h.shape[0]
    hn, eid, slot_w, counts = router(h, gain, wg, bg, we, be)
    blk_e, blk_n, blk_start = block_plan(counts, S * TOP_K)
    dest = assignment_dest(eid, blk_start)[:, :TOP_K].reshape(S * TOP_K)
    y_slots = experts(hn, blk_e, blk_n, dest, w_gate, w_up, w_down)
    return combine(h, y_slots, slot_w, final_gain)


def kernel(x, mem, positions, mix_norm, w_in, shift_mu, decay_w0, decay_w2, aaa_a0, aaa_a2, gate_g2, k_k, k_a,
           r_k, lnx_w, lnx_b, diff_lambda, subln_w, w_out, mem_q_norm, mem_kv_norm, wq_mem, wk_mem, wv_mem,
           wo_mem, moe_norm, router_group_w, router_group_b, router_expert_w, router_expert_b, expert_gate,
           expert_up, expert_down, final_norm):
    B, S, D = x.shape
    depth = w_in.shape[0]
    outs = []
    for b in range(B):
        h = x[b]
        memb = mem[b]
        for l in range(depth):
            lambda_init = 0.8 - 0.6 * math.exp(-0.3 * l)
            w_in_b = w_in[l].astype(BF16)
            proj_r = matmul(h, w_in_b[:, :RWKV_COLS], gain=mix_norm[l], tm=1024, tn=RWKV_COLS // 2)
            qkv = diff_proj(h, mix_norm[l], w_in_b[:, RWKV_COLS:], positions[b])
            pre = rwkv_prep(proj_r, shift_mu[l], decay_w0[l], decay_w2[l], aaa_a0[l], aaa_a2[l], gate_g2[l],
                            k_k[l], k_a[l], r_k[l])
            y_rwkv = rwkv_scan(*pre, lnx_w[l], lnx_b[l])
            y_diff = diff_flash(qkv, diff_lambda[l], subln_w[l], lambda_init)
            h = matmul([y_rwkv, y_diff], w_out[l].astype(BF16), residual=h, tm=1024, tn=1024)
            km = matmul(memb, wk_mem[l].astype(BF16), gain=mem_kv_norm[l], out_dtype=BF16)
            vm = matmul(memb, wv_mem[l].astype(BF16), gain=mem_kv_norm[l], out_dtype=BF16)
            o = mem_attn(h, mem_q_norm[l], wq_mem[l].astype(BF16), km, vm)
            h = matmul(o, wo_mem[l].astype(BF16), residual=h, tm=1024, tn=1024)
            h = moe(h, moe_norm[l], router_group_w[l], router_group_b[l], router_expert_w[l], router_expert_b[l],
                    expert_gate[l], expert_up[l], expert_down[l], final_norm if l == depth - 1 else None)
        outs.append(h)
    return jnp.stack(outs, axis=0)
```

```python
import functools
import math

import numpy as np
import jax
import jax.numpy as jnp
from jax import lax
from jax.experimental import pallas as pl
from jax.experimental.pallas import tpu as pltpu

F32 = jnp.float32
BF16 = jnp.bfloat16
HIGHEST = lax.Precision.HIGHEST

RWKV_HEADS = 16
HEAD_N = 64
RWKV_WIDTH = RWKV_HEADS * HEAD_N
DECAY_LORA = 64
AAA_LORA = 64
GATE_LORA = 128
RWKV_COLS = 3 * RWKV_WIDTH + DECAY_LORA + AAA_LORA + GATE_LORA
DIFF_HEADS = 8
DIFF_QK = 64
DIFF_V = 128
DIFF_WIDTH = DIFF_HEADS * DIFF_V
ROPE_THETA = 10000.0
MEM_HEADS = 4
N_GROUPS = 8
EXPERTS_PER_GROUP = 8
N_EXPERTS = N_GROUPS * EXPERTS_PER_GROUP
TOP_K = 2
NORM_EPS = 1e-6
LNX_EPS = 64e-5
SUBLN_EPS = 1e-5

LANES = 128
MXU_WIDTH = 256
VMEM_LIMIT = 56 * 1024 * 1024

CHUNK = 64
SCAN_ROWS = 256
SCAN_WIDTH = 512
MOE_ROWS = 512
MOE_UNIT = 128
MOE_SPLIT = 2
MOE_DMA_UNROLL = 8
MOE_WEIGHT_SLOTS = 2


def _params(*sem):
    return pltpu.CompilerParams(dimension_semantics=sem, vmem_limit_bytes=VMEM_LIMIT)


def _mm_kernel(*refs, n_x, has_norm, has_res):
    it = iter(refs)
    x_refs = [next(it) for _ in range(n_x)]
    g_ref = next(it) if has_norm else None
    w_ref = next(it)
    r_ref = next(it) if has_res else None
    o_ref = next(it)
    xs_ref = next(it)

    @pl.when(pl.program_id(1) == 0)
    def _():
        if has_norm:
            x = x_refs[0][...].astype(F32)
            x = x * lax.rsqrt(jnp.mean(x * x, axis=-1, keepdims=True) + NORM_EPS) * g_ref[...]
            xs_ref[...] = x.astype(BF16)
        else:
            k0 = 0
            for x_ref in x_refs:
                xs_ref[:, k0:k0 + x_ref.shape[1]] = x_ref[...].astype(BF16)
                k0 += x_ref.shape[1]

    acc = jnp.dot(xs_ref[...], w_ref[...], preferred_element_type=F32)
    if has_res:
        acc = acc + r_ref[...]
    o_ref[...] = acc.astype(o_ref.dtype)


def matmul(x, w, *, gain=None, residual=None, out_dtype=F32, tm=512, tn=512):
    xs = list(x) if isinstance(x, (list, tuple)) else [x]
    assert gain is None or len(xs) == 1
    M = xs[0].shape[0]
    K, N = w.shape
    assert sum(p.shape[1] for p in xs) == K
    tm = min(tm, M)
    assert M % tm == 0 and N % tn == 0, (M, N, tm, tn)
    in_specs = [pl.BlockSpec((tm, p.shape[1]), lambda i, j: (i, 0)) for p in xs]
    args = list(xs)
    if gain is not None:
        in_specs.append(pl.BlockSpec((1, K), lambda i, j: (0, 0)))
        args.append(gain.reshape(1, K).astype(F32))
    in_specs.append(pl.BlockSpec((K, tn), lambda i, j: (0, j)))
    args.append(w)
    if residual is not None:
        in_specs.append(pl.BlockSpec((tm, tn), lambda i, j: (i, j)))
        args.append(residual)
    return pl.pallas_call(
        functools.partial(_mm_kernel, n_x=len(xs), has_norm=gain is not None, has_res=residual is not None),
        out_shape=jax.ShapeDtypeStruct((M, N), out_dtype),
        grid=(M // tm, N // tn),
        in_specs=in_specs,
        out_specs=pl.BlockSpec((tm, tn), lambda i, j: (i, j)),
        scratch_shapes=[pltpu.VMEM((tm, K), BF16)],
        compiler_params=_params("parallel", "arbitrary"),
        name="matmul",
    )(*args)


def _sigmoid(x):
    return 1.0 / (1.0 + jnp.exp(-x))


def _rwkv_prep_kernel(p_ref, pp_ref, mu_ref, w0_ref, w2_ref, a0_ref, a2_ref, g2_ref, kk_ref, ka_ref, rk_ref,
                      hsum_ref, at_o, bt_o, kt_o, rt_o, v_o, g_o, bonus_o, pc_o):
    W = RWKV_WIDTH
    C = CHUNK
    p = p_ref[...]
    last = jnp.where(pl.program_id(0) == 0, 0.0, pp_ref[7:8, :])
    prev = pltpu.roll(p, 1, axis=0)
    row = lax.broadcasted_iota(jnp.int32, p.shape, 0)
    prev = jnp.where(row == 0, last, prev)
    ps = p + (prev - p) * mu_ref[...]
    r = ps[:, 0:W]
    k = ps[:, W:2 * W]
    v = ps[:, 2 * W:3 * W]
    o = 3 * W
    wd = ps[:, o:o + DECAY_LORA]
    ad = ps[:, o + DECAY_LORA:o + DECAY_LORA + AAA_LORA]
    gd = ps[:, o + DECAY_LORA + AAA_LORA:o + DECAY_LORA + AAA_LORA + GATE_LORA]
    z = w0_ref[...] + jnp.dot(jnp.tanh(wd).astype(BF16), w2_ref[...], preferred_element_type=F32)
    nz = -z
    softplus = jnp.maximum(nz, 0.0) + jnp.log(1.0 + jnp.exp(-jnp.abs(nz)))
    w_raw = -softplus - 0.5
    lw = -jnp.exp(w_raw)
    tm = p.shape[0]
    rr = lax.broadcasted_iota(jnp.int32, (tm, tm), 0)
    cc = lax.broadcasted_iota(jnp.int32, (tm, tm), 1)
    sh = C.bit_length() - 1
    tri = (((rr >> sh) == (cc >> sh)) & (rr >= cc)).astype(F32)
    cs = _dot(tri, lw, HIGHEST)
    a = _sigmoid(a0_ref[...] + jnp.dot(ad.astype(BF16), a2_ref[...], preferred_element_type=F32))
    g_o[...] = jnp.dot(_sigmoid(gd).astype(BF16), g2_ref[...], preferred_element_type=F32)

    def head_sum(x):
        hi = x.astype(BF16)
        lo = (x - hi.astype(F32)).astype(BF16)
        ones = hsum_ref[...]
        wb = ones.shape[0]
        cols = [slice(c, c + wb) for c in range(0, x.shape[1], wb)]
        return jnp.concatenate([_dot(hi[:, cs], ones) + _dot(lo[:, cs], ones) for cs in cols], axis=1)

    kk = k * kk_ref[...]
    kk = kk / jnp.maximum(jnp.sqrt(head_sum(kk * kk)), 1e-12)
    k2 = k * (1.0 + (a - 1.0) * ka_ref[...])
    e_neg = jnp.exp(-cs)
    at_o[...] = (-kk * jnp.exp(cs - lw)).astype(at_o.dtype)
    bt_o[...] = (kk * a) * e_neg
    kt_o[...] = k2 * e_neg
    rt_o[...] = r * jnp.exp(cs)
    v_o[...] = v
    bonus_o[...] = head_sum(r * k2 * rk_ref[...]) * v
    for q in range(tm // C):
        pc_o[q] = jnp.exp(cs[q * C + C - 1:q * C + C, :])


def rwkv_prep(p, mu, w0, w2, a0, a2, g2, k_k, k_a, r_k, *, tm=256):
    S = p.shape[0]
    tm = min(tm, S)
    assert tm % CHUNK == 0
    W = RWKV_WIDTH
    row = lambda x: x.reshape(1, -1).astype(F32)
    full = lambda a: pl.BlockSpec(a.shape, lambda i: (0, 0))
    head = jnp.arange(2 * LANES, dtype=jnp.int32) // HEAD_N
    hsum = (head[:, None] == head[None, :]).astype(BF16)
    args = [p, p, row(mu), row(w0), w2.astype(BF16), row(a0), a2.astype(BF16), g2.astype(BF16),
            row(k_k), row(k_a), row(r_k), hsum]
    in_specs = [pl.BlockSpec((tm, RWKV_COLS), lambda i: (i, 0)),
                pl.BlockSpec((8, RWKV_COLS), lambda i: (jnp.maximum(i * (tm // 8) - 1, 0), 0))]
    in_specs += [full(a) for a in args[2:]]
    f32 = jax.ShapeDtypeStruct((S, W), F32)
    blk = pl.BlockSpec((tm, W), lambda i: (i, 0))
    return pl.pallas_call(
        _rwkv_prep_kernel,
        out_shape=[jax.ShapeDtypeStruct((S, W), BF16)] + [f32] * 6 + [jax.ShapeDtypeStruct((S // CHUNK, 1, W), F32)],
        grid=(S // tm,),
        in_specs=in_specs,
        out_specs=[blk] * 7 + [pl.BlockSpec((tm // CHUNK, 1, W), lambda i: (i, 0, 0))],
        compiler_params=_params("parallel"),
        name="rwkv_prep",
    )(*args)


def _dot_nt(a, b, precision=None):
    return lax.dot_general(a, b, (((1,), (1,)), ((), ())), preferred_element_type=F32, precision=precision)


def _dot_tn(a, b, precision=None):
    return lax.dot_general(a, b, (((0,), (0,)), ((), ())), preferred_element_type=F32, precision=precision)


def _dot(a, b, precision=None):
    return jnp.dot(a, b, preferred_element_type=F32, precision=precision)


def _rwkv_scan_kernel(at_ref, bt_ref, kt_ref, rt_ref, v_ref, g_ref, bonus_ref, pc_ref, lnw_ref, lnb_ref,
                      y_ref, state_ref, yf_ref):
    C = CHUNK
    N = HEAD_N

    @pl.when(pl.program_id(1) == 0)
    def _():
        state_ref[...] = jnp.zeros_like(state_ref)

    row = lax.broadcasted_iota(jnp.int32, (C, C), 0)
    col = lax.broadcasted_iota(jnp.int32, (C, C), 1)
    eye = (row == col).astype(F32)
    strict = row > col
    incl = row >= col

    n_chunks = rt_ref.shape[0] // C
    heads = rt_ref.shape[1] // N
    items = [(j, q) for j in range(heads) for q in range(n_chunks)]
    G = range(len(items))

    def tile(ref, j, q):
        return ref[q * C:(q + 1) * C, j * N:(j + 1) * N]

    def par(ref, j):
        return ref[:, j * N:(j + 1) * N]

    at = [tile(at_ref, j, q) for j, q in items]
    bt = [tile(bt_ref, j, q) for j, q in items]
    kt = [tile(kt_ref, j, q) for j, q in items]
    rt = [tile(rt_ref, j, q) for j, q in items]
    v = [tile(v_ref, j, q) for j, q in items]
    pc = [pc_ref[q, :, j * N:(j + 1) * N] for j, q in items]
    vb = [x.astype(BF16) for x in v]
    btb = [x.astype(BF16) for x in bt]
    ktb = [x.astype(BF16) for x in kt]
    rtb = [x.astype(BF16) for x in rt]
    ar = [jnp.concatenate([at[g], rtb[g]], axis=0) for g in G]
    xb_ = [_dot_nt(ar[g], btb[g]) for g in G]
    xk_ = [_dot_nt(ar[g], ktb[g]) for g in G]
    n_ab = [jnp.where(strict, xb_[g][:C], 0.0) for g in G]
    a_ak = [jnp.where(strict, xk_[g][:C], 0.0).astype(BF16) for g in G]
    a_rb = [jnp.where(incl, xb_[g][C:], 0.0).astype(BF16) for g in G]
    a_rk = [jnp.where(incl, xk_[g][C:], 0.0).astype(BF16) for g in G]
    akv = [_dot(a_ak[g], vb[g]) for g in G]
    t = None
    b = 1
    while b < C:
        sh = (2 * b).bit_length() - 1
        low_left = ((row >> sh) == (col >> sh)) & ((row & b) != 0) & ((col & b) == 0)
        nb = [jnp.where(low_left, n_ab[g], 0.0) for g in G]
        if b == 1:
            t = [eye + nb[g] for g in G]
        else:
            tb = [t[g].astype(BF16) for g in G]
            z = [_dot(nb[g].astype(BF16), tb[g]).astype(BF16) for g in G]
            t = [t[g] + _dot(tb[g], z[g]) for g in G]
        b *= 2
    tb = [t[g].astype(BF16) for g in G]
    wm = [_dot(tb[g], at[g]).astype(BF16) for g in G]
    u0 = [_dot(tb[g], akv[g].astype(BF16)).astype(BF16) for g in G]
    rm = [(rt[g] + _dot(a_rb[g], wm[g])).astype(BF16) for g in G]
    y0 = [_dot(a_rb[g], u0[g]) + _dot(a_rk[g], vb[g]) for g in G]
    bp = [(bt[g] * pc[g]).astype(BF16) for g in G]
    kp = [(kt[g] * pc[g]).astype(BF16) for g in G]
    mp = [_dot_tn(wm[g], bp[g]).astype(BF16) for g in G]
    s_add = [_dot_tn(u0[g], bp[g]) + _dot_tn(vb[g], kp[g]) for g in G]

    for j in range(heads):
        s = state_ref[j]
        for q in range(n_chunks):
            g = j * n_chunks + q
            sb = s.astype(BF16)
            y = _dot_nt(rm[g], sb) + y0[g]
            s = s * pc[g] + _dot(sb, mp[g]) + s_add[g]
            mean = jnp.mean(y, axis=-1, keepdims=True)
            yc = y - mean
            var = jnp.mean(yc * yc, axis=-1, keepdims=True)
            yn = yc * lax.rsqrt(var + LNX_EPS) * par(lnw_ref, j) + par(lnb_ref, j)
            yf_ref[q * C:(q + 1) * C, j * N:(j + 1) * N] = (yn + tile(bonus_ref, j, q)) * tile(g_ref, j, q)
        state_ref[j] = s
    y_ref[...] = yf_ref[...].astype(y_ref.dtype)


def rwkv_scan(at, bt, kt, rt, v, g, bonus, pc, lnx_w, lnx_b, *, rows=SCAN_ROWS, width=SCAN_WIDTH):
    S, W = rt.shape
    rows = min(rows, S)
    hp = width // HEAD_N
    row = lambda x: x.reshape(1, W).astype(F32)
    seq = pl.BlockSpec((rows, width), lambda h, c: (c, h))
    par = pl.BlockSpec((1, width), lambda h, c: (0, h))
    return pl.pallas_call(
        _rwkv_scan_kernel,
        out_shape=jax.ShapeDtypeStruct((S, W), BF16),
        grid=(W // width, S // rows),
        in_specs=[seq] * 7 + [pl.BlockSpec((rows // CHUNK, 1, width), lambda h, c: (c, 0, h))] + [par] * 2,
        out_specs=seq,
        scratch_shapes=[pltpu.VMEM((hp, HEAD_N, HEAD_N), F32), pltpu.VMEM((rows, width), F32)],
        compiler_params=_params("parallel", "arbitrary"),
        name="rwkv_scan",
    )(at, bt, kt, rt, v, g, bonus, pc, row(lnx_w), row(lnx_b))


def _diff_proj_kernel(x_ref, g_ref, w_ref, pos_ref, freq_ref, o_ref, xs_ref, cos_ref, sin_ref):
    j = pl.program_id(1)
    half = DIFF_QK // 2

    def first_half(shape):
        return (lax.broadcasted_iota(jnp.int32, shape, 1) & (DIFF_QK - 1)) < half

    @pl.when(j == 0)
    def _():
        x = x_ref[...]
        x = x * lax.rsqrt(jnp.mean(x * x, axis=-1, keepdims=True) + NORM_EPS) * g_ref[...]
        xs_ref[...] = x.astype(BF16)
        ang = pos_ref[...] * freq_ref[...]
        cos_ref[...] = jnp.cos(ang)
        sin_ref[...] = jnp.where(first_half(ang.shape), -1.0, 1.0) * jnp.sin(ang)

    @pl.when(j < 2)
    def _():
        sc = jnp.where(j == 0, DIFF_QK ** -0.5, 1.0)
        cos = cos_ref[...]
        sin = sin_ref[...]
        fh = first_half(cos.shape)
        for c0 in range(0, o_ref.shape[1], MXU_WIDTH):
            acc = jnp.dot(xs_ref[...], w_ref[:, c0:c0 + MXU_WIDTH], preferred_element_type=F32)
            for b0 in range(0, MXU_WIDTH, LANES):
                x = acc[:, b0:b0 + LANES]
                partner = jnp.where(fh, pltpu.roll(x, LANES - half, axis=1), pltpu.roll(x, half, axis=1))
                o_ref[:, c0 + b0:c0 + b0 + LANES] = ((x * cos + partner * sin) * sc).astype(o_ref.dtype)

    @pl.when(j == 2)
    def _():
        o_ref[...] = jnp.dot(xs_ref[...], w_ref[...], preferred_element_type=F32).astype(o_ref.dtype)


def diff_proj(x, gain, w, positions, *, tm=1024):
    S, K = x.shape
    W = DIFF_WIDTH
    tm = min(tm, S)
    assert w.shape == (K, 3 * W) and S % tm == 0
    inv_freq = ROPE_THETA ** (-(jnp.arange(0, DIFF_QK, 2, dtype=F32) / DIFF_QK))
    freq = jnp.tile(inv_freq, LANES // (DIFF_QK // 2)).reshape(1, LANES)
    pos = positions.reshape(S, 1).astype(F32)
    return pl.pallas_call(
        _diff_proj_kernel,
        out_shape=jax.ShapeDtypeStruct((S, 3 * W), BF16),
        grid=(S // tm, 3),
        in_specs=[pl.BlockSpec((tm, K), lambda i, j: (i, 0)),
                  pl.BlockSpec((1, K), lambda i, j: (0, 0)),
                  pl.BlockSpec((K, W), lambda i, j: (0, j)),
                  pl.BlockSpec((tm, 1), lambda i, j: (i, 0)),
                  pl.BlockSpec((1, LANES), lambda i, j: (0, 0))],
        out_specs=pl.BlockSpec((tm, W), lambda i, j: (i, j)),
        scratch_shapes=[pltpu.VMEM((tm, K), BF16), pltpu.VMEM((tm, LANES), F32), pltpu.VMEM((tm, LANES), F32)],
        compiler_params=_params("parallel", "arbitrary"),
        name="diff_proj",
    )(x, gain.reshape(1, K).astype(F32), w, pos, freq)


def _diff_flash_kernel(qi_tab, ki_tab, q_ref, k_ref, v_ref, lam_ref, sw_ref, o_ref,
                       m_ref, l_ref, acc_ref, s_ref, p_ref, alpha_ref, *, tile, row_block, diag_parts, lambda_init):
    p = pl.program_id(1)
    qi = qi_tab[p]
    ki = ki_tab[p]

    @pl.when(ki == 0)
    def _():
        m_ref[...] = jnp.full_like(m_ref, -jnp.inf)
        l_ref[...] = jnp.zeros_like(l_ref)
        acc_ref[...] = jnp.zeros_like(acc_ref)

    def process(row0, nrows, kmax, masked, scores_first):
        rows = slice(row0, row0 + nrows)
        q = q_ref[rows, :]
        k = k_ref[0:kmax, :]
        v = v_ref[0:kmax, :]
        lane = lax.broadcasted_iota(jnp.int32, q.shape, 1)
        reps = kmax // LANES

        def buf(c):
            return c if scores_first else 0

        def scores(c):
            qc = jnp.where((lane >> (DIFF_QK.bit_length() - 1)) == c, q, jnp.zeros_like(q))
            s_ref[buf(c), 0:nrows, 0:kmax] = _dot_nt(qc, k)

        if scores_first:
            scores(0)
            scores(1)
        for c in range(2):
            if not scores_first:
                scores(c)
            for r0 in range(0, nrows, row_block):
                ar = slice(row0 + r0, row0 + r0 + row_block)
                lr = slice(r0, r0 + row_block)
                s = s_ref[buf(c), lr, 0:kmax]
                if masked:
                    qpos = lax.broadcasted_iota(jnp.int32, s.shape, 0) + (row0 + r0)
                    kpos = lax.broadcasted_iota(jnp.int32, s.shape, 1)
                    s = jnp.where(kpos <= qpos, s, -jnp.inf)
                m_old = m_ref[c, ar, :]
                m_new = jnp.maximum(m_old, jnp.max(s, axis=-1, keepdims=True))
                alpha = jnp.exp(m_old - m_new)
                pr = jnp.exp(s - jnp.concatenate([m_new] * reps, axis=1))
                l_ref[c, ar, :] = alpha * l_ref[c, ar, :] + jnp.sum(pr, axis=-1, keepdims=True)
                m_ref[c, ar, :] = m_new
                alpha_ref[buf(c), lr, :] = alpha
                p_ref[buf(c), lr, 0:kmax] = pr.astype(BF16)
            acc_ref[c, rows, :] = (alpha_ref[buf(c), 0:nrows, :] * acc_ref[c, rows, :]
                                   + _dot(p_ref[buf(c), 0:nrows, 0:kmax], v))

    @pl.when(ki < qi)
    def _():
        process(0, tile, tile, False, False)

    @pl.when(ki == qi)
    def _():
        part = tile // diag_parts
        for a in range(diag_parts):
            process(a * part, part, (a + 1) * part, True, True)
        lp = lam_ref[...]
        lam = (jnp.exp(jnp.sum(lp[0:1] * lp[1:2], axis=-1, keepdims=True))
               - jnp.exp(jnp.sum(lp[2:3] * lp[3:4], axis=-1, keepdims=True)) + lambda_init)
        o = acc_ref[0] / l_ref[0] - lam * (acc_ref[1] / l_ref[1])
        o = o * lax.rsqrt(jnp.mean(o * o, axis=-1, keepdims=True) + SUBLN_EPS)
        o_ref[...] = (o * sw_ref[...] * (1.0 - lambda_init)).astype(o_ref.dtype)


def diff_flash(qkv, lambdas, subln_w, lambda_init, *, tile=1024, row_block=32, diag_parts=2):
    S = qkv.shape[0]
    tile = min(tile, S)
    assert S % tile == 0 and (tile // diag_parts) % LANES == 0 and (tile // diag_parts) % row_block == 0
    pairs = [(qi, ki) for qi in range(S // tile) for ki in range(qi + 1)]
    qi_tab = jnp.asarray([p[0] for p in pairs], jnp.int32)
    ki_tab = jnp.asarray([p[1] for p in pairs], jnp.int32)
    grid_spec = pltpu.PrefetchScalarGridSpec(
        num_scalar_prefetch=2,
        grid=(DIFF_HEADS, len(pairs)),
        in_specs=[pl.BlockSpec((tile, LANES), lambda h, p, qt, kt: (qt[p], h)),
                  pl.BlockSpec((tile, LANES), lambda h, p, qt, kt: (kt[p], DIFF_HEADS + h)),
                  pl.BlockSpec((tile, LANES), lambda h, p, qt, kt: (kt[p], 2 * DIFF_HEADS + h)),
                  pl.BlockSpec((4, DIFF_QK), lambda h, p, qt, kt: (0, 0)),
                  pl.BlockSpec((1, DIFF_V), lambda h, p, qt, kt: (0, 0))],
        out_specs=pl.BlockSpec((tile, LANES), lambda h, p, qt, kt: (qt[p], h)),
        scratch_shapes=[pltpu.VMEM((2, tile, LANES), F32),
                        pltpu.VMEM((2, tile, LANES), F32),
                        pltpu.VMEM((2, tile, DIFF_V), F32),
                        pltpu.VMEM((2, tile, tile), F32),
                        pltpu.VMEM((2, tile, tile), BF16),
                        pltpu.VMEM((2, tile, LANES), F32)],
    )
    return pl.pallas_call(
        functools.partial(_diff_flash_kernel, tile=tile, row_block=row_block, diag_parts=diag_parts,
                          lambda_init=lambda_init),
        out_shape=jax.ShapeDtypeStruct((S, DIFF_WIDTH), BF16),
        grid_spec=grid_spec,
        compiler_params=_params("parallel", "arbitrary"),
        name="diff_flash",
    )(qi_tab, ki_tab, qkv, qkv, qkv, lambdas.astype(F32), subln_w.reshape(1, DIFF_V).astype(F32))


def _mem_attn_kernel(x_ref, g_ref, w_ref, k_ref, v_ref, o_ref, xs_ref):
    @pl.when(pl.program_id(1) == 0)
    def _():
        x = x_ref[...]
        x = x * lax.rsqrt(jnp.mean(x * x, axis=-1, keepdims=True) + NORM_EPS) * g_ref[...]
        xs_ref[...] = x.astype(BF16)

    q = jnp.dot(xs_ref[...], w_ref[...], preferred_element_type=F32).astype(BF16)
    s = _dot_nt(q, k_ref[...]) * (q.shape[1] ** -0.5)
    s = s - jnp.max(s, axis=-1, keepdims=True)
    e = jnp.exp(s)
    pr = e / jnp.sum(e, axis=-1, keepdims=True)
    o_ref[...] = _dot(pr.astype(BF16), v_ref[...]).astype(o_ref.dtype)


def mem_attn(x, gain, wq, k, v, *, tm=1024):
    S, D = x.shape
    M = k.shape[0]
    hd = D // MEM_HEADS
    tm = min(tm, S)
    return pl.pallas_call(
        _mem_attn_kernel,
        out_shape=jax.ShapeDtypeStruct((S, D), BF16),
        grid=(S // tm, MEM_HEADS),
        in_specs=[pl.BlockSpec((tm, D), lambda i, j: (i, 0)),
                  pl.BlockSpec((1, D), lambda i, j: (0, 0)),
                  pl.BlockSpec((D, hd), lambda i, j: (0, j)),
                  pl.BlockSpec((M, hd), lambda i, j: (0, j)),
                  pl.BlockSpec((M, hd), lambda i, j: (0, j))],
        out_specs=pl.BlockSpec((tm, hd), lambda i, j: (i, j)),
        scratch_shapes=[pltpu.VMEM((tm, D), BF16)],
        compiler_params=_params("parallel", "arbitrary"),
        name="mem_attn",
    )(x, gain.reshape(1, D).astype(F32), wq, k, v)


def _first_argmax(x, lane, big):
    m = jnp.max(x, axis=-1, keepdims=True)
    idx = jnp.min(jnp.where(x == m, lane, big), axis=-1, keepdims=True)
    return m, idx


def _router_kernel(h_ref, gain_ref, wg_ref, bg_ref, we_ref, be_ref, hn_ref, eid_ref, ew_ref, cnt_ref):
    @pl.when(pl.program_id(0) == 0)
    def _():
        cnt_ref[...] = jnp.zeros_like(cnt_ref)

    x = h_ref[...]
    hn = x * lax.rsqrt(jnp.mean(x * x, axis=-1, keepdims=True) + NORM_EPS) * gain_ref[...]
    hn_ref[...] = hn
    hn_hi = hn.astype(BF16)
    hn_lo = (hn - hn_hi.astype(F32)).astype(BF16)

    def logits(w_ref, b_ref):
        w = w_ref[...]
        w_hi = w.astype(BF16)
        w_lo = (w - w_hi.astype(F32)).astype(BF16)
        return _dot(hn_hi, w_hi) + (_dot(hn_hi, w_lo) + _dot(hn_lo, w_hi)) + b_ref[...]
    g_logits = logits(wg_ref, bg_ref)
    e_logits = logits(we_ref, be_ref)
    tm = x.shape[0]
    lane_g = lax.broadcasted_iota(jnp.int32, (tm, N_GROUPS), 1)
    g_max, g_idx = _first_argmax(g_logits, lane_g, N_GROUPS)
    g_w = 1.0 / jnp.sum(jnp.exp(g_logits - g_max), axis=-1, keepdims=True)
    lane_e = lax.broadcasted_iota(jnp.int32, (tm, N_EXPERTS), 1)
    in_group = (lane_e // EXPERTS_PER_GROUP) == g_idx
    el = jnp.where(in_group, e_logits, -jnp.inf)
    e_max = jnp.max(el, axis=-1, keepdims=True)
    ex = jnp.exp(el - e_max)
    prob = ex / jnp.sum(ex, axis=-1, keepdims=True)
    prob = jnp.where(in_group, prob, -1.0)
    p1, i1 = _first_argmax(prob, lane_e, N_EXPERTS)
    p2, i2 = _first_argmax(jnp.where(lane_e == i1, -1.0, prob), lane_e, N_EXPERTS)
    tot = p1 + p2
    lane_o = lax.broadcasted_iota(jnp.int32, (tm, LANES), 1)
    eid_ref[...] = jnp.where(lane_o == 0, i1, jnp.where(lane_o == 1, i2, 0))
    ew_ref[...] = jnp.where(lane_o == 0, g_w * (p1 / tot), jnp.where(lane_o == 1, g_w * (p2 / tot), 0.0))
    chosen = ((lane_o == i1) | (lane_o == i2)).astype(F32)
    cnt_ref[...] = cnt_ref[...] + jnp.sum(chosen, axis=0, keepdims=True).astype(jnp.int32)


def router(h, gain, wg, bg, we, be, *, tm=512):
    S, D = h.shape
    tm = min(tm, S)
    full = lambda a: pl.BlockSpec(a.shape, lambda i: (0, 0))
    args = [h, gain.reshape(1, D).astype(F32), wg.astype(F32), bg.reshape(1, -1).astype(F32),
            we.astype(F32), be.reshape(1, -1).astype(F32)]
    return pl.pallas_call(
        _router_kernel,
        out_shape=[jax.ShapeDtypeStruct((S, D), F32),
                   jax.ShapeDtypeStruct((S, LANES), jnp.int32),
                   jax.ShapeDtypeStruct((S, LANES), F32),
                   jax.ShapeDtypeStruct((1, LANES), jnp.int32)],
        grid=(S // tm,),
        in_specs=[pl.BlockSpec((tm, D), lambda i: (i, 0))] + [full(a) for a in args[1:]],
        out_specs=[pl.BlockSpec((tm, D), lambda i: (i, 0)),
                   pl.BlockSpec((tm, LANES), lambda i: (i, 0)),
                   pl.BlockSpec((tm, LANES), lambda i: (i, 0)),
                   pl.BlockSpec((1, LANES), lambda i: (0, 0))],
        compiler_params=_params("arbitrary"),
        name="router",
    )(*args)


def _plan_kernel(cnt_ref, blk_e_ref, blk_n_ref, start_ref):
    nb = blk_e_ref.shape[0]
    sh = MOE_ROWS.bit_length() - 1

    def per_expert(e, b):
        n = cnt_ref[e]
        start_ref[e] = b

        def per_block(t, _):
            blk_e_ref[b + t] = e
            blk_n_ref[b + t] = jnp.minimum(n - t * MOE_ROWS, MOE_ROWS)
            return 0
        k = (n + (MOE_ROWS - 1)) >> sh
        lax.fori_loop(0, k, per_block, 0)
        return b + k
    total = lax.fori_loop(0, N_EXPERTS, per_expert, 0)

    def rest(e, _):
        start_ref[e] = total
        return 0
    lax.fori_loop(N_EXPERTS, start_ref.shape[0], rest, 0)
    last_e = blk_e_ref[jnp.maximum(total - 1, 0)]

    def tail(b, _):
        blk_e_ref[b] = last_e
        blk_n_ref[b] = 0
        return 0
    lax.fori_loop(total, nb, tail, 0)


def block_plan(counts, n_assign):
    nb = (n_assign + N_EXPERTS * (MOE_ROWS - 1)) // MOE_ROWS
    smem = pl.BlockSpec(memory_space=pltpu.SMEM)
    return pl.pallas_call(
        _plan_kernel,
        out_shape=[jax.ShapeDtypeStruct((nb,), jnp.int32), jax.ShapeDtypeStruct((nb,), jnp.int32),
                   jax.ShapeDtypeStruct((LANES,), jnp.int32)],
        in_specs=[smem],
        out_specs=[smem, smem, smem],
        name="block_plan",
    )(counts.reshape(LANES))


def _dest_kernel(eid_ref, start_ref, dest_ref, carry_ref):
    @pl.when(pl.program_id(0) == 0)
    def _():
        carry_ref[...] = jnp.zeros_like(carry_ref)

    eid = eid_ref[...]
    tm = eid.shape[0]
    lane = lax.broadcasted_iota(jnp.int32, (tm, LANES), 1)
    oh0 = lane == eid[:, 0:1]
    oh1 = lane == eid[:, 1:2]
    both = jnp.where(oh0 | oh1, 1.0, 0.0).astype(BF16)
    r = lax.broadcasted_iota(jnp.int32, (tm, tm), 0)
    c = lax.broadcasted_iota(jnp.int32, (tm, tm), 1)
    before = _dot(jnp.where(r > c, 1.0, 0.0).astype(BF16), both) + carry_ref[...]
    pos = before + (start_ref[...] * MOE_ROWS).astype(F32)
    d0 = jnp.sum(jnp.where(oh0, pos, 0.0), axis=-1, keepdims=True).astype(jnp.int32)
    d1 = jnp.sum(jnp.where(oh1, pos, 0.0), axis=-1, keepdims=True).astype(jnp.int32)
    dest_ref[...] = jnp.where(lane == 0, d0, jnp.where(lane == 1, d1, 0))
    carry_ref[...] = carry_ref[...] + jnp.sum(both.astype(F32), axis=0, keepdims=True)


def assignment_dest(eid, blk_start, *, tm=512):
    S = eid.shape[0]
    tm = min(tm, S)
    return pl.pallas_call(
        _dest_kernel,
        out_shape=jax.ShapeDtypeStruct((S, LANES), jnp.int32),
        grid=(S // tm,),
        in_specs=[pl.BlockSpec((tm, LANES), lambda i: (i, 0)), pl.BlockSpec((1, LANES), lambda i: (0, 0))],
        out_specs=pl.BlockSpec((tm, LANES), lambda i: (i, 0)),
        scratch_shapes=[pltpu.VMEM((1, LANES), F32)],
        compiler_params=_params("arbitrary"),
        name="assignment_dest",
    )(eid, blk_start.reshape(1, LANES))


def _for_rows(n, fn):
    sh = MOE_DMA_UNROLL.bit_length() - 1

    def group(t, _):
        for u in range(MOE_DMA_UNROLL):
            fn(t * MOE_DMA_UNROLL + u)
        return 0
    lax.fori_loop(0, n >> sh, group, 0)

    def one(r, _):
        fn(r)
        return 0
    lax.fori_loop((n >> sh) << sh, n, one, 0)


def _experts_kernel(blk_e, blk_n, dest, hn_hbm, wg_hbm, wu_hbm, wd_hbm, out_hbm,
                    xg_ref, xb_ref, acc_ref, wg_ring, wu_ring, wd_ring, wgb_ref, wub_ref, wdb_ref, inv_ref,
                    gsem, ssem, wsem):
    i = pl.program_id(0)
    j = pl.program_id(1)
    nb = pl.num_programs(0)
    nsplit = pl.num_programs(1)
    n_tok = hn_hbm.shape[0]
    k_shift = TOP_K.bit_length() - 1
    de = wgb_ref.shape[1]
    n_slots = wg_ring.shape[0]

    def weight_copies(b, jj):
        slot = lax.rem(b * MOE_SPLIT + jj, n_slots)
        e = blk_e[b]
        c0 = pl.multiple_of(jj * de, de)
        return (pltpu.make_async_copy(wg_hbm.at[e, :, pl.ds(c0, de)], wg_ring.at[slot], wsem.at[slot]),
                pltpu.make_async_copy(wu_hbm.at[e, :, pl.ds(c0, de)], wu_ring.at[slot], wsem.at[slot]),
                pltpu.make_async_copy(wd_hbm.at[e, pl.ds(c0, de), :], wd_ring.at[slot], wsem.at[slot]))

    def start_weights(t):
        b = lax.div(t, MOE_SPLIT)
        jj = lax.rem(t, MOE_SPLIT)
        bb = jnp.minimum(b, nb - 1)

        @pl.when((b < nb) & (blk_n[bb] > 0))
        def _():
            for cp in weight_copies(bb, jj):
                cp.start()

    def gather(b, r):
        tok = inv_ref[b * MOE_ROWS + r] >> k_shift
        return pltpu.make_async_copy(hn_hbm.at[pl.ds(tok, 1)], xg_ref.at[pl.ds(r, 1)], gsem)

    def scatter(b, r):
        a = inv_ref[b * MOE_ROWS + r]
        row = (a & (TOP_K - 1)) * n_tok + (a >> k_shift)
        return pltpu.make_async_copy(acc_ref.at[pl.ds(r, 1)], out_hbm.at[pl.ds(row, 1)], ssem)

    def n_gather(b):
        return pl.multiple_of(((blk_n[b] + 7) >> 3) << 3, 8)

    def start_gather(b):
        def pad(r, _):
            inv_ref[b * MOE_ROWS + r] = 0
            return 0
        lax.fori_loop(blk_n[b], n_gather(b), pad, 0)
        _for_rows(n_gather(b), lambda r: gather(b, r).start(priority=1))

    def wait_gather(b):
        n = n_gather(b)

        @pl.when(n > 0)
        def _():
            pltpu.make_async_copy(hn_hbm.at[pl.ds(0, n)], xg_ref.at[pl.ds(0, n)], gsem).wait()

    def start_scatter(b):
        _for_rows(blk_n[b], lambda r: scatter(b, r).start(priority=1))

    def wait_scatter(b):
        n8 = pl.multiple_of((blk_n[b] >> 3) << 3, 8)

        @pl.when(n8 > 0)
        def _():
            pltpu.make_async_copy(acc_ref.at[pl.ds(0, n8)], out_hbm.at[pl.ds(0, n8)], ssem).wait()

        def one(r, _):
            scatter(b, r).wait()
            return 0
        lax.fori_loop(n8, blk_n[b], one, 0)

    n_units = (blk_n[i] + (MOE_UNIT - 1)) >> (MOE_UNIT.bit_length() - 1)

    @pl.when(j == 0)
    def _():
        @pl.when(i == 0)
        def _():
            def invert(a):
                inv_ref[dest[a]] = a
            _for_rows(dest.shape[0], invert)
            xg_ref[...] = jnp.zeros_like(xg_ref)
            start_gather(0)
        wait_gather(i)

        def cast(sb, _):
            rs = pl.ds(pl.multiple_of(sb * MOE_UNIT, MOE_UNIT), MOE_UNIT)
            xb_ref[rs, :] = xg_ref[rs, :].astype(BF16)
            return 0
        lax.fori_loop(0, n_units, cast, 0)

        @pl.when(i + 1 < nb)
        def _():
            start_gather(i + 1)

        @pl.when(i > 0)
        def _():
            wait_scatter(i - 1)

    t = i * MOE_SPLIT + j

    @pl.when(t == 0)
    def _():
        for t0 in range(n_slots):
            start_weights(jnp.int32(t0))

    @pl.when(n_units > 0)
    def _():
        for cp in weight_copies(i, j):
            cp.wait()
        slot = lax.rem(t, n_slots)
        wgb_ref[...] = wg_ring[slot].astype(BF16)
        wub_ref[...] = wu_ring[slot].astype(BF16)
        wdb_ref[...] = wd_ring[slot].astype(BF16)

    start_weights(t + n_slots)

    def compute(row0, nrows):
        rs = pl.ds(row0, nrows)
        xb = xb_ref[rs, :]
        gate = _dot(xb, wgb_ref[...])
        hid = (gate * _sigmoid(gate)) * _dot(xb, wub_ref[...])
        part = _dot(hid.astype(BF16), wdb_ref[...])

        @pl.when(j == 0)
        def _():
            acc_ref[rs, :] = part

        @pl.when(j > 0)
        def _():
            acc_ref[rs, :] = acc_ref[rs, :] + part

    for units in range(1, MOE_ROWS // MOE_UNIT + 1):
        @pl.when(n_units == units)
        def _():
            compute(0, units * MOE_UNIT)

    @pl.when(j == nsplit - 1)
    def _():
        start_scatter(i)

        @pl.when(i == nb - 1)
        def _():
            wait_scatter(i)


def experts(hn, blk_e, blk_n, dest, w_gate, w_up, w_down):
    S, D = hn.shape
    E, _, DE = w_gate.shape
    nb = blk_e.shape[0]
    de = DE // MOE_SPLIT
    any_space = pl.BlockSpec(memory_space=pl.ANY)
    grid_spec = pltpu.PrefetchScalarGridSpec(
        num_scalar_prefetch=3,
        grid=(nb, MOE_SPLIT),
        in_specs=[any_space] * 4,
        out_specs=any_space,
        scratch_shapes=[pltpu.VMEM((MOE_ROWS, D), F32),
                        pltpu.VMEM((MOE_ROWS, D), BF16),
                        pltpu.VMEM((MOE_ROWS, D), F32),
                        pltpu.VMEM((MOE_WEIGHT_SLOTS, D, de), F32),
                        pltpu.VMEM((MOE_WEIGHT_SLOTS, D, de), F32),
                        pltpu.VMEM((MOE_WEIGHT_SLOTS, de, D), F32),
                        pltpu.VMEM((D, de), BF16),
                        pltpu.VMEM((D, de), BF16),
                        pltpu.VMEM((de, D), BF16),
                        pltpu.SMEM((nb * MOE_ROWS,), jnp.int32),
                        pltpu.SemaphoreType.DMA(()),
                        pltpu.SemaphoreType.DMA(()),
                        pltpu.SemaphoreType.DMA((MOE_WEIGHT_SLOTS,))],
    )
    return pl.pallas_call(
        _experts_kernel,
        out_shape=jax.ShapeDtypeStruct((TOP_K * S, D), F32),
        grid_spec=grid_spec,
        compiler_params=_params("arbitrary", "arbitrary"),
        name="experts",
    )(blk_e, blk_n, dest, hn, w_gate, w_up, w_down)


def _combine_kernel(*refs, has_norm):
    h_ref, y0_ref, y1_ref, w_ref = refs[:4]
    o_ref = refs[-1]
    w = w_ref[...]
    x = h_ref[...] + (y0_ref[0] * w[:, 0:1] + y1_ref[0] * w[:, 1:2])
    if has_norm:
        x = x * lax.rsqrt(jnp.mean(x * x, axis=-1, keepdims=True) + NORM_EPS) * refs[4][...]
    o_ref[...] = x


def combine(h, y_slots, slot_w, gain=None, *, tm=512):
    S, D = h.shape
    tm = min(tm, S)
    y3 = y_slots.reshape(TOP_K, S, D)
    in_specs = [pl.BlockSpec((tm, D), lambda i: (i, 0)),
                pl.BlockSpec((1, tm, D), lambda i: (0, i, 0)),
                pl.BlockSpec((1, tm, D), lambda i: (1, i, 0)),
                pl.BlockSpec((tm, LANES), lambda i: (i, 0))]
    args = [h, y3, y3, slot_w]
    if gain is not None:
        in_specs.append(pl.BlockSpec((1, D), lambda i: (0, 0)))
        args.append(gain.reshape(1, D).astype(F32))
    return pl.pallas_call(
        functools.partial(_combine_kernel, has_norm=gain is not None),
        out_shape=jax.ShapeDtypeStruct((S, D), F32),
        grid=(S // tm,),
        in_specs=in_specs,
        out_specs=pl.BlockSpec((tm, D), lambda i: (i, 0)),
        compiler_params=_params("parallel"),
        name="combine",
    )(*args)


def moe(h, gain, wg, bg, we, be, w_gate, w_up, w_down, final_gain=None):
    S = h.shape[0]
    hn, eid, slot_w, counts = router(h, gain, wg, bg, we, be)
    blk_e, blk_n, blk_start = block_plan(counts, S * TOP_K)
    dest = assignment_dest(eid, blk_start)[:, :TOP_K].reshape(S * TOP_K)
    y_slots = experts(hn, blk_e, blk_n, dest, w_gate, w_up, w_down)
    return combine(h, y_slots, slot_w, final_gain)


def kernel(x, mem, positions, mix_norm, w_in, shift_mu, decay_w0, decay_w2, aaa_a0, aaa_a2, gate_g2, k_k, k_a,
           r_k, lnx_w, lnx_b, diff_lambda, subln_w, w_out, mem_q_norm, mem_kv_norm, wq_mem, wk_mem, wv_mem,
           wo_mem, moe_norm, router_group_w, router_group_b, router_expert_w, router_expert_b, expert_gate,
           expert_up, expert_down, final_norm):
    B, S, D = x.shape
    depth = w_in.shape[0]
    outs = []
    for b in range(B):
        h = x[b]
        memb = mem[b]
        for l in range(depth):
            lambda_init = 0.8 - 0.6 * math.exp(-0.3 * l)
            w_in_b = w_in[l].astype(BF16)
            proj_r = matmul(h, w_in_b[:, :RWKV_COLS], gain=mix_norm[l], tm=1024, tn=RWKV_COLS // 2)
            qkv = diff_proj(h, mix_norm[l], w_in_b[:, RWKV_COLS:], positions[b])
            pre = rwkv_prep(proj_r, shift_mu[l], decay_w0[l], decay_w2[l], aaa_a0[l], aaa_a2[l], gate_g2[l],
                            k_k[l], k_a[l], r_k[l])
            y_rwkv = rwkv_scan(*pre, lnx_w[l], lnx_b[l])
            y_diff = diff_flash(qkv, diff_lambda[l], subln_w[l], lambda_init)
            h = matmul([y_rwkv, y_diff], w_out[l].astype(BF16), residual=h, tm=1024, tn=1024)
            km = matmul(memb, wk_mem[l].astype(BF16), gain=mem_kv_norm[l], out_dtype=BF16)
            vm = matmul(memb, wv_mem[l].astype(BF16), gain=mem_kv_norm[l], out_dtype=BF16)
            o = mem_attn(h, mem_q_norm[l], wq_mem[l].astype(BF16), km, vm)
            h = matmul(o, wo_mem[l].astype(BF16), residual=h, tm=1024, tn=1024)
            h = moe(h, moe_norm[l], router_group_w[l], router_group_b[l], router_expert_w[l], router_expert_b[l],
                    expert_gate[l], expert_up[l], expert_down[l], final_norm if l == depth - 1 else None)
        outs.append(h)
    return jnp.stack(outs, axis=0)
```
